```python
import math
import jax, jax.numpy as jnp
from jax import lax
import numpy as np

D_MODEL = 1024
BATCH = 2
SEQ = 8192
DEPTH = 2

CTX_LEN = 256
GRID_W = 64
EPS = 1e-6

DN_HEADS = 4
DN_HEAD_DIM = 128
DN_WIDTH = DN_HEADS * DN_HEAD_DIM
SHORT_CONV = 4
DN_CHUNK = 64

POOL_WINDOWS = (2, 4, 8, 16)
POOL_GROUPS = 4
POOL_WIDTH = D_MODEL // 2
POOL_GW = POOL_WIDTH // POOL_GROUPS

IN_SIZES = (3 * DN_WIDTH, DN_WIDTH, POOL_WIDTH, 2 * DN_HEADS, 2 * DN_HEADS, D_MODEL, D_MODEL)
IN_COLS = 3 * DN_WIDTH + DN_WIDTH + POOL_WIDTH + 4 * DN_HEADS + 2 * D_MODEL

PEER_HEADS = 8
N_KEYS = 128
N_EXPERTS = N_KEYS * N_KEYS
PEER_TOPK = 16
PEER_QDIM = 256
PEER_HALF = PEER_QDIM // 2
PEER_BLOCK = 128

kernel_name = 'hybrid_deltanet_pool_peer_diffusion'


def rmsnorm(x, w):
    xf = x.astype(jnp.float32)
    y = xf * lax.rsqrt(jnp.mean(xf * xf, axis=-1, keepdims=True) + EPS) * w.astype(jnp.float32)
    return y.astype(x.dtype)


def modulate(h, shift, scale):
    return h * (1 + scale) + shift


def l2norm(x):
    return x * lax.rsqrt(jnp.sum(x * x, axis=-1, keepdims=True) + EPS)


def split_columns(p):
    return jnp.split(p, np.cumsum(IN_SIZES)[:-1].tolist(), axis=-1)


def short_conv_silu(x, w):
    T = x.shape[1]
    left = SHORT_CONV // 2
    xp = jnp.pad(x, ((0, 0), (left, SHORT_CONV - 1 - left), (0, 0)))
    y = xp[:, 0:T, :] * w[0]
    for j in range(1, SHORT_CONV):
        y = y + xp[:, j:j + T, :] * w[j]
    return jax.nn.silu(y)


def dn_inputs(qkv, b_raw, a_raw, conv_w, a_log, dt_bias):
    Bn, T, _ = qkv.shape
    qkv = short_conv_silu(qkv, conv_w).astype(jnp.float32)
    q, k, v = jnp.split(qkv, 3, axis=-1)
    heads = lambda t: t.reshape(Bn, T, DN_HEADS, DN_HEAD_DIM).transpose(0, 2, 1, 3)
    q, k, v = l2norm(heads(q)), l2norm(heads(k)), heads(v)
    dirs = lambda t: t.astype(jnp.float32).reshape(Bn, T, 2, DN_HEADS).transpose(2, 0, 3, 1)
    beta = jax.nn.sigmoid(dirs(b_raw))
    a_f = a_log.astype(jnp.float32)[:, None, :, None]
    dtb = dt_bias.astype(jnp.float32)[:, None, :, None]
    g = -jnp.exp(a_f) * jax.nn.softplus(dirs(a_raw) + dtb)
    return (q, k, v, beta, g)


def gated_delta_rule(q, k, v, beta, g, state):
    Bn, Hn, T, DK = q.shape
    DV = v.shape[-1]
    C = DN_CHUNK
    n = T // C
    q = q * (DK ** -0.5)
    chunks = lambda t: t.reshape((Bn, Hn, n, C) + t.shape[3:])
    qc, kc, vc, bc = chunks(q), chunks(k), chunks(v), chunks(beta)
    gcum = jnp.cumsum(chunks(g), axis=-1)
    incl = jnp.tril(jnp.ones((C, C), bool))
    strict = jnp.tril(jnp.ones((C, C), bool), -1)
    decay = jnp.exp(jnp.where(incl, gcum[..., :, None] - gcum[..., None, :], -jnp.inf))
    kb = kc * bc[..., None]
    lower = jnp.where(strict, jnp.einsum('bhnid,bhnjd->bhnij', kb, kc) * decay, 0.0)
    eye = jnp.eye(C, dtype=jnp.float32)
    tinv = lax.linalg.triangular_solve(eye + lower, jnp.broadcast_to(eye, lower.shape),
                                       left_side=True, lower=True, unit_diagonal=True)
    w = tinv @ (kb * jnp.exp(gcum)[..., None])
    u = tinv @ (vc * bc[..., None])
    qg = qc * jnp.exp(gcum)[..., None]
    intra = jnp.einsum('bhnid,bhnjd->bhnij', qc, kc) * decay
    g_last = gcum[..., -1]
    kdec = kc * jnp.exp(g_last[..., None] - gcum)[..., None]

    def step(S, xs):
        w_i, u_i, qg_i, intra_i, kdec_i, gl_i = xs
        v_new = u_i - w_i @ S
        o_i = qg_i @ S + intra_i @ v_new
        S = S * jnp.exp(gl_i)[..., None, None] + jnp.swapaxes(kdec_i, -1, -2) @ v_new
        return S, o_i

    mv = lambda t: jnp.moveaxis(t, 2, 0)
    S, o = lax.scan(step, state, (mv(w), mv(u), mv(qg), mv(intra), mv(kdec), mv(g_last)))
    o = jnp.moveaxis(o, 0, 2).reshape(Bn, Hn, T, DV)
    return o, S


def bidirectional_delta(lat, ctx):
    ql, kl, vl, bl, gl = lat
    qc, kc, vc, bc, g_c = ctx
    s0 = jnp.zeros(qc.shape[:2] + (DN_HEAD_DIM, DN_HEAD_DIM), jnp.float32)
    rev = lambda t: jnp.flip(t, axis=2)
    oc_f, sc_f = gated_delta_rule(qc, kc, vc, bc[0], g_c[0], s0)
    ol_f, _ = gated_delta_rule(ql, kl, vl, bl[0], gl[0], sc_f)
    oc_b, sc_b = gated_delta_rule(rev(qc), rev(kc), rev(vc), rev(bc[1]), rev(g_c[1]), s0)
    ol_b, _ = gated_delta_rule(rev(ql), rev(kl), rev(vl), rev(bl[1]), rev(gl[1]), sc_b)
    return ol_f + rev(ol_b), oc_f + rev(oc_b)


def dn_output(o, z, onorm):
    Bn, Hn, T, DV = o.shape
    o = jnp.swapaxes(o, 1, 2)
    zf = z.astype(jnp.float32).reshape(Bn, T, Hn, DV)
    y = o * lax.rsqrt(jnp.mean(o * o, axis=-1, keepdims=True) + EPS) * onorm.astype(jnp.float32) * jax.nn.silu(zf)
    return y.reshape(Bn, T, Hn * DV).astype(z.dtype)


def pool_minus_self(x, window):
    L = x.shape[-2]
    xf = x.astype(jnp.float32)
    cs = jnp.cumsum(xf, axis=-2)
    P = jnp.concatenate([jnp.zeros_like(cs[..., :1, :]), cs], axis=-2)
    t = jnp.arange(L)
    lo = jnp.clip(t - window // 2, 0, L)
    hi = jnp.clip(t + window - window // 2, 0, L)
    s = jnp.take(P, hi, axis=-2) - jnp.take(P, lo, axis=-2)
    cnt = (hi - lo).astype(jnp.float32)[:, None]
    return s / cnt - xf


def multiscale_pool(p, pool_w, pool_scale, rows):
    Bn, T, _ = p.shape
    if rows is None:
        pg = p.reshape(Bn, T, POOL_GROUPS, POOL_GW)
    else:
        pg = p.reshape(Bn, rows, GRID_W, POOL_GROUPS, POOL_GW)
    y = jnp.stack([pool_minus_self(pg[..., gi, :], win) for gi, win in enumerate(POOL_WINDOWS)], axis=-2)
    y = y.reshape(Bn, T, POOL_GROUPS, POOL_GW)
    y = jnp.einsum('btgc,gcd->btgd', y, pool_w).reshape(Bn, T, POOL_WIDTH)
    return (y * pool_scale).astype(p.dtype)


def merge_branches(y_dn, y_pool, gate_dn, gate_pool, w_up_dn, w_up_pool, w_out):
    m = jax.nn.sigmoid(gate_dn) * (y_dn @ w_up_dn) + jax.nn.sigmoid(gate_pool) * (y_pool @ w_up_pool)
    return m @ w_out


def peer_ffn(h, wq, keys, u_tab, v_tab):
    Bn, T, D = h.shape
    tok = h.reshape(-1, PEER_BLOCK, D)

    def block(hb):
        q = (hb @ wq).reshape(PEER_BLOCK, PEER_HEADS, 2, PEER_HALF)
        s = jnp.einsum('nhpd,hpkd->nhpk', q, keys)
        s1, i1 = lax.top_k(s[:, :, 0], PEER_TOPK)
        s2, i2 = lax.top_k(s[:, :, 1], PEER_TOPK)
        cand_s = (s1[..., :, None] + s2[..., None, :]).reshape(PEER_BLOCK, PEER_HEADS, PEER_TOPK * PEER_TOPK)
        cand_i = (i1[..., :, None] * N_KEYS + i2[..., None, :]).reshape(PEER_BLOCK, PEER_HEADS, PEER_TOPK * PEER_TOPK)
        top_s, pos = lax.top_k(cand_s, PEER_TOPK)
        e = jnp.take_along_axis(cand_i, pos, axis=-1)
        gsc = jax.nn.softmax(top_s.astype(jnp.float32), axis=-1)
        a = jnp.einsum('nhkd,nd->nhk', u_tab[e], hb)
        act = (gsc * jax.nn.gelu(a.astype(jnp.float32), approximate=False)).astype(hb.dtype)
        return jnp.einsum('nhk,nhkd->nd', act, v_tab[e])

    return lax.map(block, tok).reshape(Bn, T, D)


def hybrid_layer(x, xc, mod, modc, w_in, conv_w, a_log, dt_bias, dn_out_norm, pool_w, pool_scale,
                 w_up_dn, w_up_pool, w_out, norm_mix, norm_ffn, peer_wq, peer_keys, peer_u, peer_v, update_ctx):
    rows = x.shape[1] // GRID_W
    sh1, sc1, gt1, sh2, sc2, gt2 = jnp.split(mod, 6, axis=-1)
    csh1, csc1, cgt1, csh2, csc2, cgt2 = jnp.split(modc, 6, axis=-1)
    qkv, z, pin, b_raw, a_raw, gd, gp = split_columns(modulate(rmsnorm(x, norm_mix), sh1, sc1) @ w_in)
    qkv_c, z_c, pin_c, b_raw_c, a_raw_c, gd_c, gp_c = split_columns(modulate(rmsnorm(xc, norm_mix), csh1, csc1) @ w_in)
    o_lat, o_ctx = bidirectional_delta(dn_inputs(qkv, b_raw, a_raw, conv_w, a_log, dt_bias),
                                       dn_inputs(qkv_c, b_raw_c, a_raw_c, conv_w, a_log, dt_bias))
    y_dn = dn_output(o_lat, z, dn_out_norm)
    y_pool = multiscale_pool(pin, pool_w, pool_scale, rows)
    x = x + gt1 * merge_branches(y_dn, y_pool, gd, gp, w_up_dn, w_up_pool, w_out)
    x = x + gt2 * peer_ffn(modulate(rmsnorm(x, norm_ffn), sh2, sc2), peer_wq, peer_keys, peer_u, peer_v)
    if update_ctx:
        y_dn_c = dn_output(o_ctx, z_c, dn_out_norm)
        y_pool_c = multiscale_pool(pin_c, pool_w, pool_scale, None)
        xc = xc + cgt1 * merge_branches(y_dn_c, y_pool_c, gd_c, gp_c, w_up_dn, w_up_pool, w_out)
        xc = xc + cgt2 * peer_ffn(modulate(rmsnorm(xc, norm_ffn), csh2, csc2), peer_wq, peer_keys, peer_u, peer_v)
    return x, xc


def setup_inputs(seed: int = 0) -> dict:
    key = jax.random.key(seed)
    ks = jax.random.split(key, 24)
    f32 = jnp.float32
    nrm = lambda k, shape, scale: jax.random.normal(k, shape, f32) * scale
    gain = lambda k, shape: 1.0 + 0.02 * jax.random.normal(k, shape, f32)
    dt = jnp.exp(jax.random.uniform(ks[10], (DEPTH, 2, DN_HEADS), f32, math.log(1e-3), math.log(1e-1)))
    return {
        'x': nrm(ks[0], (BATCH, SEQ, D_MODEL), 1.0),
        'c': nrm(ks[1], (BATCH, D_MODEL), 1.0),
        'ctx': nrm(ks[2], (BATCH, CTX_LEN, D_MODEL), 1.0),
        'c_ctx': nrm(ks[3], (D_MODEL,), 1.0),
        'w_mod': nrm(ks[4], (DEPTH, D_MODEL, 6 * D_MODEL), 0.5 * D_MODEL ** -0.5),
        'b_mod': nrm(ks[5], (DEPTH, 6 * D_MODEL), 0.02),
        'norm_mix': gain(ks[6], (DEPTH, D_MODEL)),
        'w_in': nrm(ks[7], (DEPTH, D_MODEL, IN_COLS), D_MODEL ** -0.5),
        'conv_w': nrm(ks[8], (DEPTH, SHORT_CONV, 3 * DN_WIDTH), SHORT_CONV ** -0.5),
        'a_log': jnp.log(jax.random.uniform(ks[9], (DEPTH, 2, DN_HEADS), f32, 1.0, 16.0)),
        'dt_bias': dt + jnp.log(-jnp.expm1(-dt)),
        'dn_out_norm': gain(ks[11], (DEPTH, DN_HEAD_DIM)),
        'pool_w': nrm(ks[12], (DEPTH, POOL_GROUPS, POOL_GW, POOL_GW), POOL_GW ** -0.5),
        'pool_scale': gain(ks[13], (DEPTH, POOL_WIDTH)),
        'w_up_dn': nrm(ks[14], (DEPTH, DN_WIDTH, D_MODEL), DN_WIDTH ** -0.5),
        'w_up_pool': nrm(ks[15], (DEPTH, POOL_WIDTH, D_MODEL), POOL_WIDTH ** -0.5),
        'w_out': nrm(ks[16], (DEPTH, D_MODEL, D_MODEL), D_MODEL ** -0.5),
        'norm_ffn': gain(ks[17], (DEPTH, D_MODEL)),
        'peer_wq': nrm(ks[18], (DEPTH, D_MODEL, PEER_HEADS * PEER_QDIM), D_MODEL ** -0.5),
        'peer_keys': nrm(ks[19], (DEPTH, PEER_HEADS, 2, N_KEYS, PEER_HALF), PEER_HALF ** -0.5),
        'peer_u': nrm(ks[20], (DEPTH, N_EXPERTS, D_MODEL), D_MODEL ** -0.5),
        'peer_v': nrm(ks[21], (DEPTH, N_EXPERTS, D_MODEL), 1.0),
        'final_norm': gain(ks[22], (D_MODEL,)),
    }


def reference(x, c, ctx, c_ctx, w_mod, b_mod, norm_mix, w_in, conv_w, a_log, dt_bias, dn_out_norm,
              pool_w, pool_scale, w_up_dn, w_up_pool, w_out, norm_ffn, peer_wq, peer_keys, peer_u, peer_v,
              final_norm):
    xc = ctx
    for i in range(DEPTH):
        mod = (jax.nn.silu(c) @ w_mod[i] + b_mod[i])[:, None, :].astype(x.dtype)
        modc = (jax.nn.silu(c_ctx) @ w_mod[i] + b_mod[i])[None, None, :].astype(x.dtype)
        x, xc = hybrid_layer(x, xc, mod, modc, w_in[i], conv_w[i], a_log[i], dt_bias[i], dn_out_norm[i],
                             pool_w[i], pool_scale[i], w_up_dn[i], w_up_pool[i], w_out[i], norm_mix[i],
                             norm_ffn[i], peer_wq[i], peer_keys[i], peer_u[i], peer_v[i], i < DEPTH - 1)
    return rmsnorm(x, final_norm)
```

```python
import functools

import numpy as np
import jax
import jax.numpy as jnp
from jax import lax
from jax.experimental import pallas as pl
from jax.experimental.pallas import tpu as pltpu

D_MODEL = 1024
BATCH = 2
DEPTH = 2
GRID_W = 64
EPS = 1e-6

DN_HEADS = 4
DN_HEAD_DIM = 128
DN_WIDTH = DN_HEADS * DN_HEAD_DIM
SHORT_CONV = 4
DN_CHUNK = 64

POOL_WINDOWS = (2, 4, 8, 16)
POOL_GROUPS = 4
POOL_WIDTH = D_MODEL // 2
POOL_GW = POOL_WIDTH // POOL_GROUPS

PEER_HEADS = 8
N_KEYS = 128
N_EXPERTS = N_KEYS * N_KEYS
PEER_TOPK = 16
PEER_QDIM = 256
PEER_HALF = PEER_QDIM // 2

BA_PAD = 128
IN_COLS_R = 3 * DN_WIDTH + DN_WIDTH + POOL_WIDTH + 2 * D_MODEL + BA_PAD

ROW_TILE = 256
PEER_TOK_TILE = 512
PEER_EXP_TILE = 1024
HALO = 8
VMEM_LIMIT = 56 * 1024 * 1024

F32 = jnp.float32
BF16 = jnp.bfloat16
HIGHEST = lax.Precision.HIGHEST
NEG_INF = float("-inf")


def _cparams(*sem):
    return pltpu.CompilerParams(dimension_semantics=sem, vmem_limit_bytes=VMEM_LIMIT)


def _group_of_row(row0, seq):
    return jnp.where(row0 < seq, 0, jnp.where(row0 < 2 * seq, 1, 2))


def _silu(x):
    return x * jax.nn.sigmoid(x)


def _mod_kernel(c_ref, w_ref, b_ref, o_ref):
    o_ref[0] = jnp.dot(_silu(c_ref[...]), w_ref[0], preferred_element_type=F32) + b_ref[0]


def _modulation(cvec, w_mod, b_mod):
    tn = 1536
    return pl.pallas_call(
        _mod_kernel,
        grid=(DEPTH, 6 * D_MODEL // tn),
        in_specs=[
            pl.BlockSpec((8, D_MODEL), lambda l, j: (0, 0)),
            pl.BlockSpec((1, D_MODEL, tn), lambda l, j: (l, 0, j)),
            pl.BlockSpec((1, 1, tn), lambda l, j: (l, 0, j)),
        ],
        out_specs=pl.BlockSpec((1, 8, tn), lambda l, j: (l, 0, j)),
        out_shape=jax.ShapeDtypeStruct((DEPTH, 8, 6 * D_MODEL), F32),
        compiler_params=_cparams("arbitrary", "arbitrary"),
        name="modulation",
    )(cvec, w_mod, b_mod.reshape(DEPTH, 1, 6 * D_MODEL))


def _norm_mod(x, nw, mod, k):
    ms = jnp.mean(x * x, axis=-1, keepdims=True)
    xn = x * lax.rsqrt(ms + EPS) * nw
    sh = mod[:, k * D_MODEL:(k + 1) * D_MODEL]
    sc = mod[:, (k + 1) * D_MODEL:(k + 2) * D_MODEL]
    return xn * (1 + sc) + sh


def _inproj_kernel(x_ref, modv_ref, nw_ref, w_ref, qkv_ref, z_ref, pin_ref, gd_ref, gp_ref, ba_ref, *, seq):
    g = _group_of_row(pl.program_id(0) * ROW_TILE, seq)
    mod = modv_ref[pl.ds(g, 1), :]
    h = _norm_mod(x_ref[...], nw_ref[...], mod, 0)
    y = jnp.dot(h.astype(BF16), w_ref[...], preferred_element_type=F32)
    o = 0
    for ref in (qkv_ref, z_ref, pin_ref, gd_ref, gp_ref, ba_ref):
        n = ref.shape[1]
        ref[...] = y[:, o:o + n]
        o += n


def _inproj(xall, modv, norm_w, w_in_r, seq):
    rows = xall.shape[0]
    widths = (3 * DN_WIDTH, DN_WIDTH, POOL_WIDTH, D_MODEL, D_MODEL, BA_PAD)
    return pl.pallas_call(
        functools.partial(_inproj_kernel, seq=seq),
        grid=(rows // ROW_TILE,),
        in_specs=[
            pl.BlockSpec((ROW_TILE, D_MODEL), lambda i: (i, 0)),
            pl.BlockSpec((8, 6 * D_MODEL), lambda i: (0, 0)),
            pl.BlockSpec((1, D_MODEL), lambda i: (0, 0)),
            pl.BlockSpec((D_MODEL, IN_COLS_R), lambda i: (0, 0)),
        ],
        out_specs=[pl.BlockSpec((ROW_TILE, n), lambda i: (i, 0)) for n in widths],
        out_shape=[jax.ShapeDtypeStruct((rows, n), F32) for n in widths],
        compiler_params=_cparams("arbitrary"),
        name="inproj",
    )(xall, modv, norm_w.reshape(1, D_MODEL), w_in_r)


def _dnconv_kernel(cur_ref, prev_ref, next_ref, cw_ref, ba_ref, alog_ref, dtb_ref,
                   q_ref, k_ref, v_ref, bg_ref, *, seq, ctx):
    row0 = pl.program_id(0) * ROW_TILE
    nlat = BATCH * seq
    is_start = (row0 == 0) | (row0 == seq) | (row0 == nlat) | (row0 == nlat + ctx)
    row1 = row0 + ROW_TILE
    is_end = (row1 == seq) | (row1 == nlat) | (row1 == nlat + ctx) | (row1 == nlat + BATCH * ctx)
    prev = jnp.where(is_start, 0.0, prev_ref[...])
    nxt = jnp.where(is_end, 0.0, next_ref[...])
    ext = jnp.concatenate([prev, cur_ref[...], nxt], axis=0)
    left = SHORT_CONV // 2
    cw = cw_ref[...]
    y = None
    for j in range(SHORT_CONV):
        o = HALO - left + j
        term = ext[o:o + ROW_TILE, :] * cw[j:j + 1, :]
        y = term if y is None else y + term
    y = _silu(y)
    for hh in range(DN_HEADS):
        lo, hi = hh * DN_HEAD_DIM, (hh + 1) * DN_HEAD_DIM
        qh = y[:, lo:hi]
        kh = y[:, DN_WIDTH + lo:DN_WIDTH + hi]
        q_ref[:, lo:hi] = qh * lax.rsqrt(jnp.sum(qh * qh, axis=-1, keepdims=True) + EPS) * (DN_HEAD_DIM ** -0.5)
        k_ref[:, lo:hi] = kh * lax.rsqrt(jnp.sum(kh * kh, axis=-1, keepdims=True) + EPS)
    v_ref[...] = y[:, 2 * DN_WIDTH:]
    ba = ba_ref[...]
    beta = jax.nn.sigmoid(ba)
    xs = ba + dtb_ref[...]
    softplus = jnp.maximum(xs, 0.0) + jnp.log(1.0 + jnp.exp(-jnp.abs(xs)))
    gdec = -jnp.exp(alog_ref[...]) * softplus
    col = lax.broadcasted_iota(jnp.int32, ba.shape, 1)
    bg_ref[...] = jnp.where(col < 2 * DN_HEADS, beta, jnp.where(col < 4 * DN_HEADS, gdec, 0.0))


def _dnconv(qkv, ba, conv_w, a_log, dt_bias, seq, ctx):
    rows = qkv.shape[0]
    nh = ROW_TILE // HALO
    last = rows // HALO - 1
    pad = jnp.zeros((2 * DN_HEADS,), F32)
    tail = jnp.zeros((BA_PAD - 4 * DN_HEADS,), F32)
    alog = jnp.concatenate([pad, a_log.reshape(-1), tail]).reshape(1, BA_PAD)
    dtb = jnp.concatenate([pad, dt_bias.reshape(-1), tail]).reshape(1, BA_PAD)
    w3 = 3 * DN_WIDTH
    return pl.pallas_call(
        functools.partial(_dnconv_kernel, seq=seq, ctx=ctx),
        grid=(rows // ROW_TILE,),
        in_specs=[
            pl.BlockSpec((ROW_TILE, w3), lambda i: (i, 0)),
            pl.BlockSpec((HALO, w3), lambda i: (jnp.maximum(i * nh - 1, 0), 0)),
            pl.BlockSpec((HALO, w3), lambda i: (jnp.minimum((i + 1) * nh, last), 0)),
            pl.BlockSpec((SHORT_CONV, w3), lambda i: (0, 0)),
            pl.BlockSpec((ROW_TILE, BA_PAD), lambda i: (i, 0)),
            pl.BlockSpec((1, BA_PAD), lambda i: (0, 0)),
            pl.BlockSpec((1, BA_PAD), lambda i: (0, 0)),
        ],
        out_specs=[pl.BlockSpec((ROW_TILE, DN_WIDTH), lambda i: (i, 0))] * 3
        + [pl.BlockSpec((ROW_TILE, BA_PAD), lambda i: (i, 0))],
        out_shape=[jax.ShapeDtypeStruct((rows, DN_WIDTH), F32)] * 3
        + [jax.ShapeDtypeStruct((rows, BA_PAD), F32)],
        compiler_params=_cparams("arbitrary"),
        name="dnconv",
    )(qkv, qkv, qkv, conv_w, ba, alog, dtb)


def _unit_lower_inverse(low):
    n = low.shape[0]
    eye = (lax.broadcasted_iota(jnp.int32, (n, n), 0) == lax.broadcasted_iota(jnp.int32, (n, n), 1)).astype(F32)
    p = -low
    x = eye + p
    steps = int(np.ceil(np.log2(n))) - 1
    for _ in range(steps):
        p = jnp.dot(p, p, precision=HIGHEST, preferred_element_type=F32)
        x = x + jnp.dot(x, p, precision=HIGHEST, preferred_element_type=F32)
    return x


def _dnscan_kernel(qf_ref, kf_ref, vf_ref, bgf_ref, bgtf_ref, qb_ref, kb_ref, vb_ref, bgb_ref, bgtb_ref,
                   of_ref, ob_ref, s_ref):
    @pl.when(pl.program_id(1) == 0)
    def _():
        s_ref[...] = jnp.zeros_like(s_ref)

    c = DN_CHUNK
    ri = lax.broadcasted_iota(jnp.int32, (c, c), 0)
    ci = lax.broadcasted_iota(jnp.int32, (c, c), 1)
    dirs = (
        (qf_ref, kf_ref, vf_ref, bgf_ref, bgtf_ref, of_ref, ri >= ci, ri > ci, ci >= ri, c - 1),
        (qb_ref, kb_ref, vb_ref, bgb_ref, bgtb_ref, ob_ref, ri <= ci, ri < ci, ci <= ri, 0),
    )
    for d, (q_ref, k_ref, v_ref, bg_ref, bgt_ref, o_ref, incl, strict, incl_t, last) in enumerate(dirs):
        bg = bg_ref[...]
        gc_all = jnp.dot(incl.astype(F32), bg, precision=HIGHEST, preferred_element_type=F32)
        gr_all = jnp.dot(bgt_ref[0], incl_t.astype(F32), precision=HIGHEST, preferred_element_type=F32)
        for hh in range(DN_HEADS):
            lo, hi = hh * DN_HEAD_DIM, (hh + 1) * DN_HEAD_DIM
            bcol = d * DN_HEADS + hh
            gcol = 2 * DN_HEADS + bcol
            q = q_ref[:, lo:hi]
            k = k_ref[:, lo:hi]
            v = v_ref[:, lo:hi]
            beta = bg[:, bcol:bcol + 1]
            gc = gc_all[:, gcol:gcol + 1]
            gr = gr_all[gcol:gcol + 1, :]
            glast = gc_all[last:last + 1, gcol:gcol + 1]
            decay = jnp.where(incl, jnp.exp(jnp.minimum(gc - gr, 0.0)), 0.0)
            kk = lax.dot_general(k, k, (((1,), (1,)), ((), ())), preferred_element_type=F32)
            qk = lax.dot_general(q, k, (((1,), (1,)), ((), ())), preferred_element_type=F32)
            low = jnp.where(strict, beta * kk * decay, 0.0)
            tinv = _unit_lower_inverse(low)
            egc = jnp.exp(gc)
            rhs = jnp.concatenate([k * (beta * egc), v * beta], axis=1)
            wu = jnp.dot(tinv, rhs, preferred_element_type=F32)
            w = wu[:, :DN_HEAD_DIM]
            u = wu[:, DN_HEAD_DIM:]
            si = d * DN_HEADS + hh
            s = s_ref[si]
            v_new = u - jnp.dot(w, s, preferred_element_type=F32)
            intra = jnp.where(incl, qk * decay, 0.0)
            o_ref[:, lo:hi] = (jnp.dot(q * egc, s, preferred_element_type=F32)
                               + jnp.dot(intra, v_new, preferred_element_type=F32))
            kdec = k * jnp.exp(glast - gc)
            s_ref[si] = s * jnp.exp(glast) + lax.dot_general(
                kdec, v_new, (((0,), (0,)), ((), ())), preferred_element_type=F32)


def _dnscan(q, k, v, bg, seq, ctx):
    rows = q.shape[0]
    c = DN_CHUNK
    nct, nlt = ctx // c, seq // c
    lat_chunks = BATCH * nlt
    bgt = bg[:, :4 * DN_HEADS].reshape(rows // c, c, 4 * DN_HEADS).transpose(0, 2, 1)

    def fwd(b, s):
        return jnp.where(s < nct, lat_chunks + b * nct + s, b * nlt + s - nct)

    def bwd(b, s):
        return jnp.where(s < nct, lat_chunks + b * nct + nct - 1 - s, b * nlt + nlt + nct - 1 - s)

    def specs(idx):
        wide = pl.BlockSpec((c, DN_WIDTH), lambda b, s: (idx(b, s), 0))
        return [wide, wide, wide,
                pl.BlockSpec((c, BA_PAD), lambda b, s: (idx(b, s), 0)),
                pl.BlockSpec((1, 4 * DN_HEADS, c), lambda b, s: (idx(b, s), 0, 0))]

    return pl.pallas_call(
        _dnscan_kernel,
        grid=(BATCH, nct + nlt),
        in_specs=specs(fwd) + specs(bwd),
        out_specs=[pl.BlockSpec((c, DN_WIDTH), lambda b, s: (fwd(b, s), 0)),
                   pl.BlockSpec((c, DN_WIDTH), lambda b, s: (bwd(b, s), 0))],
        out_shape=[jax.ShapeDtypeStruct((rows, DN_WIDTH), F32)] * 2,
        scratch_shapes=[pltpu.VMEM((2 * DN_HEADS, DN_HEAD_DIM, DN_HEAD_DIM), F32)],
        compiler_params=_cparams("arbitrary", "arbitrary"),
        name="dnscan",
    )(q, k, v, bg, bgt, q, k, v, bg, bgt)


def _pool_matrices(tile, seg):
    t = np.arange(tile)
    p = t % seg
    mats = []
    for win in POOL_WINDOWS:
        lo = np.clip(p - win // 2, 0, seg)
        hi = np.clip(p + win - win // 2, 0, seg)
        same = (t[:, None] // seg) == (t[None, :] // seg)
        inside = same & (p[None, :] >= lo[:, None]) & (p[None, :] < hi[:, None])
        mats.append(inside / (hi - lo)[:, None].astype(np.float64) - np.eye(tile))
    return np.stack(mats).astype(np.float32)


def _merge_kernel(x_ref, of_ref, ob_ref, z_ref, pin_ref, gd_ref, gp_ref, modv_ref, on_ref, pm_ref, pw_ref, ps_ref,
                  wud_ref, wup_ref, wo_ref, o_ref, *, seq):
    g = _group_of_row(pl.program_id(0) * ROW_TILE, seq)
    mod = modv_ref[pl.ds(g, 1), :]
    gate = mod[:, 2 * D_MODEL:3 * D_MODEL]
    o = of_ref[...] + ob_ref[...]
    z = z_ref[...]
    onw = on_ref[...]
    ys = []
    for hh in range(DN_HEADS):
        lo, hi = hh * DN_HEAD_DIM, (hh + 1) * DN_HEAD_DIM
        oh = o[:, lo:hi]
        ys.append(oh * lax.rsqrt(jnp.mean(oh * oh, axis=-1, keepdims=True) + EPS) * onw * _silu(z[:, lo:hi]))
    y_dn = jnp.concatenate(ys, axis=1)
    pin = pin_ref[...]
    yp = []
    for gi in range(POOL_GROUPS):
        lo, hi = gi * POOL_GW, (gi + 1) * POOL_GW
        pooled = jnp.dot(pm_ref[0, gi], pin[:, lo:hi], precision=HIGHEST, preferred_element_type=F32)
        yp.append(jnp.dot(pooled, pw_ref[gi], preferred_element_type=F32))
    y_pool = jnp.concatenate(yp, axis=1) * ps_ref[...]
    m = (jax.nn.sigmoid(gd_ref[...]) * jnp.dot(y_dn.astype(BF16), wud_ref[...], preferred_element_type=F32)
         + jax.nn.sigmoid(gp_ref[...]) * jnp.dot(y_pool.astype(BF16), wup_ref[...], preferred_element_type=F32))
    out = jnp.dot(m.astype(BF16), wo_ref[...], preferred_element_type=F32)
    o_ref[...] = x_ref[...] + gate * out


def _merge(xall, o_f, o_b, z, pin, gd, gp, modv, onorm, pool_w, pool_scale, wud, wup, wo, seq, ctx, rows_out):
    lat_tiles = BATCH * seq // ROW_TILE
    pm = jnp.asarray(np.stack([_pool_matrices(ROW_TILE, GRID_W), _pool_matrices(ROW_TILE, ctx)]))
    row = lambda n: pl.BlockSpec((ROW_TILE, n), lambda i: (i, 0))
    full = lambda *s: pl.BlockSpec(s, lambda i: (0,) * len(s))
    return pl.pallas_call(
        functools.partial(_merge_kernel, seq=seq),
        grid=(rows_out // ROW_TILE,),
        in_specs=[
            row(D_MODEL), row(DN_WIDTH), row(DN_WIDTH), row(DN_WIDTH), row(POOL_WIDTH), row(D_MODEL), row(D_MODEL),
            full(8, 6 * D_MODEL), full(1, DN_HEAD_DIM),
            pl.BlockSpec((1, POOL_GROUPS, ROW_TILE, ROW_TILE), lambda i: (jnp.where(i >= lat_tiles, 1, 0), 0, 0, 0)),
            full(POOL_GROUPS, POOL_GW, POOL_GW), full(1, POOL_WIDTH),
            full(DN_WIDTH, D_MODEL), full(POOL_WIDTH, D_MODEL), full(D_MODEL, D_MODEL),
        ],
        out_specs=row(D_MODEL),
        out_shape=jax.ShapeDtypeStruct((rows_out, D_MODEL), F32),
        compiler_params=_cparams("arbitrary"),
        name="merge",
    )(xall, o_f, o_b, z, pin, gd, gp, modv, onorm.reshape(1, DN_HEAD_DIM), pm, pool_w,
      pool_scale.reshape(1, POOL_WIDTH), wud, wup, wo)


def _top_ranks(s, vals_ref):
    n = s.shape[0]
    iota = lax.broadcasted_iota(jnp.int32, s.shape, 0)

    def body(r, carry):
        cur, rank = carry
        m = jnp.max(cur, axis=0, keepdims=True)
        idx = jnp.min(jnp.where(cur == m, iota, n), axis=0, keepdims=True)
        sel = iota == idx
        vals_ref[pl.ds(r, 1), :] = m
        return jnp.where(sel, NEG_INF, cur), jnp.where(sel, r, rank)

    _, rank = lax.fori_loop(0, PEER_TOPK, body, (s, jnp.full(s.shape, PEER_TOPK, jnp.int32)))
    return rank


def _peer_route_kernel(x_ref, modv_ref, nw_ref, wq_ref, keys_ref, hn_ref, n1_ref, e1_ref, r2_ref, e2_ref,
                       q_scr, c_scr, d_scr, *, seq):
    g = _group_of_row(pl.program_id(0) * ROW_TILE, seq)
    mod = modv_ref[pl.ds(g, 1), :]
    hn = _norm_mod(x_ref[...], nw_ref[...], mod, 3).astype(BF16)
    hn_ref[...] = hn
    q = jnp.dot(hn, wq_ref[...], preferred_element_type=F32)
    for j in range(2 * PEER_HEADS):
        q_scr[j] = q[:, j * PEER_HALF:(j + 1) * PEER_HALF]
    k = PEER_TOPK
    iota_a = lax.broadcasted_iota(jnp.int32, (k, ROW_TILE), 0)

    def head(hh, carry):
        nt = (((1,), (1,)), ((), ()))
        s1 = lax.dot_general(keys_ref[2 * hh], q_scr[2 * hh], nt, preferred_element_type=F32)
        s2 = lax.dot_general(keys_ref[2 * hh + 1], q_scr[2 * hh + 1], nt, preferred_element_type=F32)
        rank1 = _top_ranks(s1, c_scr)
        rank2 = _top_ranks(s2, d_scr)
        c = c_scr[...]
        d = d_scr[...]

        def pick(_, cnt):
            dn = jnp.full(c.shape, NEG_INF, F32)
            for b in range(k):
                dn = jnp.where(cnt == b, d[b:b + 1, :], dn)
            f = c + dn
            m = jnp.max(f, axis=0, keepdims=True)
            a_sel = jnp.min(jnp.where(f == m, iota_a, k), axis=0, keepdims=True)
            return cnt + (iota_a == a_sel).astype(jnp.int32)

        cnt = lax.fori_loop(0, k, pick, jnp.zeros((k, ROW_TILE), jnp.int32))
        e1c = jnp.exp(c - c[0:1, :])
        e2d = jnp.exp(d - d[0:1, :])
        part = jnp.zeros(c.shape, F32)
        for b in range(k):
            part = part + jnp.where(cnt > b, e2d[b:b + 1, :], 0.0)
        zsum = jnp.sum(e1c * part, axis=0, keepdims=True)
        cntf = cnt.astype(F32)
        n1 = jnp.zeros(s1.shape, F32)
        for a in range(k):
            n1 = jnp.where(rank1 == a, cntf[a:a + 1, :], n1)
        n1_ref[hh] = n1
        e1_ref[hh] = jnp.exp(s1 - c[0:1, :]) / zsum
        r2_ref[hh] = rank2.astype(F32)
        e2_ref[hh] = jnp.exp(s2 - d[0:1, :])
        return carry

    lax.fori_loop(0, PEER_HEADS, head, 0)


def _peer_route(xall, modv, norm_w, wq, keys, seq, rows_out):
    row = lambda n: pl.BlockSpec((ROW_TILE, n), lambda i: (i, 0))
    full = lambda *s: pl.BlockSpec(s, lambda i: (0,) * len(s))
    tab = pl.BlockSpec((PEER_HEADS, N_KEYS, ROW_TILE), lambda i: (0, 0, i))
    tab_shape = jax.ShapeDtypeStruct((PEER_HEADS, N_KEYS, rows_out), F32)
    return pl.pallas_call(
        functools.partial(_peer_route_kernel, seq=seq),
        grid=(rows_out // ROW_TILE,),
        in_specs=[row(D_MODEL), full(8, 6 * D_MODEL), full(1, D_MODEL), full(D_MODEL, PEER_HEADS * PEER_QDIM),
                  full(2 * PEER_HEADS, N_KEYS, PEER_HALF)],
        out_specs=[row(D_MODEL), tab, tab, tab, tab],
        out_shape=[jax.ShapeDtypeStruct((rows_out, D_MODEL), BF16), tab_shape, tab_shape, tab_shape, tab_shape],
        scratch_shapes=[pltpu.VMEM((2 * PEER_HEADS, ROW_TILE, PEER_HALF), F32),
                        pltpu.VMEM((PEER_TOPK, ROW_TILE), F32), pltpu.VMEM((PEER_TOPK, ROW_TILE), F32)],
        compiler_params=_cparams("arbitrary"),
        name="peer_route",
    )(xall, modv, norm_w.reshape(1, D_MODEL), wq, keys.reshape(2 * PEER_HEADS, N_KEYS, PEER_HALF))


def _peer_dense_kernel(x_ref, modv_ref, hn_ref, u_ref, v_ref, n1_ref, e1_ref, r2_ref, e2_ref, fn_ref, o_ref,
                       acc_ref, a_ref, act_ref, *, seq, final):
    e = pl.program_id(1)

    @pl.when(e == 0)
    def _():
        acc_ref[...] = jnp.zeros_like(acc_ref)

    a_ref[...] = lax.dot_general(u_ref[...], hn_ref[...], (((1,), (1,)), ((), ())), preferred_element_type=F32)
    n_first = PEER_EXP_TILE // N_KEYS

    def first_key(i, carry):
        wsum = jnp.zeros((N_KEYS, PEER_TOK_TILE), F32)
        for hh in range(PEER_HEADS):
            n1 = n1_ref[hh, pl.ds(i, 1), :]
            e1 = e1_ref[hh, pl.ds(i, 1), :]
            wsum = wsum + jnp.where(r2_ref[hh] < n1, e2_ref[hh], 0.0) * e1
        rows = pl.ds(pl.multiple_of(i * N_KEYS, N_KEYS), N_KEYS)
        a = a_ref[rows, :]
        gelu = 0.5 * a * (1.0 + lax.erf(a * (2.0 ** -0.5)))
        act_ref[rows, :] = (wsum * gelu).astype(BF16)
        return carry

    lax.fori_loop(0, n_first, first_key, 0)
    acc_ref[...] += lax.dot_general(act_ref[...], v_ref[...], (((0,), (0,)), ((), ())), preferred_element_type=F32)

    @pl.when(e == pl.num_programs(1) - 1)
    def _():
        g = _group_of_row(pl.program_id(0) * PEER_TOK_TILE, seq)
        mod = modv_ref[pl.ds(g, 1), :]
        y = x_ref[...] + mod[:, 5 * D_MODEL:6 * D_MODEL] * acc_ref[...]
        if final:
            y = y * lax.rsqrt(jnp.mean(y * y, axis=-1, keepdims=True) + EPS) * fn_ref[...]
        o_ref[...] = y


def _peer_dense(xall, modv, hn, u_bf, v_bf, n1, e1, r2, e2, final_norm, seq, rows_out, final):
    tt, et = PEER_TOK_TILE, PEER_EXP_TILE
    n_first = et // N_KEYS
    tok = lambda n: pl.BlockSpec((tt, n), lambda t, e: (t, 0))
    full = lambda *s: pl.BlockSpec(s, lambda t, e: (0,) * len(s))
    per_first = pl.BlockSpec((PEER_HEADS, n_first, tt), lambda t, e: (0, e, t))
    per_second = pl.BlockSpec((PEER_HEADS, N_KEYS, tt), lambda t, e: (0, 0, t))
    return pl.pallas_call(
        functools.partial(_peer_dense_kernel, seq=seq, final=final),
        grid=(rows_out // tt, N_EXPERTS // et),
        in_specs=[tok(D_MODEL), full(8, 6 * D_MODEL), tok(D_MODEL),
                  pl.BlockSpec((et, D_MODEL), lambda t, e: (e, 0)),
                  pl.BlockSpec((et, D_MODEL), lambda t, e: (e, 0)),
                  per_first, per_first, per_second, per_second, full(1, D_MODEL)],
        out_specs=tok(D_MODEL),
        out_shape=jax.ShapeDtypeStruct((rows_out, D_MODEL), F32),
        scratch_shapes=[pltpu.VMEM((tt, D_MODEL), F32), pltpu.VMEM((et, tt), F32), pltpu.VMEM((et, tt), BF16)],
        compiler_params=_cparams("arbitrary", "arbitrary"),
        name="peer_dense",
    )(xall, modv, hn, u_bf, v_bf, n1, e1, r2, e2, final_norm.reshape(1, D_MODEL))


def _reorder_in_weight(w):
    s = np.cumsum((0, 3 * DN_WIDTH, DN_WIDTH, POOL_WIDTH, 2 * DN_HEADS, 2 * DN_HEADS, D_MODEL, D_MODEL))
    qkv, z, pin, b, a, gd, gp = (w[:, s[i]:s[i + 1]] for i in range(7))
    pad = jnp.zeros((w.shape[0], BA_PAD - 4 * DN_HEADS), w.dtype)
    return jnp.concatenate([qkv, z, pin, gd, gp, b, a, pad], axis=1).astype(BF16)


def _forward(x, c, ctx, c_ctx, w_mod, b_mod, norm_mix, w_in, conv_w, a_log, dt_bias, dn_out_norm, pool_w, pool_scale,
             w_up_dn, w_up_pool, w_out, norm_ffn, peer_wq, peer_keys, peer_u, peer_v, final_norm):
    seq, nctx = x.shape[1], ctx.shape[1]
    nlat = BATCH * seq
    xall = jnp.concatenate([x.reshape(nlat, D_MODEL), ctx.reshape(BATCH * nctx, D_MODEL)], axis=0)
    rows = xall.shape[0]
    cvec = jnp.concatenate([c, c_ctx[None, :], jnp.zeros((8 - BATCH - 1, D_MODEL), F32)], axis=0)
    modv_all = _modulation(cvec, w_mod, b_mod)
    for i in range(DEPTH):
        last = i == DEPTH - 1
        rows_out = nlat if last else rows
        modv = modv_all[i]
        qkv, z, pin, gd, gp, ba = _inproj(xall, modv, norm_mix[i], _reorder_in_weight(w_in[i]), seq)
        q, k, v, bg = _dnconv(qkv, ba, conv_w[i], a_log[i], dt_bias[i], seq, nctx)
        o_f, o_b = _dnscan(q, k, v, bg, seq, nctx)
        xall = _merge(xall, o_f, o_b, z, pin, gd, gp, modv, dn_out_norm[i], pool_w[i], pool_scale[i],
                      w_up_dn[i].astype(BF16), w_up_pool[i].astype(BF16), w_out[i].astype(BF16), seq, nctx, rows_out)
        hn, n1, e1, r2, e2 = _peer_route(xall, modv, norm_ffn[i], peer_wq[i].astype(BF16), peer_keys[i], seq, rows_out)
        xall = _peer_dense(xall, modv, hn, peer_u[i].astype(BF16), peer_v[i].astype(BF16), n1, e1, r2, e2,
                           final_norm, seq, rows_out, last)
    return xall.reshape(BATCH, seq, D_MODEL)


def kernel(x, c, ctx, c_ctx, w_mod, b_mod, norm_mix, w_in, conv_w, a_log, dt_bias, dn_out_norm, pool_w, pool_scale, w_up_dn, w_up_pool, w_out, norm_ffn, peer_wq, peer_keys, peer_u, peer_v, final_norm):
    return _forward(x, c, ctx, c_ctx, w_mod, b_mod, norm_mix, w_in, conv_w, a_log, dt_bias, dn_out_norm, pool_w,
                    pool_scale, w_up_dn, w_up_pool, w_out, norm_ffn, peer_wq, peer_keys, peer_u, peer_v, final_norm)
```

```python
import functools

import numpy as np
import jax
import jax.numpy as jnp
from jax import lax
from jax.experimental import pallas as pl
from jax.experimental.pallas import tpu as pltpu

D_MODEL = 1024
BATCH = 2
DEPTH = 2
GRID_W = 64
EPS = 1e-6

DN_HEADS = 4
DN_HEAD_DIM = 128
DN_WIDTH = DN_HEADS * DN_HEAD_DIM
SHORT_CONV = 4
DN_CHUNK = 64
DN_CHAINS = 2 * DN_HEADS
DN_PREP_CHUNKS = 2

POOL_WINDOWS = (2, 4, 8, 16)
POOL_GROUPS = 4
POOL_WIDTH = D_MODEL // 2
POOL_GW = POOL_WIDTH // POOL_GROUPS

PEER_HEADS = 8
N_KEYS = 128
N_EXPERTS = N_KEYS * N_KEYS
PEER_TOPK = 16
PEER_QDIM = 256
PEER_HALF = PEER_QDIM // 2

BA_PAD = 128
IN_COLS_R = 3 * DN_WIDTH + DN_WIDTH + POOL_WIDTH + 2 * D_MODEL + BA_PAD

LANES = 128
ROW_TILE = 256
PEER_TOK_TILE = 512
PEER_EXP_TILE = 1024
PEER_SLAB = 256
HALO = 8
VMEM_LIMIT = 56 * 1024 * 1024

F32 = jnp.float32
BF16 = jnp.bfloat16
HIGHEST = lax.Precision.HIGHEST
NEG_INF = float("-inf")
NT_DIMS = (((1,), (1,)), ((), ()))


def _cparams(*sem):
    return pltpu.CompilerParams(dimension_semantics=sem, vmem_limit_bytes=VMEM_LIMIT)


def _group_of_row(row0, seq):
    return jnp.where(row0 < seq, 0, jnp.where(row0 < 2 * seq, 1, 2))


def _silu(x):
    return x * jax.nn.sigmoid(x)


def _split_bf16(a):
    hi = a.astype(BF16)
    lo = (a - hi.astype(F32)).astype(BF16)
    return hi, lo


def _dot_split(a, b):
    ah, al = _split_bf16(a)
    bh, bl = _split_bf16(b)
    a4 = jnp.concatenate([ah, al, ah, al], axis=1)
    b4 = jnp.concatenate([bh, bh, bl, bl], axis=0)
    return jnp.dot(a4, b4, preferred_element_type=F32)


def _mod_kernel(c_ref, w_ref, b_ref, o_ref):
    o_ref[0] = jnp.dot(_silu(c_ref[...]), w_ref[0], preferred_element_type=F32) + b_ref[0]


def _modulation(cvec, w_mod, b_mod):
    tn = 1536
    return pl.pallas_call(
        _mod_kernel,
        grid=(DEPTH, 6 * D_MODEL // tn),
        in_specs=[
            pl.BlockSpec((8, D_MODEL), lambda l, j: (0, 0)),
            pl.BlockSpec((1, D_MODEL, tn), lambda l, j: (l, 0, j)),
            pl.BlockSpec((1, 1, tn), lambda l, j: (l, 0, j)),
        ],
        out_specs=pl.BlockSpec((1, 8, tn), lambda l, j: (l, 0, j)),
        out_shape=jax.ShapeDtypeStruct((DEPTH, 8, 6 * D_MODEL), F32),
        compiler_params=_cparams("arbitrary", "arbitrary"),
        name="modulation",
    )(cvec, w_mod, b_mod.reshape(DEPTH, 1, 6 * D_MODEL))


def _norm_mod(x, nw, mod, k):
    ms = jnp.mean(x * x, axis=-1, keepdims=True)
    xn = x * lax.rsqrt(ms + EPS) * nw
    sh = mod[:, k * D_MODEL:(k + 1) * D_MODEL]
    sc = mod[:, (k + 1) * D_MODEL:(k + 2) * D_MODEL]
    return xn * (1 + sc) + sh


def _inproj_kernel(x_ref, modv_ref, nw_ref, w_ref, qkv_ref, z_ref, pin_ref, gd_ref, gp_ref, ba_ref, *, seq):
    g = _group_of_row(pl.program_id(0) * ROW_TILE, seq)
    mod = modv_ref[pl.ds(g, 1), :]
    h = _norm_mod(x_ref[...], nw_ref[...], mod, 0)
    y = jnp.dot(h.astype(BF16), w_ref[...], preferred_element_type=F32)
    o = 0
    for ref in (qkv_ref, z_ref, pin_ref, gd_ref, gp_ref, ba_ref):
        n = ref.shape[1]
        ref[...] = y[:, o:o + n]
        o += n


def _inproj(xall, modv, norm_w, w_in_r, seq):
    rows = xall.shape[0]
    widths = (3 * DN_WIDTH, DN_WIDTH, POOL_WIDTH, D_MODEL, D_MODEL, BA_PAD)
    return pl.pallas_call(
        functools.partial(_inproj_kernel, seq=seq),
        grid=(rows // ROW_TILE,),
        in_specs=[
            pl.BlockSpec((ROW_TILE, D_MODEL), lambda i: (i, 0)),
            pl.BlockSpec((8, 6 * D_MODEL), lambda i: (0, 0)),
            pl.BlockSpec((1, D_MODEL), lambda i: (0, 0)),
            pl.BlockSpec((D_MODEL, IN_COLS_R), lambda i: (0, 0)),
        ],
        out_specs=[pl.BlockSpec((ROW_TILE, n), lambda i: (i, 0)) for n in widths],
        out_shape=[jax.ShapeDtypeStruct((rows, n), F32) for n in widths],
        compiler_params=_cparams("arbitrary"),
        name="inproj",
    )(xall, modv, norm_w.reshape(1, D_MODEL), w_in_r)


def _dnconv_kernel(cur_ref, prev_ref, next_ref, cw_ref, ba_ref, alog_ref, dtb_ref,
                   q_ref, k_ref, v_ref, bg_ref, *, seq, ctx):
    row0 = pl.program_id(0) * ROW_TILE
    nlat = BATCH * seq
    is_start = (row0 == 0) | (row0 == seq) | (row0 == nlat) | (row0 == nlat + ctx)
    row1 = row0 + ROW_TILE
    is_end = (row1 == seq) | (row1 == nlat) | (row1 == nlat + ctx) | (row1 == nlat + BATCH * ctx)
    prev = jnp.where(is_start, 0.0, prev_ref[...])
    nxt = jnp.where(is_end, 0.0, next_ref[...])
    ext = jnp.concatenate([prev, cur_ref[...], nxt], axis=0)
    left = SHORT_CONV // 2
    cw = cw_ref[...]
    y = None
    for j in range(SHORT_CONV):
        o = HALO - left + j
        term = ext[o:o + ROW_TILE, :] * cw[j:j + 1, :]
        y = term if y is None else y + term
    y = _silu(y)
    for hh in range(DN_HEADS):
        lo, hi = hh * DN_HEAD_DIM, (hh + 1) * DN_HEAD_DIM
        qh = y[:, lo:hi]
        kh = y[:, DN_WIDTH + lo:DN_WIDTH + hi]
        q_ref[:, lo:hi] = qh * lax.rsqrt(jnp.sum(qh * qh, axis=-1, keepdims=True) + EPS) * (DN_HEAD_DIM ** -0.5)
        k_ref[:, lo:hi] = kh * lax.rsqrt(jnp.sum(kh * kh, axis=-1, keepdims=True) + EPS)
    v_ref[...] = y[:, 2 * DN_WIDTH:]
    ba = ba_ref[...]
    beta = jax.nn.sigmoid(ba)
    xs = ba + dtb_ref[...]
    softplus = jnp.maximum(xs, 0.0) + jnp.log(1.0 + jnp.exp(-jnp.abs(xs)))
    gdec = -jnp.exp(alog_ref[...]) * softplus
    col = lax.broadcasted_iota(jnp.int32, ba.shape, 1)
    bg_ref[...] = jnp.where(col < 2 * DN_HEADS, beta, jnp.where(col < 4 * DN_HEADS, gdec, 0.0))


def _dnconv(qkv, ba, conv_w, a_log, dt_bias, seq, ctx):
    rows = qkv.shape[0]
    nh = ROW_TILE // HALO
    last = rows // HALO - 1
    pad = jnp.zeros((2 * DN_HEADS,), F32)
    tail = jnp.zeros((BA_PAD - 4 * DN_HEADS,), F32)
    alog = jnp.concatenate([pad, a_log.reshape(-1), tail]).reshape(1, BA_PAD)
    dtb = jnp.concatenate([pad, dt_bias.reshape(-1), tail]).reshape(1, BA_PAD)
    w3 = 3 * DN_WIDTH
    return pl.pallas_call(
        functools.partial(_dnconv_kernel, seq=seq, ctx=ctx),
        grid=(rows // ROW_TILE,),
        in_specs=[
            pl.BlockSpec((ROW_TILE, w3), lambda i: (i, 0)),
            pl.BlockSpec((HALO, w3), lambda i: (jnp.maximum(i * nh - 1, 0), 0)),
            pl.BlockSpec((HALO, w3), lambda i: (jnp.minimum((i + 1) * nh, last), 0)),
            pl.BlockSpec((SHORT_CONV, w3), lambda i: (0, 0)),
            pl.BlockSpec((ROW_TILE, BA_PAD), lambda i: (i, 0)),
            pl.BlockSpec((1, BA_PAD), lambda i: (0, 0)),
            pl.BlockSpec((1, BA_PAD), lambda i: (0, 0)),
        ],
        out_specs=[pl.BlockSpec((ROW_TILE, DN_WIDTH), lambda i: (i, 0))] * 3
        + [pl.BlockSpec((ROW_TILE, BA_PAD), lambda i: (i, 0))],
        out_shape=[jax.ShapeDtypeStruct((rows, DN_WIDTH), F32)] * 3
        + [jax.ShapeDtypeStruct((rows, BA_PAD), F32)],
        compiler_params=_cparams("arbitrary"),
        name="dnconv",
    )(qkv, qkv, qkv, conv_w, ba, alog, dtb)


def _dnprep_kernel(q_ref, k_ref, v_ref, bg_ref, bgt_ref, wq_ref, u_ref, lk_ref, eg_ref):
    c = DN_CHUNK
    ri = lax.broadcasted_iota(jnp.int32, (c, c), 0)
    ci = lax.broadcasted_iota(jnp.int32, (c, c), 1)
    dirs = ((ri >= ci, ri > ci, ci >= ri, c - 1), (ri <= ci, ri < ci, ci <= ri, 0))
    chains = []
    for j in range(DN_PREP_CHUNKS):
        rows = slice(j * c, (j + 1) * c)
        bg = bg_ref[rows, :]
        bgh, bgl = _split_bf16(bg)
        bgth, bgtl = _split_bf16(bgt_ref[j])
        bg2 = jnp.concatenate([bgh, bgl], axis=0)
        bgt2 = jnp.concatenate([bgth, bgtl], axis=1)
        egs = []
        for d, (incl, strict, incl_t, last) in enumerate(dirs):
            m = incl.astype(BF16)
            mt = incl_t.astype(BF16)
            gc_all = jnp.dot(jnp.concatenate([m, m], axis=1), bg2, preferred_element_type=F32)
            gr_all = jnp.dot(bgt2, jnp.concatenate([mt, mt], axis=0), preferred_element_type=F32)
            for hh in range(DN_HEADS):
                lo, hi = hh * DN_HEAD_DIM, (hh + 1) * DN_HEAD_DIM
                ch = d * DN_HEADS + hh
                gcol = 2 * DN_HEADS + ch
                q = q_ref[rows, lo:hi]
                k = k_ref[rows, lo:hi]
                v = v_ref[rows, lo:hi]
                beta = bg[:, ch:ch + 1]
                gc = gc_all[:, gcol:gcol + 1]
                gr = gr_all[gcol:gcol + 1, :]
                glast = gc_all[last:last + 1, gcol:gcol + 1]
                decay = jnp.where(incl, jnp.exp(jnp.minimum(gc - gr, 0.0)), 0.0)
                kk = lax.dot_general(k, k, NT_DIMS, preferred_element_type=F32)
                qk = lax.dot_general(q, k, NT_DIMS, preferred_element_type=F32)
                egc = jnp.exp(gc)
                cols = slice(ch * DN_HEAD_DIM, (ch + 1) * DN_HEAD_DIM)
                wq_ref[0, j, c:2 * c, cols] = (q * egc).astype(BF16)
                lk_ref[0, j, ch, 0:c, :] = jnp.where(incl, qk * decay, 0.0).astype(BF16)
                lk_ref[0, j, ch, c:, :] = (k * jnp.exp(glast - gc)).T.astype(BF16)
                egs.append(jnp.broadcast_to(jnp.exp(glast), (1, LANES)))
                chains.append((j, cols, jnp.where(strict, -(beta * kk * decay), 0.0),
                               jnp.concatenate([k * (beta * egc), v * beta], axis=1)))
        eg_ref[0, j] = jnp.concatenate(egs, axis=0)
    ps = [ch[2] for ch in chains]
    ys = [ch[3] for ch in chains]
    ys = [y + _dot_split(p, y) for p, y in zip(ps, ys)]
    for _ in range(int(np.log2(c)) - 1):
        ps = [_dot_split(p, p) for p in ps]
        ys = [y + _dot_split(p, y) for p, y in zip(ps, ys)]
    for (j, cols, _, _), y in zip(chains, ys):
        wq_ref[0, j, 0:c, cols] = y[:, :DN_HEAD_DIM].astype(BF16)
        u_ref[0, j, :, cols] = y[:, DN_HEAD_DIM:]


def _dnprep(q, k, v, bg, seq, ctx):
    rows = q.shape[0]
    c = DN_CHUNK
    n = DN_PREP_CHUNKS
    nct, nlt = ctx // c, seq // c
    assert nct % n == 0 and nlt % n == 0
    nch = nct + nlt
    bgt = bg[:, :4 * DN_HEADS].reshape(rows // c, c, 4 * DN_HEADS).transpose(0, 2, 1)

    def seq_pos(i):
        i = i * n
        lat = i < BATCH * nlt
        j = i - BATCH * nlt
        return jnp.where(lat, i // nlt, j // nct), jnp.where(lat, nct + i % nlt, j % nct) // n

    def out_spec(*tail):
        return pl.BlockSpec((1, n) + tail, lambda i: seq_pos(i) + (0,) * len(tail))

    wide = pl.BlockSpec((n * c, DN_WIDTH), lambda i: (i, 0))
    width = DN_CHAINS * DN_HEAD_DIM
    return pl.pallas_call(
        _dnprep_kernel,
        grid=(rows // (n * c),),
        in_specs=[wide, wide, wide,
                  pl.BlockSpec((n * c, BA_PAD), lambda i: (i, 0)),
                  pl.BlockSpec((n, 4 * DN_HEADS, c), lambda i: (i, 0, 0))],
        out_specs=[out_spec(2 * c, width), out_spec(c, width), out_spec(DN_CHAINS, c + DN_HEAD_DIM, c),
                   out_spec(DN_CHAINS, LANES)],
        out_shape=[jax.ShapeDtypeStruct((BATCH, nch, 2 * c, width), BF16),
                   jax.ShapeDtypeStruct((BATCH, nch, c, width), F32),
                   jax.ShapeDtypeStruct((BATCH, nch, DN_CHAINS, c + DN_HEAD_DIM, c), BF16),
                   jax.ShapeDtypeStruct((BATCH, nch, DN_CHAINS, LANES), F32)],
        compiler_params=_cparams("arbitrary"),
        name="dnprep",
    )(q, k, v, bg, bgt)


def _dnscan_kernel(wqf_ref, uf_ref, lkf_ref, egf_ref, wqb_ref, ub_ref, lkb_ref, egb_ref, of_ref, ob_ref, s_ref):
    @pl.when(pl.program_id(0) == 0)
    def _():
        s_ref[...] = jnp.zeros_like(s_ref)

    c = DN_CHUNK
    dirs = ((wqf_ref, uf_ref, lkf_ref, egf_ref, of_ref), (wqb_ref, ub_ref, lkb_ref, egb_ref, ob_ref))
    chains = [(d, b, hh) for d in range(2) for b in range(BATCH) for hh in range(DN_HEADS)]
    sidx = lambda d, b, hh: (b * 2 + d) * DN_HEADS + hh
    cols = lambda hh: slice(hh * DN_HEAD_DIM, (hh + 1) * DN_HEAD_DIM)
    ss = [s_ref[sidx(*ch)] for ch in chains]
    r1 = [jnp.dot(dirs[d][0][b, 0, :, cols(hh)], s.astype(BF16), preferred_element_type=F32)
          for (d, b, hh), s in zip(chains, ss)]
    vn = [dirs[d][1][b, 0, :, cols(hh)] - r[:c] for (d, b, hh), r in zip(chains, r1)]
    r2 = [jnp.dot(dirs[d][2][b, 0, hh], v.astype(BF16), preferred_element_type=F32)
          for (d, b, hh), v in zip(chains, vn)]
    for (d, b, hh), s, a1, a2 in zip(chains, ss, r1, r2):
        dirs[d][4][b, 0, :, cols(hh)] = a1[c:] + a2[:c]
        row = d * DN_HEADS + hh
        s_ref[sidx(d, b, hh)] = s * dirs[d][3][b, 0, row:row + 1, :] + a2[c:]


def _dnscan(wq, u, lk, eg, seq, ctx):
    c = DN_CHUNK
    nct, nlt = ctx // c, seq // c
    nch = nct + nlt

    def bwd(s):
        return jnp.where(s < nct, nct - 1 - s, 2 * nct + nlt - 1 - s)

    def specs(pos, d):
        return [pl.BlockSpec((BATCH, 1, 2 * c, DN_WIDTH), lambda s: (0, pos(s), 0, d)),
                pl.BlockSpec((BATCH, 1, c, DN_WIDTH), lambda s: (0, pos(s), 0, d)),
                pl.BlockSpec((BATCH, 1, DN_HEADS, c + DN_HEAD_DIM, c), lambda s: (0, pos(s), d, 0, 0)),
                pl.BlockSpec((BATCH, 1, DN_CHAINS, LANES), lambda s: (0, pos(s), 0, 0))]

    fwd = lambda s: s
    return pl.pallas_call(
        _dnscan_kernel,
        grid=(nch,),
        in_specs=specs(fwd, 0) + specs(bwd, 1),
        out_specs=[pl.BlockSpec((BATCH, 1, c, DN_WIDTH), lambda s: (0, s, 0, 0)),
                   pl.BlockSpec((BATCH, 1, c, DN_WIDTH), lambda s: (0, bwd(s), 0, 0))],
        out_shape=[jax.ShapeDtypeStruct((BATCH, nch, c, DN_WIDTH), F32)] * 2,
        scratch_shapes=[pltpu.VMEM((BATCH * DN_CHAINS, DN_HEAD_DIM, DN_HEAD_DIM), F32)],
        compiler_params=_cparams("arbitrary"),
        name="dnscan",
    )(wq, u, lk, eg, wq, u, lk, eg)


def _pool_matrices(tile, seg):
    t = np.arange(tile)
    p = t % seg
    mats = []
    for win in POOL_WINDOWS:
        lo = np.clip(p - win // 2, 0, seg)
        hi = np.clip(p + win - win // 2, 0, seg)
        same = (t[:, None] // seg) == (t[None, :] // seg)
        inside = same & (p[None, :] >= lo[:, None]) & (p[None, :] < hi[:, None])
        mats.append(inside / (hi - lo)[:, None].astype(np.float64) - np.eye(tile))
    return np.stack(mats).astype(np.float32)


def _merge_kernel(x_ref, of_ref, ob_ref, z_ref, pin_ref, gd_ref, gp_ref, modv_ref, on_ref, pm_ref, pw_ref, ps_ref,
                  wud_ref, wup_ref, wo_ref, o_ref, *, seq):
    g = _group_of_row(pl.program_id(0) * ROW_TILE, seq)
    mod = modv_ref[pl.ds(g, 1), :]
    gate = mod[:, 2 * D_MODEL:3 * D_MODEL]
    o = (of_ref[0] + ob_ref[0]).reshape(ROW_TILE, DN_WIDTH)
    z = z_ref[...]
    onw = on_ref[...]
    ys = []
    for hh in range(DN_HEADS):
        lo, hi = hh * DN_HEAD_DIM, (hh + 1) * DN_HEAD_DIM
        oh = o[:, lo:hi]
        ys.append(oh * lax.rsqrt(jnp.mean(oh * oh, axis=-1, keepdims=True) + EPS) * onw * _silu(z[:, lo:hi]))
    y_dn = jnp.concatenate(ys, axis=1)
    pin = pin_ref[...]
    yp = []
    for gi in range(POOL_GROUPS):
        lo, hi = gi * POOL_GW, (gi + 1) * POOL_GW
        pooled = jnp.dot(pm_ref[0, gi], pin[:, lo:hi], precision=HIGHEST, preferred_element_type=F32)
        yp.append(jnp.dot(pooled, pw_ref[gi], preferred_element_type=F32))
    y_pool = jnp.concatenate(yp, axis=1) * ps_ref[...]
    m = (jax.nn.sigmoid(gd_ref[...]) * jnp.dot(y_dn.astype(BF16), wud_ref[...], preferred_element_type=F32)
         + jax.nn.sigmoid(gp_ref[...]) * jnp.dot(y_pool.astype(BF16), wup_ref[...], preferred_element_type=F32))
    out = jnp.dot(m.astype(BF16), wo_ref[...], preferred_element_type=F32)
    o_ref[...] = x_ref[...] + gate * out


def _merge(xall, o_f, o_b, z, pin, gd, gp, modv, onorm, pool_w, pool_scale, wud, wup, wo, seq, ctx, rows_out):
    assert ctx == ROW_TILE and seq % ROW_TILE == 0
    lat_tiles = BATCH * seq // ROW_TILE
    tiles_per_seq = seq // ROW_TILE
    cpt = ROW_TILE // DN_CHUNK
    pm = jnp.asarray(np.stack([_pool_matrices(ROW_TILE, GRID_W), _pool_matrices(ROW_TILE, ctx)]))
    row = lambda n: pl.BlockSpec((ROW_TILE, n), lambda i: (i, 0))
    full = lambda *s: pl.BlockSpec(s, lambda i: (0,) * len(s))

    def o_pos(i):
        lat = i < lat_tiles
        return (jnp.where(lat, i // tiles_per_seq, i - lat_tiles), jnp.where(lat, 1 + i % tiles_per_seq, 0), 0, 0)

    o_spec = pl.BlockSpec((1, cpt, DN_CHUNK, DN_WIDTH), o_pos)
    return pl.pallas_call(
        functools.partial(_merge_kernel, seq=seq),
        grid=(rows_out // ROW_TILE,),
        in_specs=[
            row(D_MODEL), o_spec, o_spec, row(DN_WIDTH), row(POOL_WIDTH), row(D_MODEL), row(D_MODEL),
            full(8, 6 * D_MODEL), full(1, DN_HEAD_DIM),
            pl.BlockSpec((1, POOL_GROUPS, ROW_TILE, ROW_TILE), lambda i: (jnp.where(i >= lat_tiles, 1, 0), 0, 0, 0)),
            full(POOL_GROUPS, POOL_GW, POOL_GW), full(1, POOL_WIDTH),
            full(DN_WIDTH, D_MODEL), full(POOL_WIDTH, D_MODEL), full(D_MODEL, D_MODEL),
        ],
        out_specs=row(D_MODEL),
        out_shape=jax.ShapeDtypeStruct((rows_out, D_MODEL), F32),
        compiler_params=_cparams("arbitrary"),
        name="merge",
    )(xall, o_f, o_b, z, pin, gd, gp, modv, onorm.reshape(1, DN_HEAD_DIM), pm, pool_w,
      pool_scale.reshape(1, POOL_WIDTH), wud, wup, wo)


def _col_max(x):
    return jnp.max(x, axis=0, keepdims=True)


def _col_min(x):
    return jnp.min(x, axis=0, keepdims=True)


def _top_select(scores, iota):
    n = float(scores[0].shape[0])
    cur = list(scores)
    vals = [[] for _ in cur]
    idxs = [[] for _ in cur]
    for _ in range(PEER_TOPK):
        for j in range(len(cur)):
            m = _col_max(cur[j])
            cand = jnp.where(cur[j] == m, iota, n)
            idx = _col_min(cand)
            cur[j] = jnp.where(cand == idx, NEG_INF, cur[j])
            vals[j].append(m)
            idxs[j].append(idx)
    return [(jnp.concatenate(v, axis=0), jnp.concatenate(i, axis=0)) for v, i in zip(vals, idxs)]


def _peer_route_kernel(x_ref, modv_ref, nw_ref, wq_ref, keys_ref, hn_ref, n1_ref, e1_ref, r2_ref, e2_ref,
                       q_scr, *, seq):
    g = _group_of_row(pl.program_id(0) * ROW_TILE, seq)
    mod = modv_ref[pl.ds(g, 1), :]
    hn = _norm_mod(x_ref[...], nw_ref[...], mod, 3).astype(BF16)
    hn_ref[...] = hn
    q = jnp.dot(hn, wq_ref[...], preferred_element_type=F32)
    nsub = ROW_TILE // LANES
    for j in range(2 * PEER_HEADS):
        for t in range(nsub):
            q_scr[j * nsub + t] = q[t * LANES:(t + 1) * LANES, j * PEER_HALF:(j + 1) * PEER_HALF]
    k = PEER_TOPK
    iota_k = lax.broadcasted_iota(jnp.int32, (N_KEYS, LANES), 0).astype(F32)
    iota_a = lax.broadcasted_iota(jnp.int32, (k, LANES), 0).astype(F32)

    def head_sub(it, carry):
        hh = it // nsub
        t = it % nsub
        s1 = lax.dot_general(keys_ref[2 * hh], q_scr[2 * hh * nsub + t], NT_DIMS, preferred_element_type=F32)
        s2 = lax.dot_general(keys_ref[2 * hh + 1], q_scr[(2 * hh + 1) * nsub + t], NT_DIMS,
                             preferred_element_type=F32)
        (c, idx1), (d, idx2) = _top_select([s1, s2], iota_k)

        cnt = jnp.zeros((k, LANES), F32)
        for _ in range(k):
            dn = jnp.full((k, LANES), NEG_INF, F32)
            for b in range(k):
                dn = jnp.where(cnt == float(b), d[b:b + 1, :], dn)
            f = c + dn
            cand = jnp.where(f == _col_max(f), iota_a, float(k))
            cnt = cnt + (cand == _col_min(cand)).astype(F32)
        e1c = jnp.exp(c - c[0:1, :])
        e2d = jnp.exp(d - d[0:1, :])
        part = jnp.zeros((k, LANES), F32)
        for b in range(k):
            part = part + jnp.where(cnt > float(b), e2d[b:b + 1, :], 0.0)
        zsum = jnp.sum(e1c * part, axis=0, keepdims=True)
        n1 = jnp.zeros((N_KEYS, LANES), F32)
        r2 = jnp.full((N_KEYS, LANES), float(k), F32)
        for a in range(k):
            n1 = jnp.where(iota_k == idx1[a:a + 1, :], cnt[a:a + 1, :], n1)
            r2 = jnp.where(iota_k == idx2[a:a + 1, :], float(a), r2)
        n1_ref[hh, t] = n1
        e1_ref[hh, t] = jnp.exp(s1 - c[0:1, :]) / zsum
        r2_ref[hh, t] = r2.astype(BF16)
        e2_ref[hh, t] = jnp.exp(s2 - d[0:1, :]).astype(BF16)
        return carry

    lax.fori_loop(0, PEER_HEADS * nsub, head_sub, 0)


def _peer_route(xall, modv, norm_w, wq, keys, seq, rows_out):
    row = lambda n: pl.BlockSpec((ROW_TILE, n), lambda i: (i, 0))
    full = lambda *s: pl.BlockSpec(s, lambda i: (0,) * len(s))
    tab = pl.BlockSpec((PEER_HEADS, ROW_TILE // LANES, N_KEYS, LANES), lambda i: (0, i, 0, 0))
    tab_shape = lambda dt: jax.ShapeDtypeStruct((PEER_HEADS, rows_out // LANES, N_KEYS, LANES), dt)
    return pl.pallas_call(
        functools.partial(_peer_route_kernel, seq=seq),
        grid=(rows_out // ROW_TILE,),
        in_specs=[row(D_MODEL), full(8, 6 * D_MODEL), full(1, D_MODEL), full(D_MODEL, PEER_HEADS * PEER_QDIM),
                  full(2 * PEER_HEADS, N_KEYS, PEER_HALF)],
        out_specs=[row(D_MODEL), tab, tab, tab, tab],
        out_shape=[jax.ShapeDtypeStruct((rows_out, D_MODEL), BF16), tab_shape(F32), tab_shape(F32),
                   tab_shape(BF16), tab_shape(BF16)],
        scratch_shapes=[pltpu.VMEM((2 * PEER_HEADS * (ROW_TILE // LANES), LANES, PEER_HALF), F32)],
        compiler_params=_cparams("arbitrary"),
        name="peer_route",
    )(xall, modv, norm_w.reshape(1, D_MODEL), wq, keys.reshape(2 * PEER_HEADS, N_KEYS, PEER_HALF))


def _peer_dense_kernel(x_ref, modv_ref, hn_ref, u_ref, vt_ref, n1_ref, e1_ref, r2_ref, e2_ref, fn_ref, o_ref,
                       acc_ref, *, seq, final):
    e = pl.program_id(1)

    @pl.when(e == 0)
    def _():
        acc_ref[...] = jnp.zeros_like(acc_ref)

    hn = hn_ref[...]
    per_slab = PEER_SLAB // N_KEYS
    for sl in range(PEER_EXP_TILE // PEER_SLAB):
        rows = slice(sl * PEER_SLAB, (sl + 1) * PEER_SLAB)
        a = lax.dot_general(u_ref[rows, :], hn, NT_DIMS, preferred_element_type=F32)
        acts = []
        for j in range(per_slab):
            i = sl * per_slab + j
            blocks = []
            for t in range(PEER_TOK_TILE // LANES):
                wsum = None
                for hh in range(PEER_HEADS):
                    keep = r2_ref[hh, t] < n1_ref[hh, t, i:i + 1, :].astype(BF16)
                    term = jnp.where(keep, e2_ref[hh, t], jnp.zeros((), BF16)) * e1_ref[hh, t, i:i + 1, :].astype(BF16)
                    wsum = term if wsum is None else wsum + term
                ajt = a[j * N_KEYS:(j + 1) * N_KEYS, t * LANES:(t + 1) * LANES]
                gelu = 0.5 * ajt * (1.0 + lax.erf(ajt * (2.0 ** -0.5)))
                blocks.append(gelu.astype(BF16) * wsum)
            acts.append(jnp.concatenate(blocks, axis=1))
        act = jnp.concatenate(acts, axis=0)
        acc_ref[...] += jnp.dot(vt_ref[:, rows], act, preferred_element_type=F32)

    @pl.when(e == pl.num_programs(1) - 1)
    def _():
        g = _group_of_row(pl.program_id(0) * PEER_TOK_TILE, seq)
        mod = modv_ref[pl.ds(g, 1), :]
        y = x_ref[...] + mod[:, 5 * D_MODEL:6 * D_MODEL] * acc_ref[...].T
        if final:
            y = y * lax.rsqrt(jnp.mean(y * y, axis=-1, keepdims=True) + EPS) * fn_ref[...]
        o_ref[...] = y


def _peer_dense(xall, modv, hn, u_bf, vt_bf, n1, e1, r2, e2, final_norm, seq, rows_out, final):
    tt, et = PEER_TOK_TILE, PEER_EXP_TILE
    n_first = et // N_KEYS
    tok = lambda n: pl.BlockSpec((tt, n), lambda t, e: (t, 0))
    full = lambda *s: pl.BlockSpec(s, lambda t, e: (0,) * len(s))
    per_first = pl.BlockSpec((PEER_HEADS, tt // LANES, n_first, LANES), lambda t, e: (0, t, e, 0))
    per_second = pl.BlockSpec((PEER_HEADS, tt // LANES, N_KEYS, LANES), lambda t, e: (0, t, 0, 0))
    return pl.pallas_call(
        functools.partial(_peer_dense_kernel, seq=seq, final=final),
        grid=(rows_out // tt, N_EXPERTS // et),
        in_specs=[tok(D_MODEL), full(8, 6 * D_MODEL), tok(D_MODEL),
                  pl.BlockSpec((et, D_MODEL), lambda t, e: (e, 0)),
                  pl.BlockSpec((D_MODEL, et), lambda t, e: (0, e)),
                  per_first, per_first, per_second, per_second, full(1, D_MODEL)],
        out_specs=tok(D_MODEL),
        out_shape=jax.ShapeDtypeStruct((rows_out, D_MODEL), F32),
        scratch_shapes=[pltpu.VMEM((D_MODEL, tt), F32)],
        compiler_params=_cparams("arbitrary", "arbitrary"),
        name="peer_dense",
    )(xall, modv, hn, u_bf, vt_bf, n1, e1, r2, e2, final_norm.reshape(1, D_MODEL))


def _reorder_in_weight(w):
    s = np.cumsum((0, 3 * DN_WIDTH, DN_WIDTH, POOL_WIDTH, 2 * DN_HEADS, 2 * DN_HEADS, D_MODEL, D_MODEL))
    qkv, z, pin, b, a, gd, gp = (w[:, s[i]:s[i + 1]] for i in range(7))
    pad = jnp.zeros((w.shape[0], BA_PAD - 4 * DN_HEADS), w.dtype)
    return jnp.concatenate([qkv, z, pin, gd, gp, b, a, pad], axis=1).astype(BF16)


def _forward(x, c, ctx, c_ctx, w_mod, b_mod, norm_mix, w_in, conv_w, a_log, dt_bias, dn_out_norm, pool_w, pool_scale,
             w_up_dn, w_up_pool, w_out, norm_ffn, peer_wq, peer_keys, peer_u, peer_v, final_norm):
    seq, nctx = x.shape[1], ctx.shape[1]
    nlat = BATCH * seq
    xall = jnp.concatenate([x.reshape(nlat, D_MODEL), ctx.reshape(BATCH * nctx, D_MODEL)], axis=0)
    rows = xall.shape[0]
    cvec = jnp.concatenate([c, c_ctx[None, :], jnp.zeros((8 - BATCH - 1, D_MODEL), F32)], axis=0)
    modv_all = _modulation(cvec, w_mod, b_mod)
    for i in range(DEPTH):
        last = i == DEPTH - 1
        rows_out = nlat if last else rows
        modv = modv_all[i]
        qkv, z, pin, gd, gp, ba = _inproj(xall, modv, norm_mix[i], _reorder_in_weight(w_in[i]), seq)
        q, k, v, bg = _dnconv(qkv, ba, conv_w[i], a_log[i], dt_bias[i], seq, nctx)
        o_f, o_b = _dnscan(*_dnprep(q, k, v, bg, seq, nctx), seq, nctx)
        xall = _merge(xall, o_f, o_b, z, pin, gd, gp, modv, dn_out_norm[i], pool_w[i], pool_scale[i],
                      w_up_dn[i].astype(BF16), w_up_pool[i].astype(BF16), w_out[i].astype(BF16), seq, nctx, rows_out)
        hn, n1, e1, r2, e2 = _peer_route(xall, modv, norm_ffn[i], peer_wq[i].astype(BF16), peer_keys[i], seq, rows_out)
        xall = _peer_dense(xall, modv, hn, peer_u[i].astype(BF16), peer_v[i].T.astype(BF16), n1, e1, r2, e2,
                           final_norm, seq, rows_out, last)
    return xall.reshape(BATCH, seq, D_MODEL)


def kernel(x, c, ctx, c_ctx, w_mod, b_mod, norm_mix, w_in, conv_w, a_log, dt_bias, dn_out_norm, pool_w, pool_scale, w_up_dn, w_up_pool, w_out, norm_ffn, peer_wq, peer_keys, peer_u, peer_v, final_norm):
    return _forward(x, c, ctx, c_ctx, w_mod, b_mod, norm_mix, w_in, conv_w, a_log, dt_bias, dn_out_norm, pool_w,
                    pool_scale, w_up_dn, w_up_pool, w_out, norm_ffn, peer_wq, peer_keys, peer_u, peer_v, final_norm)
```

```python
import functools

import numpy as np
import jax
import jax.numpy as jnp
from jax import lax
from jax.experimental import pallas as pl
from jax.experimental.pallas import tpu as pltpu

D_MODEL = 1024
BATCH = 2
DEPTH = 2
GRID_W = 64
EPS = 1e-6

DN_HEADS = 4
DN_HEAD_DIM = 128
DN_WIDTH = DN_HEADS * DN_HEAD_DIM
SHORT_CONV = 4
DN_CHUNK = 64
DN_CHAINS = 2 * DN_HEADS
DN_PREP_CHUNKS = 2

POOL_WINDOWS = (2, 4, 8, 16)
POOL_GROUPS = 4
POOL_WIDTH = D_MODEL // 2
POOL_GW = POOL_WIDTH // POOL_GROUPS

PEER_HEADS = 8
N_KEYS = 128
N_EXPERTS = N_KEYS * N_KEYS
PEER_TOPK = 16
PEER_QDIM = 256
PEER_HALF = PEER_QDIM // 2

BA_PAD = 128
IN_COLS_R = 3 * DN_WIDTH + DN_WIDTH + POOL_WIDTH + 2 * D_MODEL + BA_PAD

LANES = 128
SUBLANES = 8
ROW_TILE = 256
PEER_TOK_TILE = 512
PEER_EXP_TILE = 1024
PEER_SLAB = 256
HALO = 8
VMEM_LIMIT = 56 * 1024 * 1024

F32 = jnp.float32
BF16 = jnp.bfloat16
HIGHEST = lax.Precision.HIGHEST
NEG_INF = float("-inf")
NT_DIMS = (((1,), (1,)), ((), ()))


def _cparams(*sem):
    return pltpu.CompilerParams(dimension_semantics=sem, vmem_limit_bytes=VMEM_LIMIT)


def _group_of_row(row0, seq):
    return jnp.where(row0 < seq, 0, jnp.where(row0 < 2 * seq, 1, 2))


def _silu(x):
    return x * jax.nn.sigmoid(x)


def _split_bf16(a):
    hi = a.astype(BF16)
    lo = (a - hi.astype(F32)).astype(BF16)
    return hi, lo


def _dot_split(a, b):
    ah, al = _split_bf16(a)
    bh, bl = _split_bf16(b)
    a4 = jnp.concatenate([ah, al, ah, al], axis=1)
    b4 = jnp.concatenate([bh, bh, bl, bl], axis=0)
    return jnp.dot(a4, b4, preferred_element_type=F32)


def _mod_kernel(c_ref, w_ref, b_ref, o_ref):
    o_ref[0] = jnp.dot(_silu(c_ref[...]), w_ref[0], preferred_element_type=F32) + b_ref[0]


def _modulation(cvec, w_mod, b_mod):
    tn = 1536
    return pl.pallas_call(
        _mod_kernel,
        grid=(DEPTH, 6 * D_MODEL // tn),
        in_specs=[
            pl.BlockSpec((8, D_MODEL), lambda l, j: (0, 0)),
            pl.BlockSpec((1, D_MODEL, tn), lambda l, j: (l, 0, j)),
            pl.BlockSpec((1, 1, tn), lambda l, j: (l, 0, j)),
        ],
        out_specs=pl.BlockSpec((1, 8, tn), lambda l, j: (l, 0, j)),
        out_shape=jax.ShapeDtypeStruct((DEPTH, 8, 6 * D_MODEL), F32),
        compiler_params=_cparams("arbitrary", "arbitrary"),
        name="modulation",
    )(cvec, w_mod, b_mod.reshape(DEPTH, 1, 6 * D_MODEL))


def _norm_mod(x, nw, mod, k):
    ms = jnp.mean(x * x, axis=-1, keepdims=True)
    xn = x * lax.rsqrt(ms + EPS) * nw
    sh = mod[:, k * D_MODEL:(k + 1) * D_MODEL]
    sc = mod[:, (k + 1) * D_MODEL:(k + 2) * D_MODEL]
    return xn * (1 + sc) + sh


def _inproj_kernel(x_ref, modv_ref, nw_ref, w_ref, qkv_ref, z_ref, pin_ref, gd_ref, gp_ref, ba_ref, *, seq):
    g = _group_of_row(pl.program_id(0) * ROW_TILE, seq)
    mod = modv_ref[pl.ds(g, 1), :]
    h = _norm_mod(x_ref[...], nw_ref[...], mod, 0)
    y = jnp.dot(h.astype(BF16), w_ref[...], preferred_element_type=F32)
    o = 0
    for ref in (qkv_ref, z_ref, pin_ref, gd_ref, gp_ref, ba_ref):
        n = ref.shape[1]
        ref[...] = y[:, o:o + n]
        o += n


def _inproj(xall, modv, norm_w, w_in_r, seq):
    rows = xall.shape[0]
    widths = (3 * DN_WIDTH, DN_WIDTH, POOL_WIDTH, D_MODEL, D_MODEL, BA_PAD)
    return pl.pallas_call(
        functools.partial(_inproj_kernel, seq=seq),
        grid=(rows // ROW_TILE,),
        in_specs=[
            pl.BlockSpec((ROW_TILE, D_MODEL), lambda i: (i, 0)),
            pl.BlockSpec((8, 6 * D_MODEL), lambda i: (0, 0)),
            pl.BlockSpec((1, D_MODEL), lambda i: (0, 0)),
            pl.BlockSpec((D_MODEL, IN_COLS_R), lambda i: (0, 0)),
        ],
        out_specs=[pl.BlockSpec((ROW_TILE, n), lambda i: (i, 0)) for n in widths],
        out_shape=[jax.ShapeDtypeStruct((rows, n), F32) for n in widths],
        compiler_params=_cparams("arbitrary"),
        name="inproj",
    )(xall, modv, norm_w.reshape(1, D_MODEL), w_in_r)


def _dnconv_kernel(cur_ref, prev_ref, next_ref, cw_ref, ba_ref, alog_ref, dtb_ref,
                   q_ref, k_ref, v_ref, bg_ref, *, seq, ctx):
    row0 = pl.program_id(0) * ROW_TILE
    nlat = BATCH * seq
    is_start = (row0 == 0) | (row0 == seq) | (row0 == nlat) | (row0 == nlat + ctx)
    row1 = row0 + ROW_TILE
    is_end = (row1 == seq) | (row1 == nlat) | (row1 == nlat + ctx) | (row1 == nlat + BATCH * ctx)
    prev = jnp.where(is_start, 0.0, prev_ref[...])
    nxt = jnp.where(is_end, 0.0, next_ref[...])
    ext = jnp.concatenate([prev, cur_ref[...], nxt], axis=0)
    left = SHORT_CONV // 2
    cw = cw_ref[...]
    y = None
    for j in range(SHORT_CONV):
        o = HALO - left + j
        term = ext[o:o + ROW_TILE, :] * cw[j:j + 1, :]
        y = term if y is None else y + term
    y = _silu(y)
    for hh in range(DN_HEADS):
        lo, hi = hh * DN_HEAD_DIM, (hh + 1) * DN_HEAD_DIM
        qh = y[:, lo:hi]
        kh = y[:, DN_WIDTH + lo:DN_WIDTH + hi]
        q_ref[:, lo:hi] = qh * lax.rsqrt(jnp.sum(qh * qh, axis=-1, keepdims=True) + EPS) * (DN_HEAD_DIM ** -0.5)
        k_ref[:, lo:hi] = kh * lax.rsqrt(jnp.sum(kh * kh, axis=-1, keepdims=True) + EPS)
    v_ref[...] = y[:, 2 * DN_WIDTH:]
    ba = ba_ref[...]
    beta = jax.nn.sigmoid(ba)
    xs = ba + dtb_ref[...]
    softplus = jnp.maximum(xs, 0.0) + jnp.log(1.0 + jnp.exp(-jnp.abs(xs)))
    gdec = -jnp.exp(alog_ref[...]) * softplus
    col = lax.broadcasted_iota(jnp.int32, ba.shape, 1)
    bg_ref[...] = jnp.where(col < 2 * DN_HEADS, beta, jnp.where(col < 4 * DN_HEADS, gdec, 0.0))


def _dnconv(qkv, ba, conv_w, a_log, dt_bias, seq, ctx):
    rows = qkv.shape[0]
    nh = ROW_TILE // HALO
    last = rows // HALO - 1
    pad = jnp.zeros((2 * DN_HEADS,), F32)
    tail = jnp.zeros((BA_PAD - 4 * DN_HEADS,), F32)
    alog = jnp.concatenate([pad, a_log.reshape(-1), tail]).reshape(1, BA_PAD)
    dtb = jnp.concatenate([pad, dt_bias.reshape(-1), tail]).reshape(1, BA_PAD)
    w3 = 3 * DN_WIDTH
    return pl.pallas_call(
        functools.partial(_dnconv_kernel, seq=seq, ctx=ctx),
        grid=(rows // ROW_TILE,),
        in_specs=[
            pl.BlockSpec((ROW_TILE, w3), lambda i: (i, 0)),
            pl.BlockSpec((HALO, w3), lambda i: (jnp.maximum(i * nh - 1, 0), 0)),
            pl.BlockSpec((HALO, w3), lambda i: (jnp.minimum((i + 1) * nh, last), 0)),
            pl.BlockSpec((SHORT_CONV, w3), lambda i: (0, 0)),
            pl.BlockSpec((ROW_TILE, BA_PAD), lambda i: (i, 0)),
            pl.BlockSpec((1, BA_PAD), lambda i: (0, 0)),
            pl.BlockSpec((1, BA_PAD), lambda i: (0, 0)),
        ],
        out_specs=[pl.BlockSpec((ROW_TILE, DN_WIDTH), lambda i: (i, 0))] * 3
        + [pl.BlockSpec((ROW_TILE, BA_PAD), lambda i: (i, 0))],
        out_shape=[jax.ShapeDtypeStruct((rows, DN_WIDTH), F32)] * 3
        + [jax.ShapeDtypeStruct((rows, BA_PAD), F32)],
        compiler_params=_cparams("arbitrary"),
        name="dnconv",
    )(qkv, qkv, qkv, conv_w, ba, alog, dtb)


def _dnprep_kernel(q_ref, k_ref, v_ref, bg_ref, bgt_ref, wq_ref, u_ref, lk_ref, eg_ref):
    c = DN_CHUNK
    ri = lax.broadcasted_iota(jnp.int32, (c, c), 0)
    ci = lax.broadcasted_iota(jnp.int32, (c, c), 1)
    dirs = ((ri >= ci, ri > ci, ci >= ri, c - 1), (ri <= ci, ri < ci, ci <= ri, 0))
    chains = []
    for j in range(DN_PREP_CHUNKS):
        rows = slice(j * c, (j + 1) * c)
        bg = bg_ref[rows, :]
        bgh, bgl = _split_bf16(bg)
        bgth, bgtl = _split_bf16(bgt_ref[j])
        bg2 = jnp.concatenate([bgh, bgl], axis=0)
        bgt2 = jnp.concatenate([bgth, bgtl], axis=1)
        egs = []
        for d, (incl, strict, incl_t, last) in enumerate(dirs):
            m = incl.astype(BF16)
            mt = incl_t.astype(BF16)
            gc_all = jnp.dot(jnp.concatenate([m, m], axis=1), bg2, preferred_element_type=F32)
            gr_all = jnp.dot(bgt2, jnp.concatenate([mt, mt], axis=0), preferred_element_type=F32)
            for hh in range(DN_HEADS):
                lo, hi = hh * DN_HEAD_DIM, (hh + 1) * DN_HEAD_DIM
                ch = d * DN_HEADS + hh
                gcol = 2 * DN_HEADS + ch
                q = q_ref[rows, lo:hi]
                k = k_ref[rows, lo:hi]
                v = v_ref[rows, lo:hi]
                beta = bg[:, ch:ch + 1]
                gc = gc_all[:, gcol:gcol + 1]
                gr = gr_all[gcol:gcol + 1, :]
                glast = gc_all[last:last + 1, gcol:gcol + 1]
                decay = jnp.where(incl, jnp.exp(jnp.minimum(gc - gr, 0.0)), 0.0)
                kk = lax.dot_general(k, k, NT_DIMS, preferred_element_type=F32)
                qk = lax.dot_general(q, k, NT_DIMS, preferred_element_type=F32)
                egc = jnp.exp(gc)
                cols = slice(ch * DN_HEAD_DIM, (ch + 1) * DN_HEAD_DIM)
                wq_ref[0, j, c:2 * c, cols] = (q * egc).astype(BF16)
                lk_ref[0, j, ch, 0:c, :] = jnp.where(incl, qk * decay, 0.0).astype(BF16)
                lk_ref[0, j, ch, c:, :] = (k * jnp.exp(glast - gc)).T.astype(BF16)
                egs.append(jnp.broadcast_to(jnp.exp(glast), (1, LANES)))
                chains.append((j, cols, jnp.where(strict, -(beta * kk * decay), 0.0),
                               jnp.concatenate([k * (beta * egc), v * beta], axis=1)))
        eg_ref[0, j] = jnp.concatenate(egs, axis=0)
    ps = [ch[2] for ch in chains]
    ys = [ch[3] for ch in chains]
    ys = [y + _dot_split(p, y) for p, y in zip(ps, ys)]
    for _ in range(int(np.log2(c)) - 1):
        ps = [_dot_split(p, p) for p in ps]
        ys = [y + _dot_split(p, y) for p, y in zip(ps, ys)]
    for (j, cols, _, _), y in zip(chains, ys):
        wq_ref[0, j, 0:c, cols] = y[:, :DN_HEAD_DIM].astype(BF16)
        u_ref[0, j, :, cols] = y[:, DN_HEAD_DIM:]


def _dnprep(q, k, v, bg, seq, ctx):
    rows = q.shape[0]
    c = DN_CHUNK
    n = DN_PREP_CHUNKS
    nct, nlt = ctx // c, seq // c
    assert nct % n == 0 and nlt % n == 0
    nch = nct + nlt
    bgt = bg[:, :4 * DN_HEADS].reshape(rows // c, c, 4 * DN_HEADS).transpose(0, 2, 1)

    def seq_pos(i):
        i = i * n
        lat = i < BATCH * nlt
        j = i - BATCH * nlt
        return jnp.where(lat, i // nlt, j // nct), jnp.where(lat, nct + i % nlt, j % nct) // n

    def out_spec(*tail):
        return pl.BlockSpec((1, n) + tail, lambda i: seq_pos(i) + (0,) * len(tail))

    wide = pl.BlockSpec((n * c, DN_WIDTH), lambda i: (i, 0))
    width = DN_CHAINS * DN_HEAD_DIM
    return pl.pallas_call(
        _dnprep_kernel,
        grid=(rows // (n * c),),
        in_specs=[wide, wide, wide,
                  pl.BlockSpec((n * c, BA_PAD), lambda i: (i, 0)),
                  pl.BlockSpec((n, 4 * DN_HEADS, c), lambda i: (i, 0, 0))],
        out_specs=[out_spec(2 * c, width), out_spec(c, width), out_spec(DN_CHAINS, c + DN_HEAD_DIM, c),
                   out_spec(DN_CHAINS, LANES)],
        out_shape=[jax.ShapeDtypeStruct((BATCH, nch, 2 * c, width), BF16),
                   jax.ShapeDtypeStruct((BATCH, nch, c, width), F32),
                   jax.ShapeDtypeStruct((BATCH, nch, DN_CHAINS, c + DN_HEAD_DIM, c), BF16),
                   jax.ShapeDtypeStruct((BATCH, nch, DN_CHAINS, LANES), F32)],
        compiler_params=_cparams("arbitrary"),
        name="dnprep",
    )(q, k, v, bg, bgt)


def _dnscan_kernel(wqf_ref, uf_ref, lkf_ref, egf_ref, wqb_ref, ub_ref, lkb_ref, egb_ref, of_ref, ob_ref, s_ref):
    @pl.when(pl.program_id(0) == 0)
    def _():
        s_ref[...] = jnp.zeros_like(s_ref)

    c = DN_CHUNK
    dirs = ((wqf_ref, uf_ref, lkf_ref, egf_ref, of_ref), (wqb_ref, ub_ref, lkb_ref, egb_ref, ob_ref))
    chains = [(d, b, hh) for d in range(2) for b in range(BATCH) for hh in range(DN_HEADS)]
    sidx = lambda d, b, hh: (b * 2 + d) * DN_HEADS + hh
    cols = lambda hh: slice(hh * DN_HEAD_DIM, (hh + 1) * DN_HEAD_DIM)
    ss = [s_ref[sidx(*ch)] for ch in chains]
    r1 = [jnp.dot(dirs[d][0][b, 0, :, cols(hh)], s.astype(BF16), preferred_element_type=F32)
          for (d, b, hh), s in zip(chains, ss)]
    vn = [dirs[d][1][b, 0, :, cols(hh)] - r[:c] for (d, b, hh), r in zip(chains, r1)]
    r2 = [jnp.dot(dirs[d][2][b, 0, hh], v.astype(BF16), preferred_element_type=F32)
          for (d, b, hh), v in zip(chains, vn)]
    for (d, b, hh), s, a1, a2 in zip(chains, ss, r1, r2):
        dirs[d][4][b, 0, :, cols(hh)] = a1[c:] + a2[:c]
        row = d * DN_HEADS + hh
        s_ref[sidx(d, b, hh)] = s * dirs[d][3][b, 0, row:row + 1, :] + a2[c:]


def _dnscan(wq, u, lk, eg, seq, ctx):
    c = DN_CHUNK
    nct, nlt = ctx // c, seq // c
    nch = nct + nlt

    def bwd(s):
        return jnp.where(s < nct, nct - 1 - s, 2 * nct + nlt - 1 - s)

    def specs(pos, d):
        return [pl.BlockSpec((BATCH, 1, 2 * c, DN_WIDTH), lambda s: (0, pos(s), 0, d)),
                pl.BlockSpec((BATCH, 1, c, DN_WIDTH), lambda s: (0, pos(s), 0, d)),
                pl.BlockSpec((BATCH, 1, DN_HEADS, c + DN_HEAD_DIM, c), lambda s: (0, pos(s), d, 0, 0)),
                pl.BlockSpec((BATCH, 1, DN_CHAINS, LANES), lambda s: (0, pos(s), 0, 0))]

    fwd = lambda s: s
    return pl.pallas_call(
        _dnscan_kernel,
        grid=(nch,),
        in_specs=specs(fwd, 0) + specs(bwd, 1),
        out_specs=[pl.BlockSpec((BATCH, 1, c, DN_WIDTH), lambda s: (0, s, 0, 0)),
                   pl.BlockSpec((BATCH, 1, c, DN_WIDTH), lambda s: (0, bwd(s), 0, 0))],
        out_shape=[jax.ShapeDtypeStruct((BATCH, nch, c, DN_WIDTH), F32)] * 2,
        scratch_shapes=[pltpu.VMEM((BATCH * DN_CHAINS, DN_HEAD_DIM, DN_HEAD_DIM), F32)],
        compiler_params=_cparams("arbitrary"),
        name="dnscan",
    )(wq, u, lk, eg, wq, u, lk, eg)


def _pool_matrices(tile, seg):
    t = np.arange(tile)
    p = t % seg
    mats = []
    for win in POOL_WINDOWS:
        lo = np.clip(p - win // 2, 0, seg)
        hi = np.clip(p + win - win // 2, 0, seg)
        same = (t[:, None] // seg) == (t[None, :] // seg)
        inside = same & (p[None, :] >= lo[:, None]) & (p[None, :] < hi[:, None])
        mats.append(inside / (hi - lo)[:, None].astype(np.float64) - np.eye(tile))
    return np.stack(mats).astype(np.float32)


def _merge_kernel(x_ref, of_ref, ob_ref, z_ref, pin_ref, gd_ref, gp_ref, modv_ref, on_ref, pm_ref, pw_ref, ps_ref,
                  wud_ref, wup_ref, wo_ref, o_ref, *, seq):
    g = _group_of_row(pl.program_id(0) * ROW_TILE, seq)
    mod = modv_ref[pl.ds(g, 1), :]
    gate = mod[:, 2 * D_MODEL:3 * D_MODEL]
    o = (of_ref[0] + ob_ref[0]).reshape(ROW_TILE, DN_WIDTH)
    z = z_ref[...]
    onw = on_ref[...]
    ys = []
    for hh in range(DN_HEADS):
        lo, hi = hh * DN_HEAD_DIM, (hh + 1) * DN_HEAD_DIM
        oh = o[:, lo:hi]
        ys.append(oh * lax.rsqrt(jnp.mean(oh * oh, axis=-1, keepdims=True) + EPS) * onw * _silu(z[:, lo:hi]))
    y_dn = jnp.concatenate(ys, axis=1)
    pin = pin_ref[...]
    yp = []
    for gi in range(POOL_GROUPS):
        lo, hi = gi * POOL_GW, (gi + 1) * POOL_GW
        pooled = jnp.dot(pm_ref[0, gi], pin[:, lo:hi], precision=HIGHEST, preferred_element_type=F32)
        yp.append(jnp.dot(pooled, pw_ref[gi], preferred_element_type=F32))
    y_pool = jnp.concatenate(yp, axis=1) * ps_ref[...]
    m = (jax.nn.sigmoid(gd_ref[...]) * jnp.dot(y_dn.astype(BF16), wud_ref[...], preferred_element_type=F32)
         + jax.nn.sigmoid(gp_ref[...]) * jnp.dot(y_pool.astype(BF16), wup_ref[...], preferred_element_type=F32))
    out = jnp.dot(m.astype(BF16), wo_ref[...], preferred_element_type=F32)
    o_ref[...] = x_ref[...] + gate * out


def _merge(xall, o_f, o_b, z, pin, gd, gp, modv, onorm, pool_w, pool_scale, wud, wup, wo, seq, ctx, rows_out):
    assert ctx == ROW_TILE and seq % ROW_TILE == 0
    lat_tiles = BATCH * seq // ROW_TILE
    tiles_per_seq = seq // ROW_TILE
    cpt = ROW_TILE // DN_CHUNK
    pm = jnp.asarray(np.stack([_pool_matrices(ROW_TILE, GRID_W), _pool_matrices(ROW_TILE, ctx)]))
    row = lambda n: pl.BlockSpec((ROW_TILE, n), lambda i: (i, 0))
    full = lambda *s: pl.BlockSpec(s, lambda i: (0,) * len(s))

    def o_pos(i):
        lat = i < lat_tiles
        return (jnp.where(lat, i // tiles_per_seq, i - lat_tiles), jnp.where(lat, 1 + i % tiles_per_seq, 0), 0, 0)

    o_spec = pl.BlockSpec((1, cpt, DN_CHUNK, DN_WIDTH), o_pos)
    return pl.pallas_call(
        functools.partial(_merge_kernel, seq=seq),
        grid=(rows_out // ROW_TILE,),
        in_specs=[
            row(D_MODEL), o_spec, o_spec, row(DN_WIDTH), row(POOL_WIDTH), row(D_MODEL), row(D_MODEL),
            full(8, 6 * D_MODEL), full(1, DN_HEAD_DIM),
            pl.BlockSpec((1, POOL_GROUPS, ROW_TILE, ROW_TILE), lambda i: (jnp.where(i >= lat_tiles, 1, 0), 0, 0, 0)),
            full(POOL_GROUPS, POOL_GW, POOL_GW), full(1, POOL_WIDTH),
            full(DN_WIDTH, D_MODEL), full(POOL_WIDTH, D_MODEL), full(D_MODEL, D_MODEL),
        ],
        out_specs=row(D_MODEL),
        out_shape=jax.ShapeDtypeStruct((rows_out, D_MODEL), F32),
        compiler_params=_cparams("arbitrary"),
        name="merge",
    )(xall, o_f, o_b, z, pin, gd, gp, modv, onorm.reshape(1, DN_HEAD_DIM), pm, pool_w,
      pool_scale.reshape(1, POOL_WIDTH), wud, wup, wo)


def _col_max(x):
    return jnp.max(x, axis=0, keepdims=True)


def _col_min(x):
    return jnp.min(x, axis=0, keepdims=True)


def _oddeven_merge(lo, hi, r):
    step = r * 2
    if step < hi - lo:
        yield from _oddeven_merge(lo, hi, step)
        yield from _oddeven_merge(lo + r, hi, step)
        yield from [(i, i + r) for i in range(lo + r, hi - r, step)]
    else:
        yield (lo, lo + r)


def _oddeven_merge_sort(lo, hi):
    if hi - lo >= 1:
        mid = lo + (hi - lo) // 2
        yield from _oddeven_merge_sort(lo, mid)
        yield from _oddeven_merge_sort(mid + 1, hi)
        yield from _oddeven_merge(lo, hi, 1)


def _exchange(v, i, j):
    v[i], v[j] = jnp.maximum(v[i], v[j]), jnp.minimum(v[i], v[j])


def _sorted_top(blocks):
    k = PEER_TOPK
    v = list(blocks)
    for i, j in _oddeven_merge_sort(0, k - 1):
        _exchange(v, i, j)
    shift = SUBLANES // 2
    while shift >= 1:
        w = [pltpu.roll(x, shift, 0) for x in v]
        v = [jnp.maximum(v[j], w[k - 1 - j]) for j in range(k)]
        stride = k // 2
        while stride >= 1:
            for i in range(k):
                if i & stride == 0:
                    _exchange(v, i, i + stride)
            stride //= 2
        shift //= 2
    return v


def _sublane_total(x):
    shift = SUBLANES // 2
    while shift >= 1:
        x = x + pltpu.roll(x, shift, 0)
        shift //= 2
    return x


def _unambiguous(blocks, top):
    ok = top[0] > top[1]
    for a in range(1, PEER_TOPK - 1):
        ok = ok & (top[a] > top[a + 1])
    ge = None
    for blk in blocks:
        one = jnp.where(blk >= top[PEER_TOPK - 1], 1.0, 0.0)
        ge = one if ge is None else ge + one
    return ok & (_sublane_total(ge) == float(PEER_TOPK))


def _top_select(scores, iota):
    n = float(scores[0].shape[0])
    cur = list(scores)
    idxs = [[] for _ in cur]
    for _ in range(PEER_TOPK):
        for j in range(len(cur)):
            m = _col_max(cur[j])
            cand = jnp.where(cur[j] == m, iota, n)
            idx = _col_min(cand)
            cur[j] = jnp.where(cand == idx, NEG_INF, cur[j])
            idxs[j].append(idx)
    return [jnp.concatenate(i, axis=0) for i in idxs]


def _peer_route_kernel(x_ref, modv_ref, nw_ref, wq_ref, keys_ref, hn_ref, n1_ref, e1_ref, r2_ref, e2_ref,
                       q_scr, *, seq):
    g = _group_of_row(pl.program_id(0) * ROW_TILE, seq)
    mod = modv_ref[pl.ds(g, 1), :]
    hn = _norm_mod(x_ref[...], nw_ref[...], mod, 3).astype(BF16)
    hn_ref[...] = hn
    q = jnp.dot(hn, wq_ref[...], preferred_element_type=F32)
    nsub = ROW_TILE // LANES
    for j in range(2 * PEER_HEADS):
        for t in range(nsub):
            q_scr[j, t] = q[t * LANES:(t + 1) * LANES, j * PEER_HALF:(j + 1) * PEER_HALF]
    k = PEER_TOPK
    nblk = N_KEYS // SUBLANES
    iota_k = lax.broadcasted_iota(jnp.int32, (N_KEYS, LANES), 0).astype(F32)
    iota_a = lax.broadcasted_iota(jnp.int32, (k, LANES), 0).astype(F32)
    split = lambda s: [s[SUBLANES * j:SUBLANES * (j + 1), :] for j in range(nblk)]
    rep = lambda row: jnp.broadcast_to(row, (SUBLANES, LANES))

    def head(hh, carry):
        for t in range(nsub):
            lanes = slice(t * LANES, (t + 1) * LANES)
            s1 = lax.dot_general(keys_ref[2 * hh], q_scr[2 * hh, t], NT_DIMS, preferred_element_type=F32)
            s2 = lax.dot_general(keys_ref[2 * hh + 1], q_scr[2 * hh + 1, t], NT_DIMS, preferred_element_type=F32)
            b1, b2 = split(s1), split(s2)
            top1, top2 = _sorted_top(b1), _sorted_top(b2)
            c = jnp.concatenate([x[0:1, :] for x in top1], axis=0)
            d = jnp.concatenate([x[0:1, :] for x in top2], axis=0)

            cnt = jnp.zeros((k, LANES), F32)
            for _ in range(k):
                dn = jnp.full((k, LANES), NEG_INF, F32)
                for b in range(k):
                    dn = jnp.where(cnt == float(b), d[b:b + 1, :], dn)
                f = c + dn
                cand = jnp.where(f == _col_max(f), iota_a, float(k))
                cnt = cnt + (cand == _col_min(cand)).astype(F32)
            e1c = jnp.exp(c - c[0:1, :])
            e2d = jnp.exp(d - d[0:1, :])
            part = jnp.zeros((k, LANES), F32)
            for b in range(k):
                part = part + jnp.where(cnt > float(b), e2d[b:b + 1, :], 0.0)
            zsum = jnp.sum(e1c * part, axis=0, keepdims=True)

            def by_value():
                n1b = [jnp.zeros((SUBLANES, LANES), F32)] * nblk
                r2b = [jnp.full((SUBLANES, LANES), float(k), F32)] * nblk
                for a in range(k):
                    cnt_a = rep(cnt[a:a + 1, :])
                    n1b = [jnp.where(blk == top1[a], cnt_a, acc) for blk, acc in zip(b1, n1b)]
                    r2b = [jnp.where(blk == top2[a], float(a), acc) for blk, acc in zip(b2, r2b)]
                return jnp.concatenate(n1b, axis=0), jnp.concatenate(r2b, axis=0)

            def by_index():
                idx1, idx2 = _top_select([s1, s2], iota_k)
                n1 = jnp.zeros((N_KEYS, LANES), F32)
                r2 = jnp.full((N_KEYS, LANES), float(k), F32)
                for a in range(k):
                    n1 = jnp.where(iota_k == idx1[a:a + 1, :], cnt[a:a + 1, :], n1)
                    r2 = jnp.where(iota_k == idx2[a:a + 1, :], float(a), r2)
                return n1, r2

            clean = jnp.all(_unambiguous(b1, top1) & _unambiguous(b2, top2))
            n1, r2 = lax.cond(clean, by_value, by_index)
            n1_ref[hh, 0, :, lanes] = n1
            e1_ref[hh, 0, :, lanes] = jnp.exp(s1 - c[0:1, :]) / zsum
            r2_ref[hh, 0, :, lanes] = r2.astype(BF16)
            e2_ref[hh, 0, :, lanes] = jnp.exp(s2 - d[0:1, :]).astype(BF16)
        return carry

    lax.fori_loop(0, PEER_HEADS, head, 0)


def _peer_route(xall, modv, norm_w, wq, keys, seq, rows_out):
    row = lambda n: pl.BlockSpec((ROW_TILE, n), lambda i: (i, 0))
    full = lambda *s: pl.BlockSpec(s, lambda i: (0,) * len(s))
    tab = pl.BlockSpec((PEER_HEADS, 1, N_KEYS, ROW_TILE), lambda i: (0, i, 0, 0))
    tab_shape = lambda dt: jax.ShapeDtypeStruct((PEER_HEADS, rows_out // ROW_TILE, N_KEYS, ROW_TILE), dt)
    return pl.pallas_call(
        functools.partial(_peer_route_kernel, seq=seq),
        grid=(rows_out // ROW_TILE,),
        in_specs=[row(D_MODEL), full(8, 6 * D_MODEL), full(1, D_MODEL), full(D_MODEL, PEER_HEADS * PEER_QDIM),
                  full(2 * PEER_HEADS, N_KEYS, PEER_HALF)],
        out_specs=[row(D_MODEL), tab, tab, tab, tab],
        out_shape=[jax.ShapeDtypeStruct((rows_out, D_MODEL), BF16), tab_shape(F32), tab_shape(F32),
                   tab_shape(BF16), tab_shape(BF16)],
        scratch_shapes=[pltpu.VMEM((2 * PEER_HEADS, ROW_TILE // LANES, LANES, PEER_HALF), F32)],
        compiler_params=_cparams("arbitrary"),
        name="peer_route",
    )(xall, modv, norm_w.reshape(1, D_MODEL), wq, keys.reshape(2 * PEER_HEADS, N_KEYS, PEER_HALF))


PEER_PIPE_LAG = 2
PEER_I_GROUP = 1


def _peer_tile(n, lag, n_tiles, n_exp):
    m = jnp.clip(n - lag, 0, n_tiles - 1)
    return m // n_exp, m % n_exp


def _peer_dense_kernel(x_ref, modv_ref, hn_ref, u_ref, vt_ref, n1_ref, e1_ref, r2_ref, e2_ref, fn_ref, o_ref,
                       acc_ref, a_ref, act_ref, *, seq, final, n_tiles, n_exp):
    n = pl.program_id(0)
    p = n % 2

    @pl.when(n == 0)
    def _():
        acc_ref[...] = jnp.zeros_like(acc_ref)
        a_ref[...] = jnp.zeros_like(a_ref)
        act_ref[...] = jnp.zeros_like(act_ref)

    a_ref[p] = lax.dot_general(u_ref[...], hn_ref[...], NT_DIMS, preferred_element_type=F32)

    zero = jnp.zeros((), BF16)
    for t in range(PEER_TOK_TILE // ROW_TILE):
        lanes = slice(t * ROW_TILE, (t + 1) * ROW_TILE)
        for i0 in range(0, PEER_EXP_TILE // N_KEYS, PEER_I_GROUP):
            wsum = [None] * PEER_I_GROUP
            for hh in range(PEER_HEADS):
                r2 = r2_ref[hh, t]
                e2 = e2_ref[hh, t]
                for j in range(PEER_I_GROUP):
                    i = i0 + j
                    keep = r2 < n1_ref[hh, t, i:i + 1, :].astype(BF16)
                    term = jnp.where(keep, e2, zero) * e1_ref[hh, t, i:i + 1, :].astype(BF16)
                    wsum[j] = term if wsum[j] is None else wsum[j] + term
            for j in range(PEER_I_GROUP):
                rows = slice((i0 + j) * N_KEYS, (i0 + j + 1) * N_KEYS)
                a = a_ref[1 - p, rows, lanes]
                gelu = 0.5 * a * (1.0 + lax.erf(a * (2.0 ** -0.5)))
                act_ref[1 - p, rows, lanes] = gelu.astype(BF16) * wsum[j]

    t_c, e_c = _peer_tile(n, PEER_PIPE_LAG, n_tiles, n_exp)
    prev = jnp.where(e_c == 0, 0.0, acc_ref[...])
    acc_ref[...] = prev + jnp.dot(vt_ref[...], act_ref[p], preferred_element_type=F32)

    @pl.when((e_c == n_exp - 1) & (n >= PEER_PIPE_LAG))
    def _():
        g = _group_of_row(t_c * PEER_TOK_TILE, seq)
        mod = modv_ref[pl.ds(g, 1), :]
        y = x_ref[...] + mod[:, 5 * D_MODEL:6 * D_MODEL] * acc_ref[...].T
        if final:
            y = y * lax.rsqrt(jnp.mean(y * y, axis=-1, keepdims=True) + EPS) * fn_ref[...]
        o_ref[...] = y


def _peer_dense(xall, modv, hn, u_bf, vt_bf, n1, e1, r2, e2, final_norm, seq, rows_out, final):
    tt, et = PEER_TOK_TILE, PEER_EXP_TILE
    n_first = et // N_KEYS
    n_exp = N_EXPERTS // et
    n_tiles = (rows_out // tt) * n_exp
    tile = lambda lag: (lambda n: _peer_tile(n, lag, n_tiles, n_exp))
    ta, tb, tc = tile(0), tile(1), tile(PEER_PIPE_LAG)
    full = lambda *s: pl.BlockSpec(s, lambda n: (0,) * len(s))
    per_first = pl.BlockSpec((PEER_HEADS, tt // ROW_TILE, n_first, ROW_TILE), lambda n: (0, tb(n)[0], tb(n)[1], 0))
    per_second = pl.BlockSpec((PEER_HEADS, tt // ROW_TILE, N_KEYS, ROW_TILE), lambda n: (0, tb(n)[0], 0, 0))
    return pl.pallas_call(
        functools.partial(_peer_dense_kernel, seq=seq, final=final, n_tiles=n_tiles, n_exp=n_exp),
        grid=(n_tiles + PEER_PIPE_LAG,),
        in_specs=[pl.BlockSpec((tt, D_MODEL), lambda n: (tc(n)[0], 0)),
                  full(8, 6 * D_MODEL),
                  pl.BlockSpec((tt, D_MODEL), lambda n: (ta(n)[0], 0)),
                  pl.BlockSpec((et, D_MODEL), lambda n: (ta(n)[1], 0)),
                  pl.BlockSpec((D_MODEL, et), lambda n: (0, tc(n)[1])),
                  per_first, per_first, per_second, per_second, full(1, D_MODEL)],
        out_specs=pl.BlockSpec((tt, D_MODEL), lambda n: (tc(n)[0], 0)),
        out_shape=jax.ShapeDtypeStruct((rows_out, D_MODEL), F32),
        scratch_shapes=[pltpu.VMEM((D_MODEL, tt), F32), pltpu.VMEM((2, et, tt), F32), pltpu.VMEM((2, et, tt), BF16)],
        compiler_params=_cparams("arbitrary"),
        name="peer_dense",
    )(xall, modv, hn, u_bf, vt_bf, n1, e1, r2, e2, final_norm.reshape(1, D_MODEL))


def _reorder_in_weight(w):
    s = np.cumsum((0, 3 * DN_WIDTH, DN_WIDTH, POOL_WIDTH, 2 * DN_HEADS, 2 * DN_HEADS, D_MODEL, D_MODEL))
    qkv, z, pin, b, a, gd, gp = (w[:, s[i]:s[i + 1]] for i in range(7))
    pad = jnp.zeros((w.shape[0], BA_PAD - 4 * DN_HEADS), w.dtype)
    return jnp.concatenate([qkv, z, pin, gd, gp, b, a, pad], axis=1).astype(BF16)


def _forward(x, c, ctx, c_ctx, w_mod, b_mod, norm_mix, w_in, conv_w, a_log, dt_bias, dn_out_norm, pool_w, pool_scale,
             w_up_dn, w_up_pool, w_out, norm_ffn, peer_wq, peer_keys, peer_u, peer_v, final_norm):
    seq, nctx = x.shape[1], ctx.shape[1]
    nlat = BATCH * seq
    xall = jnp.concatenate([x.reshape(nlat, D_MODEL), ctx.reshape(BATCH * nctx, D_MODEL)], axis=0)
    rows = xall.shape[0]
    cvec = jnp.concatenate([c, c_ctx[None, :], jnp.zeros((8 - BATCH - 1, D_MODEL), F32)], axis=0)
    modv_all = _modulation(cvec, w_mod, b_mod)
    for i in range(DEPTH):
        last = i == DEPTH - 1
        rows_out = nlat if last else rows
        modv = modv_all[i]
        qkv, z, pin, gd, gp, ba = _inproj(xall, modv, norm_mix[i], _reorder_in_weight(w_in[i]), seq)
        q, k, v, bg = _dnconv(qkv, ba, conv_w[i], a_log[i], dt_bias[i], seq, nctx)
        o_f, o_b = _dnscan(*_dnprep(q, k, v, bg, seq, nctx), seq, nctx)
        xall = _merge(xall, o_f, o_b, z, pin, gd, gp, modv, dn_out_norm[i], pool_w[i], pool_scale[i],
                      w_up_dn[i].astype(BF16), w_up_pool[i].astype(BF16), w_out[i].astype(BF16), seq, nctx, rows_out)
        hn, n1, e1, r2, e2 = _peer_route(xall, modv, norm_ffn[i], peer_wq[i].astype(BF16), peer_keys[i], seq, rows_out)
        xall = _peer_dense(xall, modv, hn, peer_u[i].astype(BF16), peer_v[i].T.astype(BF16), n1, e1, r2, e2,
                           final_norm, seq, rows_out, last)
    return xall.reshape(BATCH, seq, D_MODEL)


def kernel(x, c, ctx, c_ctx, w_mod, b_mod, norm_mix, w_in, conv_w, a_log, dt_bias, dn_out_norm, pool_w, pool_scale, w_up_dn, w_up_pool, w_out, norm_ffn, peer_wq, peer_keys, peer_u, peer_v, final_norm):
    return _forward(x, c, ctx, c_ctx, w_mod, b_mod, norm_mix, w_in, conv_w, a_log, dt_bias, dn_out_norm, pool_w,
                    pool_scale, w_up_dn, w_up_pool, w_out, norm_ffn, peer_wq, peer_keys, peer_u, peer_v, final_norm)
```

```python
import functools

import numpy as np
import jax
import jax.numpy as jnp
from jax import lax
from jax.experimental import pallas as pl
from jax.experimental.pallas import tpu as pltpu

D_MODEL = 1024
BATCH = 2
DEPTH = 2
GRID_W = 64
EPS = 1e-6

DN_HEADS = 4
DN_HEAD_DIM = 128
DN_WIDTH = DN_HEADS * DN_HEAD_DIM
SHORT_CONV = 4
DN_CHUNK = 64
DN_CHAINS = 2 * DN_HEADS
DN_PREP_CHUNKS = 2

POOL_WINDOWS = (2, 4, 8, 16)
POOL_GROUPS = 4
POOL_WIDTH = D_MODEL // 2
POOL_GW = POOL_WIDTH // POOL_GROUPS

PEER_HEADS = 8
N_KEYS = 128
N_EXPERTS = N_KEYS * N_KEYS
PEER_TOPK = 16
PEER_QDIM = 256
PEER_HALF = PEER_QDIM // 2

BA_PAD = 128
IN_COLS_R = 3 * DN_WIDTH + DN_WIDTH + POOL_WIDTH + 2 * D_MODEL + BA_PAD

LANES = 128
SUBLANES = 8
ROW_TILE = 256
PEER_TOK_TILE = 512
PEER_EXP_TILE = 1024
PEER_SLAB = 256
HALO = 8
VMEM_LIMIT = 56 * 1024 * 1024

F32 = jnp.float32
BF16 = jnp.bfloat16
HIGHEST = lax.Precision.HIGHEST
NEG_INF = float("-inf")
NT_DIMS = (((1,), (1,)), ((), ()))


def _cparams(*sem):
    return pltpu.CompilerParams(dimension_semantics=sem, vmem_limit_bytes=VMEM_LIMIT)


def _group_of_row(row0, seq):
    return jnp.where(row0 < seq, 0, jnp.where(row0 < 2 * seq, 1, 2))


def _silu(x):
    return x * jax.nn.sigmoid(x)


def _split_bf16(a):
    hi = a.astype(BF16)
    lo = (a - hi.astype(F32)).astype(BF16)
    return hi, lo


def _bf16_pair_word(x):
    hi = lax.bitcast_convert_type(x.astype(BF16).astype(F32), jnp.uint32)
    return hi | (hi >> 16)


def _pair_word_rows(row):
    tile = pltpu.bitcast(jnp.broadcast_to(row, (SUBLANES, row.shape[1])), BF16)
    return jnp.concatenate([tile] * (N_KEYS // tile.shape[0]), axis=0)


def _dot_split(a, b):
    ah, al = _split_bf16(a)
    bh, bl = _split_bf16(b)
    a4 = jnp.concatenate([ah, al, ah, al], axis=1)
    b4 = jnp.concatenate([bh, bh, bl, bl], axis=0)
    return jnp.dot(a4, b4, preferred_element_type=F32)


def _mod_kernel(c_ref, w_ref, b_ref, o_ref):
    o_ref[0] = jnp.dot(_silu(c_ref[...]), w_ref[0], preferred_element_type=F32) + b_ref[0]


def _modulation(cvec, w_mod, b_mod):
    tn = 1536
    return pl.pallas_call(
        _mod_kernel,
        grid=(DEPTH, 6 * D_MODEL // tn),
        in_specs=[
            pl.BlockSpec((8, D_MODEL), lambda l, j: (0, 0)),
            pl.BlockSpec((1, D_MODEL, tn), lambda l, j: (l, 0, j)),
            pl.BlockSpec((1, 1, tn), lambda l, j: (l, 0, j)),
        ],
        out_specs=pl.BlockSpec((1, 8, tn), lambda l, j: (l, 0, j)),
        out_shape=jax.ShapeDtypeStruct((DEPTH, 8, 6 * D_MODEL), F32),
        compiler_params=_cparams("arbitrary", "arbitrary"),
        name="modulation",
    )(cvec, w_mod, b_mod.reshape(DEPTH, 1, 6 * D_MODEL))


def _norm_mod(x, nw, mod, k):
    ms = jnp.mean(x * x, axis=-1, keepdims=True)
    xn = x * lax.rsqrt(ms + EPS) * nw
    sh = mod[:, k * D_MODEL:(k + 1) * D_MODEL]
    sc = mod[:, (k + 1) * D_MODEL:(k + 2) * D_MODEL]
    return xn * (1 + sc) + sh


def _inproj_kernel(x_ref, modv_ref, nw_ref, w_ref, qkv_ref, z_ref, pin_ref, gd_ref, gp_ref, ba_ref, *, seq):
    g = _group_of_row(pl.program_id(0) * ROW_TILE, seq)
    mod = modv_ref[pl.ds(g, 1), :]
    h = _norm_mod(x_ref[...], nw_ref[...], mod, 0)
    y = jnp.dot(h.astype(BF16), w_ref[...], preferred_element_type=F32)
    o = 0
    for ref in (qkv_ref, z_ref, pin_ref, gd_ref, gp_ref, ba_ref):
        n = ref.shape[1]
        ref[...] = y[:, o:o + n]
        o += n


def _inproj(xall, modv, norm_w, w_in_r, seq):
    rows = xall.shape[0]
    widths = (3 * DN_WIDTH, DN_WIDTH, POOL_WIDTH, D_MODEL, D_MODEL, BA_PAD)
    return pl.pallas_call(
        functools.partial(_inproj_kernel, seq=seq),
        grid=(rows // ROW_TILE,),
        in_specs=[
            pl.BlockSpec((ROW_TILE, D_MODEL), lambda i: (i, 0)),
            pl.BlockSpec((8, 6 * D_MODEL), lambda i: (0, 0)),
            pl.BlockSpec((1, D_MODEL), lambda i: (0, 0)),
            pl.BlockSpec((D_MODEL, IN_COLS_R), lambda i: (0, 0)),
        ],
        out_specs=[pl.BlockSpec((ROW_TILE, n), lambda i: (i, 0)) for n in widths],
        out_shape=[jax.ShapeDtypeStruct((rows, n), F32) for n in widths],
        compiler_params=_cparams("arbitrary"),
        name="inproj",
    )(xall, modv, norm_w.reshape(1, D_MODEL), w_in_r)


def _dnconv_kernel(cur_ref, prev_ref, next_ref, cw_ref, ba_ref, alog_ref, dtb_ref,
                   q_ref, k_ref, v_ref, bg_ref, *, seq, ctx):
    row0 = pl.program_id(0) * ROW_TILE
    nlat = BATCH * seq
    is_start = (row0 == 0) | (row0 == seq) | (row0 == nlat) | (row0 == nlat + ctx)
    row1 = row0 + ROW_TILE
    is_end = (row1 == seq) | (row1 == nlat) | (row1 == nlat + ctx) | (row1 == nlat + BATCH * ctx)
    prev = jnp.where(is_start, 0.0, prev_ref[...])
    nxt = jnp.where(is_end, 0.0, next_ref[...])
    ext = jnp.concatenate([prev, cur_ref[...], nxt], axis=0)
    left = SHORT_CONV // 2
    cw = cw_ref[...]
    y = None
    for j in range(SHORT_CONV):
        o = HALO - left + j
        term = ext[o:o + ROW_TILE, :] * cw[j:j + 1, :]
        y = term if y is None else y + term
    y = _silu(y)
    for hh in range(DN_HEADS):
        lo, hi = hh * DN_HEAD_DIM, (hh + 1) * DN_HEAD_DIM
        qh = y[:, lo:hi]
        kh = y[:, DN_WIDTH + lo:DN_WIDTH + hi]
        q_ref[:, lo:hi] = qh * lax.rsqrt(jnp.sum(qh * qh, axis=-1, keepdims=True) + EPS) * (DN_HEAD_DIM ** -0.5)
        k_ref[:, lo:hi] = kh * lax.rsqrt(jnp.sum(kh * kh, axis=-1, keepdims=True) + EPS)
    v_ref[...] = y[:, 2 * DN_WIDTH:]
    ba = ba_ref[...]
    beta = jax.nn.sigmoid(ba)
    xs = ba + dtb_ref[...]
    softplus = jnp.maximum(xs, 0.0) + jnp.log(1.0 + jnp.exp(-jnp.abs(xs)))
    gdec = -jnp.exp(alog_ref[...]) * softplus
    col = lax.broadcasted_iota(jnp.int32, ba.shape, 1)
    bg_ref[...] = jnp.where(col < 2 * DN_HEADS, beta, jnp.where(col < 4 * DN_HEADS, gdec, 0.0))


def _dnconv(qkv, ba, conv_w, a_log, dt_bias, seq, ctx):
    rows = qkv.shape[0]
    nh = ROW_TILE // HALO
    last = rows // HALO - 1
    pad = jnp.zeros((2 * DN_HEADS,), F32)
    tail = jnp.zeros((BA_PAD - 4 * DN_HEADS,), F32)
    alog = jnp.concatenate([pad, a_log.reshape(-1), tail]).reshape(1, BA_PAD)
    dtb = jnp.concatenate([pad, dt_bias.reshape(-1), tail]).reshape(1, BA_PAD)
    w3 = 3 * DN_WIDTH
    return pl.pallas_call(
        functools.partial(_dnconv_kernel, seq=seq, ctx=ctx),
        grid=(rows // ROW_TILE,),
        in_specs=[
            pl.BlockSpec((ROW_TILE, w3), lambda i: (i, 0)),
            pl.BlockSpec((HALO, w3), lambda i: (jnp.maximum(i * nh - 1, 0), 0)),
            pl.BlockSpec((HALO, w3), lambda i: (jnp.minimum((i + 1) * nh, last), 0)),
            pl.BlockSpec((SHORT_CONV, w3), lambda i: (0, 0)),
            pl.BlockSpec((ROW_TILE, BA_PAD), lambda i: (i, 0)),
            pl.BlockSpec((1, BA_PAD), lambda i: (0, 0)),
            pl.BlockSpec((1, BA_PAD), lambda i: (0, 0)),
        ],
        out_specs=[pl.BlockSpec((ROW_TILE, DN_WIDTH), lambda i: (i, 0))] * 3
        + [pl.BlockSpec((ROW_TILE, BA_PAD), lambda i: (i, 0))],
        out_shape=[jax.ShapeDtypeStruct((rows, DN_WIDTH), F32)] * 3
        + [jax.ShapeDtypeStruct((rows, BA_PAD), F32)],
        compiler_params=_cparams("arbitrary"),
        name="dnconv",
    )(qkv, qkv, qkv, conv_w, ba, alog, dtb)


def _dnprep_kernel(q_ref, k_ref, v_ref, bg_ref, bgt_ref, wq_ref, u_ref, lk_ref, eg_ref):
    c = DN_CHUNK
    ri = lax.broadcasted_iota(jnp.int32, (c, c), 0)
    ci = lax.broadcasted_iota(jnp.int32, (c, c), 1)
    dirs = ((ri >= ci, ri > ci, ci >= ri, c - 1), (ri <= ci, ri < ci, ci <= ri, 0))
    chains = []
    for j in range(DN_PREP_CHUNKS):
        rows = slice(j * c, (j + 1) * c)
        bg = bg_ref[rows, :]
        bgh, bgl = _split_bf16(bg)
        bgth, bgtl = _split_bf16(bgt_ref[j])
        bg2 = jnp.concatenate([bgh, bgl], axis=0)
        bgt2 = jnp.concatenate([bgth, bgtl], axis=1)
        egs = []
        for d, (incl, strict, incl_t, last) in enumerate(dirs):
            m = incl.astype(BF16)
            mt = incl_t.astype(BF16)
            gc_all = jnp.dot(jnp.concatenate([m, m], axis=1), bg2, preferred_element_type=F32)
            gr_all = jnp.dot(bgt2, jnp.concatenate([mt, mt], axis=0), preferred_element_type=F32)
            for hh in range(DN_HEADS):
                lo, hi = hh * DN_HEAD_DIM, (hh + 1) * DN_HEAD_DIM
                ch = d * DN_HEADS + hh
                gcol = 2 * DN_HEADS + ch
                q = q_ref[rows, lo:hi]
                k = k_ref[rows, lo:hi]
                v = v_ref[rows, lo:hi]
                beta = bg[:, ch:ch + 1]
                gc = gc_all[:, gcol:gcol + 1]
                gr = gr_all[gcol:gcol + 1, :]
                glast = gc_all[last:last + 1, gcol:gcol + 1]
                decay = jnp.where(incl, jnp.exp(jnp.minimum(gc - gr, 0.0)), 0.0)
                kk = lax.dot_general(k, k, NT_DIMS, preferred_element_type=F32)
                qk = lax.dot_general(q, k, NT_DIMS, preferred_element_type=F32)
                egc = jnp.exp(gc)
                cols = slice(ch * DN_HEAD_DIM, (ch + 1) * DN_HEAD_DIM)
                wq_ref[0, j, c:2 * c, cols] = (q * egc).astype(BF16)
                lk_ref[0, j, ch, 0:c, :] = jnp.where(incl, qk * decay, 0.0).astype(BF16)
                lk_ref[0, j, ch, c:, :] = (k * jnp.exp(glast - gc)).T.astype(BF16)
                egs.append(jnp.broadcast_to(jnp.exp(glast), (1, LANES)))
                chains.append((j, cols, jnp.where(strict, -(beta * kk * decay), 0.0),
                               jnp.concatenate([k * (beta * egc), v * beta], axis=1)))
        eg_ref[0, j] = jnp.concatenate(egs, axis=0)
    ps = [ch[2] for ch in chains]
    ys = [ch[3] for ch in chains]
    ys = [y + _dot_split(p, y) for p, y in zip(ps, ys)]
    for _ in range(int(np.log2(c)) - 1):
        ps = [_dot_split(p, p) for p in ps]
        ys = [y + _dot_split(p, y) for p, y in zip(ps, ys)]
    for (j, cols, _, _), y in zip(chains, ys):
        wq_ref[0, j, 0:c, cols] = y[:, :DN_HEAD_DIM].astype(BF16)
        u_ref[0, j, :, cols] = y[:, DN_HEAD_DIM:]


def _dnprep(q, k, v, bg, seq, ctx):
    rows = q.shape[0]
    c = DN_CHUNK
    n = DN_PREP_CHUNKS
    nct, nlt = ctx // c, seq // c
    assert nct % n == 0 and nlt % n == 0
    nch = nct + nlt
    bgt = bg[:, :4 * DN_HEADS].reshape(rows // c, c, 4 * DN_HEADS).transpose(0, 2, 1)

    def seq_pos(i):
        i = i * n
        lat = i < BATCH * nlt
        j = i - BATCH * nlt
        return jnp.where(lat, i // nlt, j // nct), jnp.where(lat, nct + i % nlt, j % nct) // n

    def out_spec(*tail):
        return pl.BlockSpec((1, n) + tail, lambda i: seq_pos(i) + (0,) * len(tail))

    wide = pl.BlockSpec((n * c, DN_WIDTH), lambda i: (i, 0))
    width = DN_CHAINS * DN_HEAD_DIM
    return pl.pallas_call(
        _dnprep_kernel,
        grid=(rows // (n * c),),
        in_specs=[wide, wide, wide,
                  pl.BlockSpec((n * c, BA_PAD), lambda i: (i, 0)),
                  pl.BlockSpec((n, 4 * DN_HEADS, c), lambda i: (i, 0, 0))],
        out_specs=[out_spec(2 * c, width), out_spec(c, width), out_spec(DN_CHAINS, c + DN_HEAD_DIM, c),
                   out_spec(DN_CHAINS, LANES)],
        out_shape=[jax.ShapeDtypeStruct((BATCH, nch, 2 * c, width), BF16),
                   jax.ShapeDtypeStruct((BATCH, nch, c, width), F32),
                   jax.ShapeDtypeStruct((BATCH, nch, DN_CHAINS, c + DN_HEAD_DIM, c), BF16),
                   jax.ShapeDtypeStruct((BATCH, nch, DN_CHAINS, LANES), F32)],
        compiler_params=_cparams("arbitrary"),
        name="dnprep",
    )(q, k, v, bg, bgt)


def _dnscan_kernel(wqf_ref, uf_ref, lkf_ref, egf_ref, wqb_ref, ub_ref, lkb_ref, egb_ref, of_ref, ob_ref, s_ref):
    @pl.when(pl.program_id(0) == 0)
    def _():
        s_ref[...] = jnp.zeros_like(s_ref)

    c = DN_CHUNK
    dirs = ((wqf_ref, uf_ref, lkf_ref, egf_ref, of_ref), (wqb_ref, ub_ref, lkb_ref, egb_ref, ob_ref))
    chains = [(d, b, hh) for d in range(2) for b in range(BATCH) for hh in range(DN_HEADS)]
    sidx = lambda d, b, hh: (b * 2 + d) * DN_HEADS + hh
    cols = lambda hh: slice(hh * DN_HEAD_DIM, (hh + 1) * DN_HEAD_DIM)
    ss = [s_ref[sidx(*ch)] for ch in chains]
    r1 = [jnp.dot(dirs[d][0][b, 0, :, cols(hh)], s.astype(BF16), preferred_element_type=F32)
          for (d, b, hh), s in zip(chains, ss)]
    vn = [dirs[d][1][b, 0, :, cols(hh)] - r[:c] for (d, b, hh), r in zip(chains, r1)]
    r2 = [jnp.dot(dirs[d][2][b, 0, hh], v.astype(BF16), preferred_element_type=F32)
          for (d, b, hh), v in zip(chains, vn)]
    for (d, b, hh), s, a1, a2 in zip(chains, ss, r1, r2):
        dirs[d][4][b, 0, :, cols(hh)] = a1[c:] + a2[:c]
        row = d * DN_HEADS + hh
        s_ref[sidx(d, b, hh)] = s * dirs[d][3][b, 0, row:row + 1, :] + a2[c:]


def _dnscan(wq, u, lk, eg, seq, ctx):
    c = DN_CHUNK
    nct, nlt = ctx // c, seq // c
    nch = nct + nlt

    def bwd(s):
        return jnp.where(s < nct, nct - 1 - s, 2 * nct + nlt - 1 - s)

    def specs(pos, d):
        return [pl.BlockSpec((BATCH, 1, 2 * c, DN_WIDTH), lambda s: (0, pos(s), 0, d)),
                pl.BlockSpec((BATCH, 1, c, DN_WIDTH), lambda s: (0, pos(s), 0, d)),
                pl.BlockSpec((BATCH, 1, DN_HEADS, c + DN_HEAD_DIM, c), lambda s: (0, pos(s), d, 0, 0)),
                pl.BlockSpec((BATCH, 1, DN_CHAINS, LANES), lambda s: (0, pos(s), 0, 0))]

    fwd = lambda s: s
    return pl.pallas_call(
        _dnscan_kernel,
        grid=(nch,),
        in_specs=specs(fwd, 0) + specs(bwd, 1),
        out_specs=[pl.BlockSpec((BATCH, 1, c, DN_WIDTH), lambda s: (0, s, 0, 0)),
                   pl.BlockSpec((BATCH, 1, c, DN_WIDTH), lambda s: (0, bwd(s), 0, 0))],
        out_shape=[jax.ShapeDtypeStruct((BATCH, nch, c, DN_WIDTH), F32)] * 2,
        scratch_shapes=[pltpu.VMEM((BATCH * DN_CHAINS, DN_HEAD_DIM, DN_HEAD_DIM), F32)],
        compiler_params=_cparams("arbitrary"),
        name="dnscan",
    )(wq, u, lk, eg, wq, u, lk, eg)


def _pool_matrices(tile, seg):
    t = np.arange(tile)
    p = t % seg
    mats = []
    for win in POOL_WINDOWS:
        lo = np.clip(p - win // 2, 0, seg)
        hi = np.clip(p + win - win // 2, 0, seg)
        same = (t[:, None] // seg) == (t[None, :] // seg)
        inside = same & (p[None, :] >= lo[:, None]) & (p[None, :] < hi[:, None])
        mats.append(inside / (hi - lo)[:, None].astype(np.float64) - np.eye(tile))
    return np.stack(mats).astype(np.float32)


def _merge_kernel(x_ref, of_ref, ob_ref, z_ref, pin_ref, gd_ref, gp_ref, modv_ref, on_ref, pm_ref, pw_ref, ps_ref,
                  wud_ref, wup_ref, wo_ref, o_ref, *, seq):
    g = _group_of_row(pl.program_id(0) * ROW_TILE, seq)
    mod = modv_ref[pl.ds(g, 1), :]
    gate = mod[:, 2 * D_MODEL:3 * D_MODEL]
    o = (of_ref[0] + ob_ref[0]).reshape(ROW_TILE, DN_WIDTH)
    z = z_ref[...]
    onw = on_ref[...]
    ys = []
    for hh in range(DN_HEADS):
        lo, hi = hh * DN_HEAD_DIM, (hh + 1) * DN_HEAD_DIM
        oh = o[:, lo:hi]
        ys.append(oh * lax.rsqrt(jnp.mean(oh * oh, axis=-1, keepdims=True) + EPS) * onw * _silu(z[:, lo:hi]))
    y_dn = jnp.concatenate(ys, axis=1)
    pin = pin_ref[...]
    yp = []
    for gi in range(POOL_GROUPS):
        lo, hi = gi * POOL_GW, (gi + 1) * POOL_GW
        pooled = jnp.dot(pm_ref[0, gi], pin[:, lo:hi], precision=HIGHEST, preferred_element_type=F32)
        yp.append(jnp.dot(pooled, pw_ref[gi], preferred_element_type=F32))
    y_pool = jnp.concatenate(yp, axis=1) * ps_ref[...]
    m = (jax.nn.sigmoid(gd_ref[...]) * jnp.dot(y_dn.astype(BF16), wud_ref[...], preferred_element_type=F32)
         + jax.nn.sigmoid(gp_ref[...]) * jnp.dot(y_pool.astype(BF16), wup_ref[...], preferred_element_type=F32))
    out = jnp.dot(m.astype(BF16), wo_ref[...], preferred_element_type=F32)
    o_ref[...] = x_ref[...] + gate * out


def _merge(xall, o_f, o_b, z, pin, gd, gp, modv, onorm, pool_w, pool_scale, wud, wup, wo, seq, ctx, rows_out):
    assert ctx == ROW_TILE and seq % ROW_TILE == 0
    lat_tiles = BATCH * seq // ROW_TILE
    tiles_per_seq = seq // ROW_TILE
    cpt = ROW_TILE // DN_CHUNK
    pm = jnp.asarray(np.stack([_pool_matrices(ROW_TILE, GRID_W), _pool_matrices(ROW_TILE, ctx)]))
    row = lambda n: pl.BlockSpec((ROW_TILE, n), lambda i: (i, 0))
    full = lambda *s: pl.BlockSpec(s, lambda i: (0,) * len(s))

    def o_pos(i):
        lat = i < lat_tiles
        return (jnp.where(lat, i // tiles_per_seq, i - lat_tiles), jnp.where(lat, 1 + i % tiles_per_seq, 0), 0, 0)

    o_spec = pl.BlockSpec((1, cpt, DN_CHUNK, DN_WIDTH), o_pos)
    return pl.pallas_call(
        functools.partial(_merge_kernel, seq=seq),
        grid=(rows_out // ROW_TILE,),
        in_specs=[
            row(D_MODEL), o_spec, o_spec, row(DN_WIDTH), row(POOL_WIDTH), row(D_MODEL), row(D_MODEL),
            full(8, 6 * D_MODEL), full(1, DN_HEAD_DIM),
            pl.BlockSpec((1, POOL_GROUPS, ROW_TILE, ROW_TILE), lambda i: (jnp.where(i >= lat_tiles, 1, 0), 0, 0, 0)),
            full(POOL_GROUPS, POOL_GW, POOL_GW), full(1, POOL_WIDTH),
            full(DN_WIDTH, D_MODEL), full(POOL_WIDTH, D_MODEL), full(D_MODEL, D_MODEL),
        ],
        out_specs=row(D_MODEL),
        out_shape=jax.ShapeDtypeStruct((rows_out, D_MODEL), F32),
        compiler_params=_cparams("arbitrary"),
        name="merge",
    )(xall, o_f, o_b, z, pin, gd, gp, modv, onorm.reshape(1, DN_HEAD_DIM), pm, pool_w,
      pool_scale.reshape(1, POOL_WIDTH), wud, wup, wo)


def _col_max(x):
    return jnp.max(x, axis=0, keepdims=True)


def _col_min(x):
    return jnp.min(x, axis=0, keepdims=True)


def _oddeven_merge(lo, hi, r):
    step = r * 2
    if step < hi - lo:
        yield from _oddeven_merge(lo, hi, step)
        yield from _oddeven_merge(lo + r, hi, step)
        yield from [(i, i + r) for i in range(lo + r, hi - r, step)]
    else:
        yield (lo, lo + r)


def _oddeven_merge_sort(lo, hi):
    if hi - lo >= 1:
        mid = lo + (hi - lo) // 2
        yield from _oddeven_merge_sort(lo, mid)
        yield from _oddeven_merge_sort(mid + 1, hi)
        yield from _oddeven_merge(lo, hi, 1)


def _exchange(v, i, j):
    v[i], v[j] = jnp.maximum(v[i], v[j]), jnp.minimum(v[i], v[j])


def _sorted_top(blocks):
    k = PEER_TOPK
    v = list(blocks)
    for i, j in _oddeven_merge_sort(0, k - 1):
        _exchange(v, i, j)
    shift = SUBLANES // 2
    while shift >= 1:
        w = [pltpu.roll(x, shift, 0) for x in v]
        v = [jnp.maximum(v[j], w[k - 1 - j]) for j in range(k)]
        stride = k // 2
        while stride >= 1:
            for i in range(k):
                if i & stride == 0:
                    _exchange(v, i, i + stride)
            stride //= 2
        shift //= 2
    return v


def _sublane_total(x):
    shift = SUBLANES // 2
    while shift >= 1:
        x = x + pltpu.roll(x, shift, 0)
        shift //= 2
    return x


def _unambiguous(blocks, top):
    ok = top[0] > top[1]
    for a in range(1, PEER_TOPK - 1):
        ok = ok & (top[a] > top[a + 1])
    ge = None
    for blk in blocks:
        one = jnp.where(blk >= top[PEER_TOPK - 1], 1.0, 0.0)
        ge = one if ge is None else ge + one
    return ok & (_sublane_total(ge) == float(PEER_TOPK))


def _top_select(scores, iota):
    n = float(scores[0].shape[0])
    cur = list(scores)
    idxs = [[] for _ in cur]
    for _ in range(PEER_TOPK):
        for j in range(len(cur)):
            m = _col_max(cur[j])
            cand = jnp.where(cur[j] == m, iota, n)
            idx = _col_min(cand)
            cur[j] = jnp.where(cand == idx, NEG_INF, cur[j])
            idxs[j].append(idx)
    return [jnp.concatenate(i, axis=0) for i in idxs]


def _peer_route_kernel(x_ref, modv_ref, nw_ref, wq_ref, keys_ref, hn_ref, n1_ref, e1_ref, r2_ref, e2_ref,
                       q_scr, *, seq):
    g = _group_of_row(pl.program_id(0) * ROW_TILE, seq)
    mod = modv_ref[pl.ds(g, 1), :]
    hn = _norm_mod(x_ref[...], nw_ref[...], mod, 3).astype(BF16)
    hn_ref[...] = hn
    q = jnp.dot(hn, wq_ref[...], preferred_element_type=F32)
    nsub = ROW_TILE // LANES
    for j in range(2 * PEER_HEADS):
        for t in range(nsub):
            q_scr[j, t] = q[t * LANES:(t + 1) * LANES, j * PEER_HALF:(j + 1) * PEER_HALF]
    k = PEER_TOPK
    nblk = N_KEYS // SUBLANES
    iota_k = lax.broadcasted_iota(jnp.int32, (N_KEYS, LANES), 0).astype(F32)
    iota_a = lax.broadcasted_iota(jnp.int32, (k, LANES), 0).astype(F32)
    split = lambda s: [s[SUBLANES * j:SUBLANES * (j + 1), :] for j in range(nblk)]
    rep = lambda row: jnp.broadcast_to(row, (SUBLANES, LANES))

    def head(hh, carry):
        for t in range(nsub):
            lanes = slice(t * LANES, (t + 1) * LANES)
            s1 = lax.dot_general(keys_ref[2 * hh], q_scr[2 * hh, t], NT_DIMS, preferred_element_type=F32)
            s2 = lax.dot_general(keys_ref[2 * hh + 1], q_scr[2 * hh + 1, t], NT_DIMS, preferred_element_type=F32)
            b1, b2 = split(s1), split(s2)
            top1, top2 = _sorted_top(b1), _sorted_top(b2)
            c = jnp.concatenate([x[0:1, :] for x in top1], axis=0)
            d = jnp.concatenate([x[0:1, :] for x in top2], axis=0)

            cnt = jnp.zeros((k, LANES), F32)
            for _ in range(k):
                dn = jnp.full((k, LANES), NEG_INF, F32)
                for b in range(k):
                    dn = jnp.where(cnt == float(b), d[b:b + 1, :], dn)
                f = c + dn
                cand = jnp.where(f == _col_max(f), iota_a, float(k))
                cnt = cnt + (cand == _col_min(cand)).astype(F32)
            e1c = jnp.exp(c - c[0:1, :])
            e2d = jnp.exp(d - d[0:1, :])
            part = jnp.zeros((k, LANES), F32)
            for b in range(k):
                part = part + jnp.where(cnt > float(b), e2d[b:b + 1, :], 0.0)
            zsum = jnp.sum(e1c * part, axis=0, keepdims=True)

            def by_value():
                n1b = [jnp.zeros((SUBLANES, LANES), F32)] * nblk
                r2b = [jnp.full((SUBLANES, LANES), float(k), F32)] * nblk
                for a in range(k):
                    cnt_a = rep(cnt[a:a + 1, :])
                    n1b = [jnp.where(blk == top1[a], cnt_a, acc) for blk, acc in zip(b1, n1b)]
                    r2b = [jnp.where(blk == top2[a], float(a), acc) for blk, acc in zip(b2, r2b)]
                return jnp.concatenate(n1b, axis=0), jnp.concatenate(r2b, axis=0)

            def by_index():
                idx1, idx2 = _top_select([s1, s2], iota_k)
                n1 = jnp.zeros((N_KEYS, LANES), F32)
                r2 = jnp.full((N_KEYS, LANES), float(k), F32)
                for a in range(k):
                    n1 = jnp.where(iota_k == idx1[a:a + 1, :], cnt[a:a + 1, :], n1)
                    r2 = jnp.where(iota_k == idx2[a:a + 1, :], float(a), r2)
                return n1, r2

            clean = jnp.all(_unambiguous(b1, top1) & _unambiguous(b2, top2))
            n1, r2 = lax.cond(clean, by_value, by_index)
            n1_ref[hh, 0, :, lanes] = _bf16_pair_word(n1)
            e1_ref[hh, 0, :, lanes] = _bf16_pair_word(0.5 * jnp.exp(s1 - c[0:1, :]) / zsum)
            r2_ref[hh, 0, :, lanes] = r2.astype(BF16)
            e2_ref[hh, 0, :, lanes] = jnp.exp(s2 - d[0:1, :]).astype(BF16)
        return carry

    lax.fori_loop(0, PEER_HEADS, head, 0)


def _peer_route(xall, modv, norm_w, wq, keys, seq, rows_out):
    row = lambda n: pl.BlockSpec((ROW_TILE, n), lambda i: (i, 0))
    full = lambda *s: pl.BlockSpec(s, lambda i: (0,) * len(s))
    tab = pl.BlockSpec((PEER_HEADS, 1, N_KEYS, ROW_TILE), lambda i: (0, i, 0, 0))
    tab_shape = lambda dt: jax.ShapeDtypeStruct((PEER_HEADS, rows_out // ROW_TILE, N_KEYS, ROW_TILE), dt)
    return pl.pallas_call(
        functools.partial(_peer_route_kernel, seq=seq),
        grid=(rows_out // ROW_TILE,),
        in_specs=[row(D_MODEL), full(8, 6 * D_MODEL), full(1, D_MODEL), full(D_MODEL, PEER_HEADS * PEER_QDIM),
                  full(2 * PEER_HEADS, N_KEYS, PEER_HALF)],
        out_specs=[row(D_MODEL), tab, tab, tab, tab],
        out_shape=[jax.ShapeDtypeStruct((rows_out, D_MODEL), BF16), tab_shape(jnp.uint32), tab_shape(jnp.uint32),
                   tab_shape(BF16), tab_shape(BF16)],
        scratch_shapes=[pltpu.VMEM((2 * PEER_HEADS, ROW_TILE // LANES, LANES, PEER_HALF), F32)],
        compiler_params=_cparams("arbitrary"),
        name="peer_route",
    )(xall, modv, norm_w.reshape(1, D_MODEL), wq, keys.reshape(2 * PEER_HEADS, N_KEYS, PEER_HALF))


PEER_PIPE_LAG = 2
PEER_MXU_BLOCK = 256


def _peer_tile(n, lag, n_tiles, n_exp):
    m = jnp.clip(n - lag, 0, n_tiles - 1)
    return m // n_exp, m % n_exp


def _peer_dense_kernel(x_ref, modv_ref, hn_ref, u_ref, vt_ref, n1_ref, e1_ref, r2_ref, e2_ref, fn_ref, o_ref,
                       acc_ref, a0_ref, a1_ref, act0_ref, act1_ref, hn_scr, *, seq, final, n_tiles, n_exp):
    n = pl.program_id(0)
    t_c, e_c = _peer_tile(n, PEER_PIPE_LAG, n_tiles, n_exp)

    @pl.when(n == 0)
    def _():
        for ref in (acc_ref, a0_ref, a1_ref, act0_ref, act1_ref):
            ref[...] = jnp.zeros_like(ref)

    @pl.when(_peer_tile(n, 0, n_tiles, n_exp)[1] == 0)
    def _():
        hn_scr[...] = hn_ref[...]

    def step(a_next, a_cur, act_next, act_cur):
        zero = jnp.zeros((), BF16)
        blk = PEER_MXU_BLOCK

        def stage_a(rb, cb):
            rows, cols = slice(rb * blk, (rb + 1) * blk), slice(cb * blk, (cb + 1) * blk)
            a_next[rows, cols] = lax.dot_general(u_ref[rows, :], hn_scr[cols, :], NT_DIMS,
                                                 preferred_element_type=F32)

        def stage_b(t, i):
            lanes = slice(t * ROW_TILE, (t + 1) * ROW_TILE)
            wsum = None
            for hh in range(PEER_HEADS):
                keep = r2_ref[hh, t] < _pair_word_rows(n1_ref[hh, t, i:i + 1, :])
                term = jnp.where(keep, e2_ref[hh, t], zero) * _pair_word_rows(e1_ref[hh, t, i:i + 1, :])
                wsum = term if wsum is None else wsum + term
            rows = slice(i * N_KEYS, (i + 1) * N_KEYS)
            a = a_cur[rows, lanes]
            gelu = a * (1.0 + lax.erf(a * (2.0 ** -0.5)))
            act_next[rows, lanes] = gelu.astype(BF16) * wsum

        def stage_c(rb, cb):
            rows, cols = slice(rb * blk, (rb + 1) * blk), slice(cb * blk, (cb + 1) * blk)
            prev = jnp.where(e_c == 0, 0.0, acc_ref[rows, cols])
            acc_ref[rows, cols] = prev + jnp.dot(vt_ref[rows, :], act_cur[:, cols], preferred_element_type=F32)

        a_pieces = [(rb, cb) for cb in range(PEER_TOK_TILE // blk) for rb in range(PEER_EXP_TILE // blk)]
        c_pieces = [(rb, cb) for cb in range(PEER_TOK_TILE // blk) for rb in range(D_MODEL // blk)]
        b_pieces = [(t, i) for t in range(PEER_TOK_TILE // ROW_TILE) for i in range(PEER_EXP_TILE // N_KEYS)]
        per_mxu = len(b_pieces) // (len(a_pieces) + len(c_pieces))
        mxu_pieces = [x for pair in zip([(stage_a, p) for p in a_pieces], [(stage_c, p) for p in c_pieces])
                      for x in pair]
        for k, (fn, piece) in enumerate(mxu_pieces):
            fn(*piece)
            for t, i in b_pieces[k * per_mxu:(k + 1) * per_mxu]:
                stage_b(t, i)

    @pl.when(n % 2 == 0)
    def _():
        step(a0_ref, a1_ref, act1_ref, act0_ref)

    @pl.when(n % 2 == 1)
    def _():
        step(a1_ref, a0_ref, act0_ref, act1_ref)

    @pl.when((e_c == n_exp - 1) & (n >= PEER_PIPE_LAG))
    def _():
        g = _group_of_row(t_c * PEER_TOK_TILE, seq)
        mod = modv_ref[pl.ds(g, 1), :]
        y = x_ref[...] + mod[:, 5 * D_MODEL:6 * D_MODEL] * acc_ref[...].T
        if final:
            y = y * lax.rsqrt(jnp.mean(y * y, axis=-1, keepdims=True) + EPS) * fn_ref[...]
        o_ref[...] = y


def _peer_dense(xall, modv, hn, u_bf, vt_bf, n1, e1, r2, e2, final_norm, seq, rows_out, final):
    tt, et = PEER_TOK_TILE, PEER_EXP_TILE
    n_first = et // N_KEYS
    n_exp = N_EXPERTS // et
    n_tiles = (rows_out // tt) * n_exp
    tile = lambda lag: (lambda n: _peer_tile(n, lag, n_tiles, n_exp))
    ta, tb, tc = tile(0), tile(1), tile(PEER_PIPE_LAG)
    full = lambda *s: pl.BlockSpec(s, lambda n: (0,) * len(s))
    per_first = pl.BlockSpec((PEER_HEADS, tt // ROW_TILE, n_first, ROW_TILE), lambda n: (0, tb(n)[0], tb(n)[1], 0))
    per_second = pl.BlockSpec((PEER_HEADS, tt // ROW_TILE, N_KEYS, ROW_TILE), lambda n: (0, tb(n)[0], 0, 0))
    return pl.pallas_call(
        functools.partial(_peer_dense_kernel, seq=seq, final=final, n_tiles=n_tiles, n_exp=n_exp),
        grid=(n_tiles + PEER_PIPE_LAG,),
        in_specs=[pl.BlockSpec((tt, D_MODEL), lambda n: (tc(n)[0], 0)),
                  full(8, 6 * D_MODEL),
                  pl.BlockSpec((tt, D_MODEL), lambda n: (ta(n)[0], 0)),
                  pl.BlockSpec((et, D_MODEL), lambda n: (ta(n)[1], 0)),
                  pl.BlockSpec((D_MODEL, et), lambda n: (0, tc(n)[1])),
                  per_first, per_first, per_second, per_second, full(1, D_MODEL)],
        out_specs=pl.BlockSpec((tt, D_MODEL), lambda n: (tc(n)[0], 0)),
        out_shape=jax.ShapeDtypeStruct((rows_out, D_MODEL), F32),
        scratch_shapes=[pltpu.VMEM((D_MODEL, tt), F32), pltpu.VMEM((et, tt), F32), pltpu.VMEM((et, tt), F32),
                        pltpu.VMEM((et, tt), BF16), pltpu.VMEM((et, tt), BF16), pltpu.VMEM((tt, D_MODEL), BF16)],
        compiler_params=_cparams("arbitrary"),
        name="peer_dense",
    )(xall, modv, hn, u_bf, vt_bf, n1, e1, r2, e2, final_norm.reshape(1, D_MODEL))


def _reorder_in_weight(w):
    s = np.cumsum((0, 3 * DN_WIDTH, DN_WIDTH, POOL_WIDTH, 2 * DN_HEADS, 2 * DN_HEADS, D_MODEL, D_MODEL))
    qkv, z, pin, b, a, gd, gp = (w[:, s[i]:s[i + 1]] for i in range(7))
    pad = jnp.zeros((w.shape[0], BA_PAD - 4 * DN_HEADS), w.dtype)
    return jnp.concatenate([qkv, z, pin, gd, gp, b, a, pad], axis=1).astype(BF16)


def _forward(x, c, ctx, c_ctx, w_mod, b_mod, norm_mix, w_in, conv_w, a_log, dt_bias, dn_out_norm, pool_w, pool_scale,
             w_up_dn, w_up_pool, w_out, norm_ffn, peer_wq, peer_keys, peer_u, peer_v, final_norm):
    seq, nctx = x.shape[1], ctx.shape[1]
    nlat = BATCH * seq
    xall = jnp.concatenate([x.reshape(nlat, D_MODEL), ctx.reshape(BATCH * nctx, D_MODEL)], axis=0)
    rows = xall.shape[0]
    cvec = jnp.concatenate([c, c_ctx[None, :], jnp.zeros((8 - BATCH - 1, D_MODEL), F32)], axis=0)
    modv_all = _modulation(cvec, w_mod, b_mod)
    for i in range(DEPTH):
        last = i == DEPTH - 1
        rows_out = nlat if last else rows
        modv = modv_all[i]
        qkv, z, pin, gd, gp, ba = _inproj(xall, modv, norm_mix[i], _reorder_in_weight(w_in[i]), seq)
        q, k, v, bg = _dnconv(qkv, ba, conv_w[i], a_log[i], dt_bias[i], seq, nctx)
        o_f, o_b = _dnscan(*_dnprep(q, k, v, bg, seq, nctx), seq, nctx)
        xall = _merge(xall, o_f, o_b, z, pin, gd, gp, modv, dn_out_norm[i], pool_w[i], pool_scale[i],
                      w_up_dn[i].astype(BF16), w_up_pool[i].astype(BF16), w_out[i].astype(BF16), seq, nctx, rows_out)
        hn, n1, e1, r2, e2 = _peer_route(xall, modv, norm_ffn[i], peer_wq[i].astype(BF16), peer_keys[i], seq, rows_out)
        xall = _peer_dense(xall, modv, hn, peer_u[i].astype(BF16), peer_v[i].T.astype(BF16), n1, e1, r2, e2,
                           final_norm, seq, rows_out, last)
    return xall.reshape(BATCH, seq, D_MODEL)


def kernel(x, c, ctx, c_ctx, w_mod, b_mod, norm_mix, w_in, conv_w, a_log, dt_bias, dn_out_norm, pool_w, pool_scale, w_up_dn, w_up_pool, w_out, norm_ffn, peer_wq, peer_keys, peer_u, peer_v, final_norm):
    return _forward(x, c, ctx, c_ctx, w_mod, b_mod, norm_mix, w_in, conv_w, a_log, dt_bias, dn_out_norm, pool_w,
                    pool_scale, w_up_dn, w_up_pool, w_out, norm_ffn, peer_wq, peer_keys, peer_u, peer_v, final_norm)
```

```python
import functools

import numpy as np
import jax
import jax.numpy as jnp
from jax import lax
from jax.experimental import pallas as pl
from jax.experimental.pallas import tpu as pltpu

D_MODEL = 1024
BATCH = 2
DEPTH = 2
GRID_W = 64
EPS = 1e-6

DN_HEADS = 4
DN_HEAD_DIM = 128
DN_WIDTH = DN_HEADS * DN_HEAD_DIM
SHORT_CONV = 4
DN_CHUNK = 64
DN_CHAINS = 2 * DN_HEADS
DN_PREP_CHUNKS = 2

POOL_WINDOWS = (2, 4, 8, 16)
POOL_GROUPS = 4
POOL_WIDTH = D_MODEL // 2
POOL_GW = POOL_WIDTH // POOL_GROUPS

PEER_HEADS = 8
N_KEYS = 128
N_EXPERTS = N_KEYS * N_KEYS
PEER_TOPK = 16
PEER_QDIM = 256
PEER_HALF = PEER_QDIM // 2

BA_PAD = 128
IN_COLS_R = 3 * DN_WIDTH + DN_WIDTH + POOL_WIDTH + 2 * D_MODEL + BA_PAD

LANES = 128
SUBLANES = 8
ROW_TILE = 256
PEER_TOK_TILE = 512
PEER_EXP_TILE = 1024
PEER_SLAB = 256
HALO = 8
VMEM_LIMIT = 56 * 1024 * 1024

F32 = jnp.float32
BF16 = jnp.bfloat16
HIGHEST = lax.Precision.HIGHEST
NEG_INF = float("-inf")
NT_DIMS = (((1,), (1,)), ((), ()))


def _cparams(*sem):
    return pltpu.CompilerParams(dimension_semantics=sem, vmem_limit_bytes=VMEM_LIMIT)


def _group_of_row(row0, seq):
    return jnp.where(row0 < seq, 0, jnp.where(row0 < 2 * seq, 1, 2))


def _silu(x):
    return x * jax.nn.sigmoid(x)


def _split_bf16(a):
    hi = a.astype(BF16)
    lo = (a - hi.astype(F32)).astype(BF16)
    return hi, lo


def _bf16_pair_word(x):
    hi = lax.bitcast_convert_type(x.astype(BF16).astype(F32), jnp.uint32)
    return hi | (hi >> 16)


def _pair_word_rows(row):
    tile = pltpu.bitcast(jnp.broadcast_to(row, (SUBLANES, row.shape[1])), BF16)
    return jnp.concatenate([tile] * (N_KEYS // tile.shape[0]), axis=0)


def _dot_split(a, b):
    ah, al = _split_bf16(a)
    bh, bl = _split_bf16(b)
    a4 = jnp.concatenate([ah, al, ah, al], axis=1)
    b4 = jnp.concatenate([bh, bh, bl, bl], axis=0)
    return jnp.dot(a4, b4, preferred_element_type=F32)


def _mod_kernel(c_ref, w_ref, b_ref, o_ref):
    o_ref[0] = jnp.dot(_silu(c_ref[...]), w_ref[0], preferred_element_type=F32) + b_ref[0]


def _modulation(cvec, w_mod, b_mod):
    tn = 1536
    return pl.pallas_call(
        _mod_kernel,
        grid=(DEPTH, 6 * D_MODEL // tn),
        in_specs=[
            pl.BlockSpec((8, D_MODEL), lambda l, j: (0, 0)),
            pl.BlockSpec((1, D_MODEL, tn), lambda l, j: (l, 0, j)),
            pl.BlockSpec((1, 1, tn), lambda l, j: (l, 0, j)),
        ],
        out_specs=pl.BlockSpec((1, 8, tn), lambda l, j: (l, 0, j)),
        out_shape=jax.ShapeDtypeStruct((DEPTH, 8, 6 * D_MODEL), F32),
        compiler_params=_cparams("arbitrary", "arbitrary"),
        name="modulation",
    )(cvec, w_mod, b_mod.reshape(DEPTH, 1, 6 * D_MODEL))


def _norm_mod(x, nw, mod, k):
    ms = jnp.mean(x * x, axis=-1, keepdims=True)
    xn = x * lax.rsqrt(ms + EPS) * nw
    sh = mod[:, k * D_MODEL:(k + 1) * D_MODEL]
    sc = mod[:, (k + 1) * D_MODEL:(k + 2) * D_MODEL]
    return xn * (1 + sc) + sh


def _inproj_kernel(x_ref, modv_ref, nw_ref, w_ref, qkv_ref, z_ref, pin_ref, gd_ref, gp_ref, ba_ref, *, seq):
    g = _group_of_row(pl.program_id(0) * ROW_TILE, seq)
    mod = modv_ref[pl.ds(g, 1), :]
    h = _norm_mod(x_ref[...], nw_ref[...], mod, 0)
    y = jnp.dot(h.astype(BF16), w_ref[...], preferred_element_type=F32)
    o = 0
    for ref in (qkv_ref, z_ref, pin_ref, gd_ref, gp_ref, ba_ref):
        n = ref.shape[1]
        ref[...] = y[:, o:o + n]
        o += n


def _inproj(xall, modv, norm_w, w_in_r, seq):
    rows = xall.shape[0]
    widths = (3 * DN_WIDTH, DN_WIDTH, POOL_WIDTH, D_MODEL, D_MODEL, BA_PAD)
    return pl.pallas_call(
        functools.partial(_inproj_kernel, seq=seq),
        grid=(rows // ROW_TILE,),
        in_specs=[
            pl.BlockSpec((ROW_TILE, D_MODEL), lambda i: (i, 0)),
            pl.BlockSpec((8, 6 * D_MODEL), lambda i: (0, 0)),
            pl.BlockSpec((1, D_MODEL), lambda i: (0, 0)),
            pl.BlockSpec((D_MODEL, IN_COLS_R), lambda i: (0, 0)),
        ],
        out_specs=[pl.BlockSpec((ROW_TILE, n), lambda i: (i, 0)) for n in widths],
        out_shape=[jax.ShapeDtypeStruct((rows, n), F32) for n in widths],
        compiler_params=_cparams("arbitrary"),
        name="inproj",
    )(xall, modv, norm_w.reshape(1, D_MODEL), w_in_r)


def _dnconv_kernel(cur_ref, prev_ref, next_ref, cw_ref, ba_ref, alog_ref, dtb_ref,
                   q_ref, k_ref, v_ref, bg_ref, *, seq, ctx):
    row0 = pl.program_id(0) * ROW_TILE
    nlat = BATCH * seq
    is_start = (row0 == 0) | (row0 == seq) | (row0 == nlat) | (row0 == nlat + ctx)
    row1 = row0 + ROW_TILE
    is_end = (row1 == seq) | (row1 == nlat) | (row1 == nlat + ctx) | (row1 == nlat + BATCH * ctx)
    prev = jnp.where(is_start, 0.0, prev_ref[...])
    nxt = jnp.where(is_end, 0.0, next_ref[...])
    ext = jnp.concatenate([prev, cur_ref[...], nxt], axis=0)
    left = SHORT_CONV // 2
    cw = cw_ref[...]
    y = None
    for j in range(SHORT_CONV):
        o = HALO - left + j
        term = ext[o:o + ROW_TILE, :] * cw[j:j + 1, :]
        y = term if y is None else y + term
    y = _silu(y)
    for hh in range(DN_HEADS):
        lo, hi = hh * DN_HEAD_DIM, (hh + 1) * DN_HEAD_DIM
        qh = y[:, lo:hi]
        kh = y[:, DN_WIDTH + lo:DN_WIDTH + hi]
        q_ref[:, lo:hi] = qh * lax.rsqrt(jnp.sum(qh * qh, axis=-1, keepdims=True) + EPS) * (DN_HEAD_DIM ** -0.5)
        k_ref[:, lo:hi] = kh * lax.rsqrt(jnp.sum(kh * kh, axis=-1, keepdims=True) + EPS)
    v_ref[...] = y[:, 2 * DN_WIDTH:]
    ba = ba_ref[...]
    beta = jax.nn.sigmoid(ba)
    xs = ba + dtb_ref[...]
    softplus = jnp.maximum(xs, 0.0) + jnp.log(1.0 + jnp.exp(-jnp.abs(xs)))
    gdec = -jnp.exp(alog_ref[...]) * softplus
    col = lax.broadcasted_iota(jnp.int32, ba.shape, 1)
    bg_ref[...] = jnp.where(col < 2 * DN_HEADS, beta, jnp.where(col < 4 * DN_HEADS, gdec, 0.0))


def _dnconv(qkv, ba, conv_w, a_log, dt_bias, seq, ctx):
    rows = qkv.shape[0]
    nh = ROW_TILE // HALO
    last = rows // HALO - 1
    pad = jnp.zeros((2 * DN_HEADS,), F32)
    tail = jnp.zeros((BA_PAD - 4 * DN_HEADS,), F32)
    alog = jnp.concatenate([pad, a_log.reshape(-1), tail]).reshape(1, BA_PAD)
    dtb = jnp.concatenate([pad, dt_bias.reshape(-1), tail]).reshape(1, BA_PAD)
    w3 = 3 * DN_WIDTH
    return pl.pallas_call(
        functools.partial(_dnconv_kernel, seq=seq, ctx=ctx),
        grid=(rows // ROW_TILE,),
        in_specs=[
            pl.BlockSpec((ROW_TILE, w3), lambda i: (i, 0)),
            pl.BlockSpec((HALO, w3), lambda i: (jnp.maximum(i * nh - 1, 0), 0)),
            pl.BlockSpec((HALO, w3), lambda i: (jnp.minimum((i + 1) * nh, last), 0)),
            pl.BlockSpec((SHORT_CONV, w3), lambda i: (0, 0)),
            pl.BlockSpec((ROW_TILE, BA_PAD), lambda i: (i, 0)),
            pl.BlockSpec((1, BA_PAD), lambda i: (0, 0)),
            pl.BlockSpec((1, BA_PAD), lambda i: (0, 0)),
        ],
        out_specs=[pl.BlockSpec((ROW_TILE, DN_WIDTH), lambda i: (i, 0))] * 3
        + [pl.BlockSpec((ROW_TILE, BA_PAD), lambda i: (i, 0))],
        out_shape=[jax.ShapeDtypeStruct((rows, DN_WIDTH), F32)] * 3
        + [jax.ShapeDtypeStruct((rows, BA_PAD), F32)],
        compiler_params=_cparams("arbitrary"),
        name="dnconv",
    )(qkv, qkv, qkv, conv_w, ba, alog, dtb)


def _dnprep_kernel(q_ref, k_ref, v_ref, bg_ref, bgt_ref, wq_ref, u_ref, lk_ref, eg_ref):
    c = DN_CHUNK
    ri = lax.broadcasted_iota(jnp.int32, (c, c), 0)
    ci = lax.broadcasted_iota(jnp.int32, (c, c), 1)
    dirs = ((ri >= ci, ri > ci, ci >= ri, c - 1), (ri <= ci, ri < ci, ci <= ri, 0))
    chains = []
    for j in range(DN_PREP_CHUNKS):
        rows = slice(j * c, (j + 1) * c)
        bg = bg_ref[rows, :]
        bgh, bgl = _split_bf16(bg)
        bgth, bgtl = _split_bf16(bgt_ref[j])
        bg2 = jnp.concatenate([bgh, bgl], axis=0)
        bgt2 = jnp.concatenate([bgth, bgtl], axis=1)
        egs = []
        for d, (incl, strict, incl_t, last) in enumerate(dirs):
            m = incl.astype(BF16)
            mt = incl_t.astype(BF16)
            gc_all = jnp.dot(jnp.concatenate([m, m], axis=1), bg2, preferred_element_type=F32)
            gr_all = jnp.dot(bgt2, jnp.concatenate([mt, mt], axis=0), preferred_element_type=F32)
            for hh in range(DN_HEADS):
                lo, hi = hh * DN_HEAD_DIM, (hh + 1) * DN_HEAD_DIM
                ch = d * DN_HEADS + hh
                gcol = 2 * DN_HEADS + ch
                q = q_ref[rows, lo:hi]
                k = k_ref[rows, lo:hi]
                v = v_ref[rows, lo:hi]
                beta = bg[:, ch:ch + 1]
                gc = gc_all[:, gcol:gcol + 1]
                gr = gr_all[gcol:gcol + 1, :]
                glast = gc_all[last:last + 1, gcol:gcol + 1]
                decay = jnp.where(incl, jnp.exp(jnp.minimum(gc - gr, 0.0)), 0.0)
                kk = lax.dot_general(k, k, NT_DIMS, preferred_element_type=F32)
                qk = lax.dot_general(q, k, NT_DIMS, preferred_element_type=F32)
                egc = jnp.exp(gc)
                cols = slice(ch * DN_HEAD_DIM, (ch + 1) * DN_HEAD_DIM)
                wq_ref[0, j, c:2 * c, cols] = (q * egc).astype(BF16)
                lk_ref[0, j, ch, 0:c, :] = jnp.where(incl, qk * decay, 0.0).astype(BF16)
                lk_ref[0, j, ch, c:, :] = (k * jnp.exp(glast - gc)).T.astype(BF16)
                egs.append(jnp.broadcast_to(jnp.exp(glast), (1, LANES)))
                chains.append((j, cols, jnp.where(strict, -(beta * kk * decay), 0.0),
                               jnp.concatenate([k * (beta * egc), v * beta], axis=1)))
        eg_ref[0, j] = jnp.concatenate(egs, axis=0)
    ps = [ch[2] for ch in chains]
    ys = [ch[3] for ch in chains]
    ys = [y + _dot_split(p, y) for p, y in zip(ps, ys)]
    for _ in range(int(np.log2(c)) - 1):
        ps = [_dot_split(p, p) for p in ps]
        ys = [y + _dot_split(p, y) for p, y in zip(ps, ys)]
    for (j, cols, _, _), y in zip(chains, ys):
        wq_ref[0, j, 0:c, cols] = y[:, :DN_HEAD_DIM].astype(BF16)
        u_ref[0, j, :, cols] = y[:, DN_HEAD_DIM:]


def _dnprep(q, k, v, bg, seq, ctx):
    rows = q.shape[0]
    c = DN_CHUNK
    n = DN_PREP_CHUNKS
    nct, nlt = ctx // c, seq // c
    assert nct % n == 0 and nlt % n == 0
    nch = nct + nlt
    bgt = bg[:, :4 * DN_HEADS].reshape(rows // c, c, 4 * DN_HEADS).transpose(0, 2, 1)

    def seq_pos(i):
        i = i * n
        lat = i < BATCH * nlt
        j = i - BATCH * nlt
        return jnp.where(lat, i // nlt, j // nct), jnp.where(lat, nct + i % nlt, j % nct) // n

    def out_spec(*tail):
        return pl.BlockSpec((1, n) + tail, lambda i: seq_pos(i) + (0,) * len(tail))

    wide = pl.BlockSpec((n * c, DN_WIDTH), lambda i: (i, 0))
    width = DN_CHAINS * DN_HEAD_DIM
    return pl.pallas_call(
        _dnprep_kernel,
        grid=(rows // (n * c),),
        in_specs=[wide, wide, wide,
                  pl.BlockSpec((n * c, BA_PAD), lambda i: (i, 0)),
                  pl.BlockSpec((n, 4 * DN_HEADS, c), lambda i: (i, 0, 0))],
        out_specs=[out_spec(2 * c, width), out_spec(c, width), out_spec(DN_CHAINS, c + DN_HEAD_DIM, c),
                   out_spec(DN_CHAINS, LANES)],
        out_shape=[jax.ShapeDtypeStruct((BATCH, nch, 2 * c, width), BF16),
                   jax.ShapeDtypeStruct((BATCH, nch, c, width), F32),
                   jax.ShapeDtypeStruct((BATCH, nch, DN_CHAINS, c + DN_HEAD_DIM, c), BF16),
                   jax.ShapeDtypeStruct((BATCH, nch, DN_CHAINS, LANES), F32)],
        compiler_params=_cparams("arbitrary"),
        name="dnprep",
    )(q, k, v, bg, bgt)


def _dnscan_kernel(wqf_ref, uf_ref, lkf_ref, egf_ref, wqb_ref, ub_ref, lkb_ref, egb_ref, of_ref, ob_ref, s_ref):
    @pl.when(pl.program_id(0) == 0)
    def _():
        s_ref[...] = jnp.zeros_like(s_ref)

    c = DN_CHUNK
    dirs = ((wqf_ref, uf_ref, lkf_ref, egf_ref, of_ref), (wqb_ref, ub_ref, lkb_ref, egb_ref, ob_ref))
    chains = [(d, b, hh) for d in range(2) for b in range(BATCH) for hh in range(DN_HEADS)]
    sidx = lambda d, b, hh: (b * 2 + d) * DN_HEADS + hh
    cols = lambda hh: slice(hh * DN_HEAD_DIM, (hh + 1) * DN_HEAD_DIM)
    ss = [s_ref[sidx(*ch)] for ch in chains]
    r1 = [jnp.dot(dirs[d][0][b, 0, :, cols(hh)], s.astype(BF16), preferred_element_type=F32)
          for (d, b, hh), s in zip(chains, ss)]
    vn = [dirs[d][1][b, 0, :, cols(hh)] - r[:c] for (d, b, hh), r in zip(chains, r1)]
    r2 = [jnp.dot(dirs[d][2][b, 0, hh], v.astype(BF16), preferred_element_type=F32)
          for (d, b, hh), v in zip(chains, vn)]
    for (d, b, hh), s, a1, a2 in zip(chains, ss, r1, r2):
        dirs[d][4][b, 0, :, cols(hh)] = a1[c:] + a2[:c]
        row = d * DN_HEADS + hh
        s_ref[sidx(d, b, hh)] = s * dirs[d][3][b, 0, row:row + 1, :] + a2[c:]


def _dnscan(wq, u, lk, eg, seq, ctx):
    c = DN_CHUNK
    nct, nlt = ctx // c, seq // c
    nch = nct + nlt

    def bwd(s):
        return jnp.where(s < nct, nct - 1 - s, 2 * nct + nlt - 1 - s)

    def specs(pos, d):
        return [pl.BlockSpec((BATCH, 1, 2 * c, DN_WIDTH), lambda s: (0, pos(s), 0, d)),
                pl.BlockSpec((BATCH, 1, c, DN_WIDTH), lambda s: (0, pos(s), 0, d)),
                pl.BlockSpec((BATCH, 1, DN_HEADS, c + DN_HEAD_DIM, c), lambda s: (0, pos(s), d, 0, 0)),
                pl.BlockSpec((BATCH, 1, DN_CHAINS, LANES), lambda s: (0, pos(s), 0, 0))]

    fwd = lambda s: s
    return pl.pallas_call(
        _dnscan_kernel,
        grid=(nch,),
        in_specs=specs(fwd, 0) + specs(bwd, 1),
        out_specs=[pl.BlockSpec((BATCH, 1, c, DN_WIDTH), lambda s: (0, s, 0, 0)),
                   pl.BlockSpec((BATCH, 1, c, DN_WIDTH), lambda s: (0, bwd(s), 0, 0))],
        out_shape=[jax.ShapeDtypeStruct((BATCH, nch, c, DN_WIDTH), F32)] * 2,
        scratch_shapes=[pltpu.VMEM((BATCH * DN_CHAINS, DN_HEAD_DIM, DN_HEAD_DIM), F32)],
        compiler_params=_cparams("arbitrary"),
        name="dnscan",
    )(wq, u, lk, eg, wq, u, lk, eg)


def _pool_matrices(tile, seg):
    t = np.arange(tile)
    p = t % seg
    mats = []
    for win in POOL_WINDOWS:
        lo = np.clip(p - win // 2, 0, seg)
        hi = np.clip(p + win - win // 2, 0, seg)
        same = (t[:, None] // seg) == (t[None, :] // seg)
        inside = same & (p[None, :] >= lo[:, None]) & (p[None, :] < hi[:, None])
        mats.append(inside / (hi - lo)[:, None].astype(np.float64) - np.eye(tile))
    return np.stack(mats).astype(np.float32)


def _merge_kernel(x_ref, of_ref, ob_ref, z_ref, pin_ref, gd_ref, gp_ref, modv_ref, on_ref, pm_ref, pw_ref, ps_ref,
                  wud_ref, wup_ref, wo_ref, o_ref, *, seq):
    g = _group_of_row(pl.program_id(0) * ROW_TILE, seq)
    mod = modv_ref[pl.ds(g, 1), :]
    gate = mod[:, 2 * D_MODEL:3 * D_MODEL]
    o = (of_ref[0] + ob_ref[0]).reshape(ROW_TILE, DN_WIDTH)
    z = z_ref[...]
    onw = on_ref[...]
    ys = []
    for hh in range(DN_HEADS):
        lo, hi = hh * DN_HEAD_DIM, (hh + 1) * DN_HEAD_DIM
        oh = o[:, lo:hi]
        ys.append(oh * lax.rsqrt(jnp.mean(oh * oh, axis=-1, keepdims=True) + EPS) * onw * _silu(z[:, lo:hi]))
    y_dn = jnp.concatenate(ys, axis=1)
    pin = pin_ref[...]
    yp = []
    for gi in range(POOL_GROUPS):
        lo, hi = gi * POOL_GW, (gi + 1) * POOL_GW
        pooled = jnp.dot(pm_ref[0, gi], pin[:, lo:hi], precision=HIGHEST, preferred_element_type=F32)
        yp.append(jnp.dot(pooled, pw_ref[gi], preferred_element_type=F32))
    y_pool = jnp.concatenate(yp, axis=1) * ps_ref[...]
    m = (jax.nn.sigmoid(gd_ref[...]) * jnp.dot(y_dn.astype(BF16), wud_ref[...], preferred_element_type=F32)
         + jax.nn.sigmoid(gp_ref[...]) * jnp.dot(y_pool.astype(BF16), wup_ref[...], preferred_element_type=F32))
    out = jnp.dot(m.astype(BF16), wo_ref[...], preferred_element_type=F32)
    o_ref[...] = x_ref[...] + gate * out


def _merge(xall, o_f, o_b, z, pin, gd, gp, modv, onorm, pool_w, pool_scale, wud, wup, wo, seq, ctx, rows_out):
    assert ctx == ROW_TILE and seq % ROW_TILE == 0
    lat_tiles = BATCH * seq // ROW_TILE
    tiles_per_seq = seq // ROW_TILE
    cpt = ROW_TILE // DN_CHUNK
    pm = jnp.asarray(np.stack([_pool_matrices(ROW_TILE, GRID_W), _pool_matrices(ROW_TILE, ctx)]))
    row = lambda n: pl.BlockSpec((ROW_TILE, n), lambda i: (i, 0))
    full = lambda *s: pl.BlockSpec(s, lambda i: (0,) * len(s))

    def o_pos(i):
        lat = i < lat_tiles
        return (jnp.where(lat, i // tiles_per_seq, i - lat_tiles), jnp.where(lat, 1 + i % tiles_per_seq, 0), 0, 0)

    o_spec = pl.BlockSpec((1, cpt, DN_CHUNK, DN_WIDTH), o_pos)
    return pl.pallas_call(
        functools.partial(_merge_kernel, seq=seq),
        grid=(rows_out // ROW_TILE,),
        in_specs=[
            row(D_MODEL), o_spec, o_spec, row(DN_WIDTH), row(POOL_WIDTH), row(D_MODEL), row(D_MODEL),
            full(8, 6 * D_MODEL), full(1, DN_HEAD_DIM),
            pl.BlockSpec((1, POOL_GROUPS, ROW_TILE, ROW_TILE), lambda i: (jnp.where(i >= lat_tiles, 1, 0), 0, 0, 0)),
            full(POOL_GROUPS, POOL_GW, POOL_GW), full(1, POOL_WIDTH),
            full(DN_WIDTH, D_MODEL), full(POOL_WIDTH, D_MODEL), full(D_MODEL, D_MODEL),
        ],
        out_specs=row(D_MODEL),
        out_shape=jax.ShapeDtypeStruct((rows_out, D_MODEL), F32),
        compiler_params=_cparams("arbitrary"),
        name="merge",
    )(xall, o_f, o_b, z, pin, gd, gp, modv, onorm.reshape(1, DN_HEAD_DIM), pm, pool_w,
      pool_scale.reshape(1, POOL_WIDTH), wud, wup, wo)


def _col_max(x):
    return jnp.max(x, axis=0, keepdims=True)


def _col_min(x):
    return jnp.min(x, axis=0, keepdims=True)


def _oddeven_merge(lo, hi, r):
    step = r * 2
    if step < hi - lo:
        yield from _oddeven_merge(lo, hi, step)
        yield from _oddeven_merge(lo + r, hi, step)
        yield from [(i, i + r) for i in range(lo + r, hi - r, step)]
    else:
        yield (lo, lo + r)


def _oddeven_merge_sort(lo, hi):
    if hi - lo >= 1:
        mid = lo + (hi - lo) // 2
        yield from _oddeven_merge_sort(lo, mid)
        yield from _oddeven_merge_sort(mid + 1, hi)
        yield from _oddeven_merge(lo, hi, 1)


def _exchange(v, i, j):
    v[i], v[j] = jnp.maximum(v[i], v[j]), jnp.minimum(v[i], v[j])


def _sorted_top(blocks):
    k = PEER_TOPK
    v = list(blocks)
    for i, j in _oddeven_merge_sort(0, k - 1):
        _exchange(v, i, j)
    shift = SUBLANES // 2
    while shift >= 1:
        w = [pltpu.roll(x, shift, 0) for x in v]
        v = [jnp.maximum(v[j], w[k - 1 - j]) for j in range(k)]
        stride = k // 2
        while stride >= 1:
            for i in range(k):
                if i & stride == 0:
                    _exchange(v, i, i + stride)
            stride //= 2
        shift //= 2
    return v


def _sublane_total(x):
    shift = SUBLANES // 2
    while shift >= 1:
        x = x + pltpu.roll(x, shift, 0)
        shift //= 2
    return x


def _unambiguous(blocks, top):
    ok = top[0] > top[1]
    for a in range(1, PEER_TOPK - 1):
        ok = ok & (top[a] > top[a + 1])
    ge = None
    for blk in blocks:
        one = jnp.where(blk >= top[PEER_TOPK - 1], 1.0, 0.0)
        ge = one if ge is None else ge + one
    return ok & (_sublane_total(ge) == float(PEER_TOPK))


def _top_select(scores, iota):
    n = float(scores[0].shape[0])
    cur = list(scores)
    idxs = [[] for _ in cur]
    for _ in range(PEER_TOPK):
        for j in range(len(cur)):
            m = _col_max(cur[j])
            cand = jnp.where(cur[j] == m, iota, n)
            idx = _col_min(cand)
            cur[j] = jnp.where(cand == idx, NEG_INF, cur[j])
            idxs[j].append(idx)
    return [jnp.concatenate(i, axis=0) for i in idxs]


def _peer_route_kernel(x_ref, modv_ref, nw_ref, wq_ref, keys_ref, hn_ref, n1_ref, e1_ref, r2_ref, e2_ref,
                       q_scr, *, seq):
    g = _group_of_row(pl.program_id(0) * ROW_TILE, seq)
    mod = modv_ref[pl.ds(g, 1), :]
    hn = _norm_mod(x_ref[...], nw_ref[...], mod, 3).astype(BF16)
    hn_ref[...] = hn
    q = jnp.dot(hn, wq_ref[...], preferred_element_type=F32)
    nsub = ROW_TILE // LANES
    for j in range(2 * PEER_HEADS):
        for t in range(nsub):
            q_scr[j, t] = q[t * LANES:(t + 1) * LANES, j * PEER_HALF:(j + 1) * PEER_HALF]
    k = PEER_TOPK
    nblk = N_KEYS // SUBLANES
    iota_k = lax.broadcasted_iota(jnp.int32, (N_KEYS, LANES), 0).astype(F32)
    iota_a = lax.broadcasted_iota(jnp.int32, (k, LANES), 0).astype(F32)
    split = lambda s: [s[SUBLANES * j:SUBLANES * (j + 1), :] for j in range(nblk)]
    rep = lambda row: jnp.broadcast_to(row, (SUBLANES, LANES))

    def head(hh, carry):
        for t in range(nsub):
            lanes = slice(t * LANES, (t + 1) * LANES)
            s1 = lax.dot_general(keys_ref[2 * hh], q_scr[2 * hh, t], NT_DIMS, preferred_element_type=F32)
            s2 = lax.dot_general(keys_ref[2 * hh + 1], q_scr[2 * hh + 1, t], NT_DIMS, preferred_element_type=F32)
            b1, b2 = split(s1), split(s2)
            top1, top2 = _sorted_top(b1), _sorted_top(b2)
            c = jnp.concatenate([x[0:1, :] for x in top1], axis=0)
            d = jnp.concatenate([x[0:1, :] for x in top2], axis=0)

            cnt = jnp.zeros((k, LANES), F32)
            for _ in range(k):
                dn = jnp.full((k, LANES), NEG_INF, F32)
                for b in range(k):
                    dn = jnp.where(cnt == float(b), d[b:b + 1, :], dn)
                f = c + dn
                cand = jnp.where(f == _col_max(f), iota_a, float(k))
                cnt = cnt + (cand == _col_min(cand)).astype(F32)
            e1c = jnp.exp(c - c[0:1, :])
            e2d = jnp.exp(d - d[0:1, :])
            part = jnp.zeros((k, LANES), F32)
            for b in range(k):
                part = part + jnp.where(cnt > float(b), e2d[b:b + 1, :], 0.0)
            zsum = jnp.sum(e1c * part, axis=0, keepdims=True)

            def by_value():
                n1b = [jnp.zeros((SUBLANES, LANES), F32)] * nblk
                r2b = [jnp.full((SUBLANES, LANES), float(k), F32)] * nblk
                for a in range(k):
                    cnt_a = rep(cnt[a:a + 1, :])
                    n1b = [jnp.where(blk == top1[a], cnt_a, acc) for blk, acc in zip(b1, n1b)]
                    r2b = [jnp.where(blk == top2[a], float(a), acc) for blk, acc in zip(b2, r2b)]
                return jnp.concatenate(n1b, axis=0), jnp.concatenate(r2b, axis=0)

            def by_index():
                idx1, idx2 = _top_select([s1, s2], iota_k)
                n1 = jnp.zeros((N_KEYS, LANES), F32)
                r2 = jnp.full((N_KEYS, LANES), float(k), F32)
                for a in range(k):
                    n1 = jnp.where(iota_k == idx1[a:a + 1, :], cnt[a:a + 1, :], n1)
                    r2 = jnp.where(iota_k == idx2[a:a + 1, :], float(a), r2)
                return n1, r2

            clean = jnp.all(_unambiguous(b1, top1) & _unambiguous(b2, top2))
            n1, r2 = lax.cond(clean, by_value, by_index)
            n1_ref[hh, 0, :, lanes] = _bf16_pair_word(n1)
            e1_ref[hh, 0, :, lanes] = _bf16_pair_word(0.5 * jnp.exp(s1 - c[0:1, :]) / zsum)
            r2_ref[hh, 0, :, lanes] = r2.astype(BF16)
            e2_ref[hh, 0, :, lanes] = jnp.exp(s2 - d[0:1, :]).astype(BF16)
        return carry

    lax.fori_loop(0, PEER_HEADS, head, 0)


def _peer_route(xall, modv, norm_w, wq, keys, seq, rows_out):
    row = lambda n: pl.BlockSpec((ROW_TILE, n), lambda i: (i, 0))
    full = lambda *s: pl.BlockSpec(s, lambda i: (0,) * len(s))
    tab = pl.BlockSpec((PEER_HEADS, 1, N_KEYS, ROW_TILE), lambda i: (0, i, 0, 0))
    tab_shape = lambda dt: jax.ShapeDtypeStruct((PEER_HEADS, rows_out // ROW_TILE, N_KEYS, ROW_TILE), dt)
    return pl.pallas_call(
        functools.partial(_peer_route_kernel, seq=seq),
        grid=(rows_out // ROW_TILE,),
        in_specs=[row(D_MODEL), full(8, 6 * D_MODEL), full(1, D_MODEL), full(D_MODEL, PEER_HEADS * PEER_QDIM),
                  full(2 * PEER_HEADS, N_KEYS, PEER_HALF)],
        out_specs=[row(D_MODEL), tab, tab, tab, tab],
        out_shape=[jax.ShapeDtypeStruct((rows_out, D_MODEL), BF16), tab_shape(jnp.uint32), tab_shape(jnp.uint32),
                   tab_shape(BF16), tab_shape(BF16)],
        scratch_shapes=[pltpu.VMEM((2 * PEER_HEADS, ROW_TILE // LANES, LANES, PEER_HALF), F32)],
        compiler_params=_cparams("arbitrary"),
        name="peer_route",
    )(xall, modv, norm_w.reshape(1, D_MODEL), wq, keys.reshape(2 * PEER_HEADS, N_KEYS, PEER_HALF))


PEER_PIPE_LAG = 2
PEER_MXU_BLOCK = 256


def _peer_tile(n, lag, n_tiles, n_exp):
    m = jnp.clip(n - lag, 0, n_tiles - 1)
    return m // n_exp, m % n_exp


def _peer_dense_kernel(x_ref, modv_ref, hn_ref, u_ref, vt_ref, n1_ref, e1_ref, r2_ref, e2_ref, fn_ref, o_ref,
                       acc_ref, a0_ref, a1_ref, act0_ref, act1_ref, hn_scr, *, seq, final, n_tiles, n_exp):
    n = pl.program_id(0)
    t_c, e_c = _peer_tile(n, PEER_PIPE_LAG, n_tiles, n_exp)

    @pl.when(n == 0)
    def _():
        for ref in (acc_ref, a0_ref, a1_ref, act0_ref, act1_ref):
            ref[...] = jnp.zeros_like(ref)

    @pl.when(_peer_tile(n, 0, n_tiles, n_exp)[1] == 0)
    def _():
        hn_scr[...] = hn_ref[...]

    def step(a_next, a_cur, act_next, act_cur):
        zero = jnp.zeros((), BF16)
        blk = PEER_MXU_BLOCK

        def stage_a(rb, cb):
            rows, cols = slice(rb * blk, (rb + 1) * blk), slice(cb * blk, (cb + 1) * blk)
            a_next[rows, cols] = lax.dot_general(u_ref[rows, :], hn_scr[cols, :], NT_DIMS,
                                                 preferred_element_type=F32)

        def stage_b(t, i):
            lanes = slice(t * ROW_TILE, (t + 1) * ROW_TILE)
            wsum = None
            for hh in range(PEER_HEADS):
                keep = r2_ref[hh, t] < _pair_word_rows(n1_ref[hh, t, i:i + 1, :])
                term = jnp.where(keep, e2_ref[hh, t], zero) * _pair_word_rows(e1_ref[hh, t, i:i + 1, :])
                wsum = term if wsum is None else wsum + term
            rows = slice(i * N_KEYS, (i + 1) * N_KEYS)
            a = a_cur[rows, lanes]
            gelu = a * (1.0 + lax.erf(a * (2.0 ** -0.5)))
            act_next[rows, lanes] = gelu.astype(BF16) * wsum

        def stage_c(rb, cb):
            rows, cols = slice(rb * blk, (rb + 1) * blk), slice(cb * blk, (cb + 1) * blk)
            prev = jnp.where(e_c == 0, 0.0, acc_ref[rows, cols])
            acc_ref[rows, cols] = prev + jnp.dot(vt_ref[rows, :], act_cur[:, cols], preferred_element_type=F32)

        a_pieces = [(rb, cb) for cb in range(PEER_TOK_TILE // blk) for rb in range(PEER_EXP_TILE // blk)]
        c_pieces = [(rb, cb) for cb in range(PEER_TOK_TILE // blk) for rb in range(D_MODEL // blk)]
        b_pieces = [(t, i) for t in range(PEER_TOK_TILE // ROW_TILE) for i in range(PEER_EXP_TILE // N_KEYS)]
        per_mxu = len(b_pieces) // (len(a_pieces) + len(c_pieces))
        mxu_pieces = [x for pair in zip([(stage_a, p) for p in a_pieces], [(stage_c, p) for p in c_pieces])
                      for x in pair]
        for k, (fn, piece) in enumerate(mxu_pieces):
            fn(*piece)
            for t, i in b_pieces[k * per_mxu:(k + 1) * per_mxu]:
                stage_b(t, i)

    step(a0_ref, a1_ref, act0_ref, act1_ref)
    a1_ref[...] = a0_ref[...]
    act1_ref[...] = act0_ref[...]

    @pl.when((e_c == n_exp - 1) & (n >= PEER_PIPE_LAG))
    def _():
        g = _group_of_row(t_c * PEER_TOK_TILE, seq)
        mod = modv_ref[pl.ds(g, 1), :]
        y = x_ref[...] + mod[:, 5 * D_MODEL:6 * D_MODEL] * acc_ref[...].T
        if final:
            y = y * lax.rsqrt(jnp.mean(y * y, axis=-1, keepdims=True) + EPS) * fn_ref[...]
        o_ref[...] = y


def _peer_dense(xall, modv, hn, u_bf, vt_bf, n1, e1, r2, e2, final_norm, seq, rows_out, final):
    tt, et = PEER_TOK_TILE, PEER_EXP_TILE
    n_first = et // N_KEYS
    n_exp = N_EXPERTS // et
    n_tiles = (rows_out // tt) * n_exp
    tile = lambda lag: (lambda n: _peer_tile(n, lag, n_tiles, n_exp))
    ta, tb, tc = tile(0), tile(1), tile(PEER_PIPE_LAG)
    full = lambda *s: pl.BlockSpec(s, lambda n: (0,) * len(s))
    per_first = pl.BlockSpec((PEER_HEADS, tt // ROW_TILE, n_first, ROW_TILE), lambda n: (0, tb(n)[0], tb(n)[1], 0))
    per_second = pl.BlockSpec((PEER_HEADS, tt // ROW_TILE, N_KEYS, ROW_TILE), lambda n: (0, tb(n)[0], 0, 0))
    return pl.pallas_call(
        functools.partial(_peer_dense_kernel, seq=seq, final=final, n_tiles=n_tiles, n_exp=n_exp),
        grid=(n_tiles + PEER_PIPE_LAG,),
        in_specs=[pl.BlockSpec((tt, D_MODEL), lambda n: (tc(n)[0], 0)),
                  full(8, 6 * D_MODEL),
                  pl.BlockSpec((tt, D_MODEL), lambda n: (ta(n)[0], 0)),
                  pl.BlockSpec((et, D_MODEL), lambda n: (ta(n)[1], 0)),
                  pl.BlockSpec((D_MODEL, et), lambda n: (0, tc(n)[1])),
                  per_first, per_first, per_second, per_second, full(1, D_MODEL)],
        out_specs=pl.BlockSpec((tt, D_MODEL), lambda n: (tc(n)[0], 0)),
        out_shape=jax.ShapeDtypeStruct((rows_out, D_MODEL), F32),
        scratch_shapes=[pltpu.VMEM((D_MODEL, tt), F32), pltpu.VMEM((et, tt), F32), pltpu.VMEM((et, tt), F32),
                        pltpu.VMEM((et, tt), BF16), pltpu.VMEM((et, tt), BF16), pltpu.VMEM((tt, D_MODEL), BF16)],
        compiler_params=_cparams("arbitrary"),
        name="peer_dense",
    )(xall, modv, hn, u_bf, vt_bf, n1, e1, r2, e2, final_norm.reshape(1, D_MODEL))


def _reorder_in_weight(w):
    s = np.cumsum((0, 3 * DN_WIDTH, DN_WIDTH, POOL_WIDTH, 2 * DN_HEADS, 2 * DN_HEADS, D_MODEL, D_MODEL))
    qkv, z, pin, b, a, gd, gp = (w[:, s[i]:s[i + 1]] for i in range(7))
    pad = jnp.zeros((w.shape[0], BA_PAD - 4 * DN_HEADS), w.dtype)
    return jnp.concatenate([qkv, z, pin, gd, gp, b, a, pad], axis=1).astype(BF16)


def _forward(x, c, ctx, c_ctx, w_mod, b_mod, norm_mix, w_in, conv_w, a_log, dt_bias, dn_out_norm, pool_w, pool_scale,
             w_up_dn, w_up_pool, w_out, norm_ffn, peer_wq, peer_keys, peer_u, peer_v, final_norm):
    seq, nctx = x.shape[1], ctx.shape[1]
    nlat = BATCH * seq
    xall = jnp.concatenate([x.reshape(nlat, D_MODEL), ctx.reshape(BATCH * nctx, D_MODEL)], axis=0)
    rows = xall.shape[0]
    cvec = jnp.concatenate([c, c_ctx[None, :], jnp.zeros((8 - BATCH - 1, D_MODEL), F32)], axis=0)
    modv_all = _modulation(cvec, w_mod, b_mod)
    for i in range(DEPTH):
        last = i == DEPTH - 1
        rows_out = nlat if last else rows
        modv = modv_all[i]
        qkv, z, pin, gd, gp, ba = _inproj(xall, modv, norm_mix[i], _reorder_in_weight(w_in[i]), seq)
        q, k, v, bg = _dnconv(qkv, ba, conv_w[i], a_log[i], dt_bias[i], seq, nctx)
        o_f, o_b = _dnscan(*_dnprep(q, k, v, bg, seq, nctx), seq, nctx)
        xall = _merge(xall, o_f, o_b, z, pin, gd, gp, modv, dn_out_norm[i], pool_w[i], pool_scale[i],
                      w_up_dn[i].astype(BF16), w_up_pool[i].astype(BF16), w_out[i].astype(BF16), seq, nctx, rows_out)
        hn, n1, e1, r2, e2 = _peer_route(xall, modv, norm_ffn[i], peer_wq[i].astype(BF16), peer_keys[i], seq, rows_out)
        xall = _peer_dense(xall, modv, hn, peer_u[i].astype(BF16), peer_v[i].T.astype(BF16), n1, e1, r2, e2,
                           final_norm, seq, rows_out, last)
    return xall.reshape(BATCH, seq, D_MODEL)


def kernel(x, c, ctx, c_ctx, w_mod, b_mod, norm_mix, w_in, conv_w, a_log, dt_bias, dn_out_norm, pool_w, pool_scale, w_up_dn, w_up_pool, w_out, norm_ffn, peer_wq, peer_keys, peer_u, peer_v, final_norm):
    return _forward(x, c, ctx, c_ctx, w_mod, b_mod, norm_mix, w_in, conv_w, a_log, dt_bias, dn_out_norm, pool_w,
                    pool_scale, w_up_dn, w_up_pool, w_out, norm_ffn, peer_wq, peer_keys, peer_u, peer_v, final_norm)
```

```python
import functools

import numpy as np
import jax
import jax.numpy as jnp
from jax import lax
from jax.experimental import pallas as pl
from jax.experimental.pallas import tpu as pltpu

D_MODEL = 1024
BATCH = 2
DEPTH = 2
GRID_W = 64
EPS = 1e-6

DN_HEADS = 4
DN_HEAD_DIM = 128
DN_WIDTH = DN_HEADS * DN_HEAD_DIM
SHORT_CONV = 4
DN_CHUNK = 64
DN_CHAINS = 2 * DN_HEADS
DN_PREP_CHUNKS = 2

POOL_WINDOWS = (2, 4, 8, 16)
POOL_GROUPS = 4
POOL_WIDTH = D_MODEL // 2
POOL_GW = POOL_WIDTH // POOL_GROUPS

PEER_HEADS = 8
N_KEYS = 128
N_EXPERTS = N_KEYS * N_KEYS
PEER_TOPK = 16
PEER_QDIM = 256
PEER_HALF = PEER_QDIM // 2

BA_PAD = 128
IN_COLS_R = 3 * DN_WIDTH + DN_WIDTH + POOL_WIDTH + 2 * D_MODEL + BA_PAD

LANES = 128
SUBLANES = 8
ROW_TILE = 256
PEER_TOK_TILE = 512
PEER_EXP_TILE = 1024
PEER_SLAB = 256
HALO = 8
VMEM_LIMIT = 56 * 1024 * 1024

F32 = jnp.float32
BF16 = jnp.bfloat16
HIGHEST = lax.Precision.HIGHEST
NEG_INF = float("-inf")
NT_DIMS = (((1,), (1,)), ((), ()))


def _cparams(*sem):
    return pltpu.CompilerParams(dimension_semantics=sem, vmem_limit_bytes=VMEM_LIMIT)


def _group_of_row(row0, seq):
    return jnp.where(row0 < seq, 0, jnp.where(row0 < 2 * seq, 1, 2))


def _silu(x):
    return x * jax.nn.sigmoid(x)


def _split_bf16(a):
    hi = a.astype(BF16)
    lo = (a - hi.astype(F32)).astype(BF16)
    return hi, lo


def _bf16_pair_word(x):
    hi = lax.bitcast_convert_type(x.astype(BF16).astype(F32), jnp.uint32)
    return hi | (hi >> 16)


def _pair_word_rows(row):
    tile = pltpu.bitcast(jnp.broadcast_to(row, (SUBLANES, row.shape[1])), BF16)
    return jnp.concatenate([tile] * (N_KEYS // tile.shape[0]), axis=0)


def _dot_split(a, b):
    ah, al = _split_bf16(a)
    bh, bl = _split_bf16(b)
    a4 = jnp.concatenate([ah, al, ah, al], axis=1)
    b4 = jnp.concatenate([bh, bh, bl, bl], axis=0)
    return jnp.dot(a4, b4, preferred_element_type=F32)


def _mod_kernel(c_ref, w_ref, b_ref, o_ref):
    o_ref[0] = jnp.dot(_silu(c_ref[...]), w_ref[0], preferred_element_type=F32) + b_ref[0]


def _modulation(cvec, w_mod, b_mod):
    tn = 1536
    return pl.pallas_call(
        _mod_kernel,
        grid=(DEPTH, 6 * D_MODEL // tn),
        in_specs=[
            pl.BlockSpec((8, D_MODEL), lambda l, j: (0, 0)),
            pl.BlockSpec((1, D_MODEL, tn), lambda l, j: (l, 0, j)),
            pl.BlockSpec((1, 1, tn), lambda l, j: (l, 0, j)),
        ],
        out_specs=pl.BlockSpec((1, 8, tn), lambda l, j: (l, 0, j)),
        out_shape=jax.ShapeDtypeStruct((DEPTH, 8, 6 * D_MODEL), F32),
        compiler_params=_cparams("arbitrary", "arbitrary"),
        name="modulation",
    )(cvec, w_mod, b_mod.reshape(DEPTH, 1, 6 * D_MODEL))


def _norm_mod(x, nw, mod, k):
    ms = jnp.mean(x * x, axis=-1, keepdims=True)
    xn = x * lax.rsqrt(ms + EPS) * nw
    sh = mod[:, k * D_MODEL:(k + 1) * D_MODEL]
    sc = mod[:, (k + 1) * D_MODEL:(k + 2) * D_MODEL]
    return xn * (1 + sc) + sh


def _inproj_kernel(x_ref, modv_ref, nw_ref, w_ref, qkv_ref, z_ref, pin_ref, gd_ref, gp_ref, ba_ref, *, seq):
    g = _group_of_row(pl.program_id(0) * ROW_TILE, seq)
    mod = modv_ref[pl.ds(g, 1), :]
    h = _norm_mod(x_ref[...], nw_ref[...], mod, 0)
    y = jnp.dot(h.astype(BF16), w_ref[...], preferred_element_type=F32)
    o = 0
    for ref in (qkv_ref, z_ref, pin_ref, gd_ref, gp_ref, ba_ref):
        n = ref.shape[1]
        ref[...] = y[:, o:o + n]
        o += n


def _inproj(xall, modv, norm_w, w_in_r, seq):
    rows = xall.shape[0]
    widths = (3 * DN_WIDTH, DN_WIDTH, POOL_WIDTH, D_MODEL, D_MODEL, BA_PAD)
    return pl.pallas_call(
        functools.partial(_inproj_kernel, seq=seq),
        grid=(rows // ROW_TILE,),
        in_specs=[
            pl.BlockSpec((ROW_TILE, D_MODEL), lambda i: (i, 0)),
            pl.BlockSpec((8, 6 * D_MODEL), lambda i: (0, 0)),
            pl.BlockSpec((1, D_MODEL), lambda i: (0, 0)),
            pl.BlockSpec((D_MODEL, IN_COLS_R), lambda i: (0, 0)),
        ],
        out_specs=[pl.BlockSpec((ROW_TILE, n), lambda i: (i, 0)) for n in widths],
        out_shape=[jax.ShapeDtypeStruct((rows, n), F32) for n in widths],
        compiler_params=_cparams("arbitrary"),
        name="inproj",
    )(xall, modv, norm_w.reshape(1, D_MODEL), w_in_r)


def _dnconv_kernel(cur_ref, prev_ref, next_ref, cw_ref, ba_ref, alog_ref, dtb_ref,
                   q_ref, k_ref, v_ref, bg_ref, *, seq, ctx):
    row0 = pl.program_id(0) * ROW_TILE
    nlat = BATCH * seq
    is_start = (row0 == 0) | (row0 == seq) | (row0 == nlat) | (row0 == nlat + ctx)
    row1 = row0 + ROW_TILE
    is_end = (row1 == seq) | (row1 == nlat) | (row1 == nlat + ctx) | (row1 == nlat + BATCH * ctx)
    prev = jnp.where(is_start, 0.0, prev_ref[...])
    nxt = jnp.where(is_end, 0.0, next_ref[...])
    ext = jnp.concatenate([prev, cur_ref[...], nxt], axis=0)
    left = SHORT_CONV // 2
    cw = cw_ref[...]
    y = None
    for j in range(SHORT_CONV):
        o = HALO - left + j
        term = ext[o:o + ROW_TILE, :] * cw[j:j + 1, :]
        y = term if y is None else y + term
    y = _silu(y)
    for hh in range(DN_HEADS):
        lo, hi = hh * DN_HEAD_DIM, (hh + 1) * DN_HEAD_DIM
        qh = y[:, lo:hi]
        kh = y[:, DN_WIDTH + lo:DN_WIDTH + hi]
        q_ref[:, lo:hi] = qh * lax.rsqrt(jnp.sum(qh * qh, axis=-1, keepdims=True) + EPS) * (DN_HEAD_DIM ** -0.5)
        k_ref[:, lo:hi] = kh * lax.rsqrt(jnp.sum(kh * kh, axis=-1, keepdims=True) + EPS)
    v_ref[...] = y[:, 2 * DN_WIDTH:]
    ba = ba_ref[...]
    beta = jax.nn.sigmoid(ba)
    xs = ba + dtb_ref[...]
    softplus = jnp.maximum(xs, 0.0) + jnp.log(1.0 + jnp.exp(-jnp.abs(xs)))
    gdec = -jnp.exp(alog_ref[...]) * softplus
    col = lax.broadcasted_iota(jnp.int32, ba.shape, 1)
    bg_ref[...] = jnp.where(col < 2 * DN_HEADS, beta, jnp.where(col < 4 * DN_HEADS, gdec, 0.0))


def _dnconv(qkv, ba, conv_w, a_log, dt_bias, seq, ctx):
    rows = qkv.shape[0]
    nh = ROW_TILE // HALO
    last = rows // HALO - 1
    pad = jnp.zeros((2 * DN_HEADS,), F32)
    tail = jnp.zeros((BA_PAD - 4 * DN_HEADS,), F32)
    alog = jnp.concatenate([pad, a_log.reshape(-1), tail]).reshape(1, BA_PAD)
    dtb = jnp.concatenate([pad, dt_bias.reshape(-1), tail]).reshape(1, BA_PAD)
    w3 = 3 * DN_WIDTH
    return pl.pallas_call(
        functools.partial(_dnconv_kernel, seq=seq, ctx=ctx),
        grid=(rows // ROW_TILE,),
        in_specs=[
            pl.BlockSpec((ROW_TILE, w3), lambda i: (i, 0)),
            pl.BlockSpec((HALO, w3), lambda i: (jnp.maximum(i * nh - 1, 0), 0)),
            pl.BlockSpec((HALO, w3), lambda i: (jnp.minimum((i + 1) * nh, last), 0)),
            pl.BlockSpec((SHORT_CONV, w3), lambda i: (0, 0)),
            pl.BlockSpec((ROW_TILE, BA_PAD), lambda i: (i, 0)),
            pl.BlockSpec((1, BA_PAD), lambda i: (0, 0)),
            pl.BlockSpec((1, BA_PAD), lambda i: (0, 0)),
        ],
        out_specs=[pl.BlockSpec((ROW_TILE, DN_WIDTH), lambda i: (i, 0))] * 3
        + [pl.BlockSpec((ROW_TILE, BA_PAD), lambda i: (i, 0))],
        out_shape=[jax.ShapeDtypeStruct((rows, DN_WIDTH), F32)] * 3
        + [jax.ShapeDtypeStruct((rows, BA_PAD), F32)],
        compiler_params=_cparams("arbitrary"),
        name="dnconv",
    )(qkv, qkv, qkv, conv_w, ba, alog, dtb)


def _dnprep_kernel(q_ref, k_ref, v_ref, bg_ref, bgt_ref, wq_ref, u_ref, lk_ref, eg_ref):
    c = DN_CHUNK
    ri = lax.broadcasted_iota(jnp.int32, (c, c), 0)
    ci = lax.broadcasted_iota(jnp.int32, (c, c), 1)
    dirs = ((ri >= ci, ri > ci, ci >= ri, c - 1), (ri <= ci, ri < ci, ci <= ri, 0))
    chains = []
    for j in range(DN_PREP_CHUNKS):
        rows = slice(j * c, (j + 1) * c)
        bg = bg_ref[rows, :]
        bgh, bgl = _split_bf16(bg)
        bgth, bgtl = _split_bf16(bgt_ref[j])
        bg2 = jnp.concatenate([bgh, bgl], axis=0)
        bgt2 = jnp.concatenate([bgth, bgtl], axis=1)
        egs = []
        for d, (incl, strict, incl_t, last) in enumerate(dirs):
            m = incl.astype(BF16)
            mt = incl_t.astype(BF16)
            gc_all = jnp.dot(jnp.concatenate([m, m], axis=1), bg2, preferred_element_type=F32)
            gr_all = jnp.dot(bgt2, jnp.concatenate([mt, mt], axis=0), preferred_element_type=F32)
            for hh in range(DN_HEADS):
                lo, hi = hh * DN_HEAD_DIM, (hh + 1) * DN_HEAD_DIM
                ch = d * DN_HEADS + hh
                gcol = 2 * DN_HEADS + ch
                q = q_ref[rows, lo:hi]
                k = k_ref[rows, lo:hi]
                v = v_ref[rows, lo:hi]
                beta = bg[:, ch:ch + 1]
                gc = gc_all[:, gcol:gcol + 1]
                gr = gr_all[gcol:gcol + 1, :]
                glast = gc_all[last:last + 1, gcol:gcol + 1]
                decay = jnp.where(incl, jnp.exp(jnp.minimum(gc - gr, 0.0)), 0.0)
                kk = lax.dot_general(k, k, NT_DIMS, preferred_element_type=F32)
                qk = lax.dot_general(q, k, NT_DIMS, preferred_element_type=F32)
                egc = jnp.exp(gc)
                cols = slice(ch * DN_HEAD_DIM, (ch + 1) * DN_HEAD_DIM)
                wq_ref[0, j, c:2 * c, cols] = (q * egc).astype(BF16)
                lk_ref[0, j, ch, 0:c, :] = jnp.where(incl, qk * decay, 0.0).astype(BF16)
                lk_ref[0, j, ch, c:, :] = (k * jnp.exp(glast - gc)).T.astype(BF16)
                egs.append(jnp.broadcast_to(jnp.exp(glast), (1, LANES)))
                chains.append((j, cols, jnp.where(strict, -(beta * kk * decay), 0.0),
                               jnp.concatenate([k * (beta * egc), v * beta], axis=1)))
        eg_ref[0, j] = jnp.concatenate(egs, axis=0)
    ps = [ch[2] for ch in chains]
    ys = [ch[3] for ch in chains]
    ys = [y + _dot_split(p, y) for p, y in zip(ps, ys)]
    for _ in range(int(np.log2(c)) - 1):
        ps = [_dot_split(p, p) for p in ps]
        ys = [y + _dot_split(p, y) for p, y in zip(ps, ys)]
    for (j, cols, _, _), y in zip(chains, ys):
        wq_ref[0, j, 0:c, cols] = y[:, :DN_HEAD_DIM].astype(BF16)
        u_ref[0, j, :, cols] = y[:, DN_HEAD_DIM:]


def _dnprep(q, k, v, bg, seq, ctx):
    rows = q.shape[0]
    c = DN_CHUNK
    n = DN_PREP_CHUNKS
    nct, nlt = ctx // c, seq // c
    assert nct % n == 0 and nlt % n == 0
    nch = nct + nlt
    bgt = bg[:, :4 * DN_HEADS].reshape(rows // c, c, 4 * DN_HEADS).transpose(0, 2, 1)

    def seq_pos(i):
        i = i * n
        lat = i < BATCH * nlt
        j = i - BATCH * nlt
        return jnp.where(lat, i // nlt, j // nct), jnp.where(lat, nct + i % nlt, j % nct) // n

    def out_spec(*tail):
        return pl.BlockSpec((1, n) + tail, lambda i: seq_pos(i) + (0,) * len(tail))

    wide = pl.BlockSpec((n * c, DN_WIDTH), lambda i: (i, 0))
    width = DN_CHAINS * DN_HEAD_DIM
    return pl.pallas_call(
        _dnprep_kernel,
        grid=(rows // (n * c),),
        in_specs=[wide, wide, wide,
                  pl.BlockSpec((n * c, BA_PAD), lambda i: (i, 0)),
                  pl.BlockSpec((n, 4 * DN_HEADS, c), lambda i: (i, 0, 0))],
        out_specs=[out_spec(2 * c, width), out_spec(c, width), out_spec(DN_CHAINS, c + DN_HEAD_DIM, c),
                   out_spec(DN_CHAINS, LANES)],
        out_shape=[jax.ShapeDtypeStruct((BATCH, nch, 2 * c, width), BF16),
                   jax.ShapeDtypeStruct((BATCH, nch, c, width), F32),
                   jax.ShapeDtypeStruct((BATCH, nch, DN_CHAINS, c + DN_HEAD_DIM, c), BF16),
                   jax.ShapeDtypeStruct((BATCH, nch, DN_CHAINS, LANES), F32)],
        compiler_params=_cparams("arbitrary"),
        name="dnprep",
    )(q, k, v, bg, bgt)


def _dnscan_kernel(wqf_ref, uf_ref, lkf_ref, egf_ref, wqb_ref, ub_ref, lkb_ref, egb_ref, of_ref, ob_ref, s_ref):
    @pl.when(pl.program_id(0) == 0)
    def _():
        s_ref[...] = jnp.zeros_like(s_ref)

    c = DN_CHUNK
    dirs = ((wqf_ref, uf_ref, lkf_ref, egf_ref, of_ref), (wqb_ref, ub_ref, lkb_ref, egb_ref, ob_ref))
    chains = [(d, b, hh) for d in range(2) for b in range(BATCH) for hh in range(DN_HEADS)]
    sidx = lambda d, b, hh: (b * 2 + d) * DN_HEADS + hh
    cols = lambda hh: slice(hh * DN_HEAD_DIM, (hh + 1) * DN_HEAD_DIM)
    ss = [s_ref[sidx(*ch)] for ch in chains]
    r1 = [jnp.dot(dirs[d][0][b, 0, :, cols(hh)], s.astype(BF16), preferred_element_type=F32)
          for (d, b, hh), s in zip(chains, ss)]
    vn = [dirs[d][1][b, 0, :, cols(hh)] - r[:c] for (d, b, hh), r in zip(chains, r1)]
    r2 = [jnp.dot(dirs[d][2][b, 0, hh], v.astype(BF16), preferred_element_type=F32)
          for (d, b, hh), v in zip(chains, vn)]
    for (d, b, hh), s, a1, a2 in zip(chains, ss, r1, r2):
        dirs[d][4][b, 0, :, cols(hh)] = a1[c:] + a2[:c]
        row = d * DN_HEADS + hh
        s_ref[sidx(d, b, hh)] = s * dirs[d][3][b, 0, row:row + 1, :] + a2[c:]


def _dnscan(wq, u, lk, eg, seq, ctx):
    c = DN_CHUNK
    nct, nlt = ctx // c, seq // c
    nch = nct + nlt

    def bwd(s):
        return jnp.where(s < nct, nct - 1 - s, 2 * nct + nlt - 1 - s)

    def specs(pos, d):
        return [pl.BlockSpec((BATCH, 1, 2 * c, DN_WIDTH), lambda s: (0, pos(s), 0, d)),
                pl.BlockSpec((BATCH, 1, c, DN_WIDTH), lambda s: (0, pos(s), 0, d)),
                pl.BlockSpec((BATCH, 1, DN_HEADS, c + DN_HEAD_DIM, c), lambda s: (0, pos(s), d, 0, 0)),
                pl.BlockSpec((BATCH, 1, DN_CHAINS, LANES), lambda s: (0, pos(s), 0, 0))]

    fwd = lambda s: s
    return pl.pallas_call(
        _dnscan_kernel,
        grid=(nch,),
        in_specs=specs(fwd, 0) + specs(bwd, 1),
        out_specs=[pl.BlockSpec((BATCH, 1, c, DN_WIDTH), lambda s: (0, s, 0, 0)),
                   pl.BlockSpec((BATCH, 1, c, DN_WIDTH), lambda s: (0, bwd(s), 0, 0))],
        out_shape=[jax.ShapeDtypeStruct((BATCH, nch, c, DN_WIDTH), F32)] * 2,
        scratch_shapes=[pltpu.VMEM((BATCH * DN_CHAINS, DN_HEAD_DIM, DN_HEAD_DIM), F32)],
        compiler_params=_cparams("arbitrary"),
        name="dnscan",
    )(wq, u, lk, eg, wq, u, lk, eg)


def _pool_matrices(tile, seg):
    t = np.arange(tile)
    p = t % seg
    mats = []
    for win in POOL_WINDOWS:
        lo = np.clip(p - win // 2, 0, seg)
        hi = np.clip(p + win - win // 2, 0, seg)
        same = (t[:, None] // seg) == (t[None, :] // seg)
        inside = same & (p[None, :] >= lo[:, None]) & (p[None, :] < hi[:, None])
        mats.append(inside / (hi - lo)[:, None].astype(np.float64) - np.eye(tile))
    return np.stack(mats).astype(np.float32)


def _merge_kernel(x_ref, of_ref, ob_ref, z_ref, pin_ref, gd_ref, gp_ref, modv_ref, on_ref, pm_ref, pw_ref, ps_ref,
                  wud_ref, wup_ref, wo_ref, o_ref, *, seq):
    g = _group_of_row(pl.program_id(0) * ROW_TILE, seq)
    mod = modv_ref[pl.ds(g, 1), :]
    gate = mod[:, 2 * D_MODEL:3 * D_MODEL]
    o = (of_ref[0] + ob_ref[0]).reshape(ROW_TILE, DN_WIDTH)
    z = z_ref[...]
    onw = on_ref[...]
    ys = []
    for hh in range(DN_HEADS):
        lo, hi = hh * DN_HEAD_DIM, (hh + 1) * DN_HEAD_DIM
        oh = o[:, lo:hi]
        ys.append(oh * lax.rsqrt(jnp.mean(oh * oh, axis=-1, keepdims=True) + EPS) * onw * _silu(z[:, lo:hi]))
    y_dn = jnp.concatenate(ys, axis=1)
    pin = pin_ref[...]
    yp = []
    for gi in range(POOL_GROUPS):
        lo, hi = gi * POOL_GW, (gi + 1) * POOL_GW
        pooled = jnp.dot(pm_ref[0, gi], pin[:, lo:hi], precision=HIGHEST, preferred_element_type=F32)
        yp.append(jnp.dot(pooled, pw_ref[gi], preferred_element_type=F32))
    y_pool = jnp.concatenate(yp, axis=1) * ps_ref[...]
    m = (jax.nn.sigmoid(gd_ref[...]) * jnp.dot(y_dn.astype(BF16), wud_ref[...], preferred_element_type=F32)
         + jax.nn.sigmoid(gp_ref[...]) * jnp.dot(y_pool.astype(BF16), wup_ref[...], preferred_element_type=F32))
    out = jnp.dot(m.astype(BF16), wo_ref[...], preferred_element_type=F32)
    o_ref[...] = x_ref[...] + gate * out


def _merge(xall, o_f, o_b, z, pin, gd, gp, modv, onorm, pool_w, pool_scale, wud, wup, wo, seq, ctx, rows_out):
    assert ctx == ROW_TILE and seq % ROW_TILE == 0
    lat_tiles = BATCH * seq // ROW_TILE
    tiles_per_seq = seq // ROW_TILE
    cpt = ROW_TILE // DN_CHUNK
    pm = jnp.asarray(np.stack([_pool_matrices(ROW_TILE, GRID_W), _pool_matrices(ROW_TILE, ctx)]))
    row = lambda n: pl.BlockSpec((ROW_TILE, n), lambda i: (i, 0))
    full = lambda *s: pl.BlockSpec(s, lambda i: (0,) * len(s))

    def o_pos(i):
        lat = i < lat_tiles
        return (jnp.where(lat, i // tiles_per_seq, i - lat_tiles), jnp.where(lat, 1 + i % tiles_per_seq, 0), 0, 0)

    o_spec = pl.BlockSpec((1, cpt, DN_CHUNK, DN_WIDTH), o_pos)
    return pl.pallas_call(
        functools.partial(_merge_kernel, seq=seq),
        grid=(rows_out // ROW_TILE,),
        in_specs=[
            row(D_MODEL), o_spec, o_spec, row(DN_WIDTH), row(POOL_WIDTH), row(D_MODEL), row(D_MODEL),
            full(8, 6 * D_MODEL), full(1, DN_HEAD_DIM),
            pl.BlockSpec((1, POOL_GROUPS, ROW_TILE, ROW_TILE), lambda i: (jnp.where(i >= lat_tiles, 1, 0), 0, 0, 0)),
            full(POOL_GROUPS, POOL_GW, POOL_GW), full(1, POOL_WIDTH),
            full(DN_WIDTH, D_MODEL), full(POOL_WIDTH, D_MODEL), full(D_MODEL, D_MODEL),
        ],
        out_specs=row(D_MODEL),
        out_shape=jax.ShapeDtypeStruct((rows_out, D_MODEL), F32),
        compiler_params=_cparams("arbitrary"),
        name="merge",
    )(xall, o_f, o_b, z, pin, gd, gp, modv, onorm.reshape(1, DN_HEAD_DIM), pm, pool_w,
      pool_scale.reshape(1, POOL_WIDTH), wud, wup, wo)


def _col_max(x):
    return jnp.max(x, axis=0, keepdims=True)


def _col_min(x):
    return jnp.min(x, axis=0, keepdims=True)


def _oddeven_merge(lo, hi, r):
    step = r * 2
    if step < hi - lo:
        yield from _oddeven_merge(lo, hi, step)
        yield from _oddeven_merge(lo + r, hi, step)
        yield from [(i, i + r) for i in range(lo + r, hi - r, step)]
    else:
        yield (lo, lo + r)


def _oddeven_merge_sort(lo, hi):
    if hi - lo >= 1:
        mid = lo + (hi - lo) // 2
        yield from _oddeven_merge_sort(lo, mid)
        yield from _oddeven_merge_sort(mid + 1, hi)
        yield from _oddeven_merge(lo, hi, 1)


def _exchange(v, i, j):
    v[i], v[j] = jnp.maximum(v[i], v[j]), jnp.minimum(v[i], v[j])


def _sorted_top(blocks):
    k = PEER_TOPK
    v = list(blocks)
    for i, j in _oddeven_merge_sort(0, k - 1):
        _exchange(v, i, j)
    shift = SUBLANES // 2
    while shift >= 1:
        w = [pltpu.roll(x, shift, 0) for x in v]
        v = [jnp.maximum(v[j], w[k - 1 - j]) for j in range(k)]
        stride = k // 2
        while stride >= 1:
            for i in range(k):
                if i & stride == 0:
                    _exchange(v, i, i + stride)
            stride //= 2
        shift //= 2
    return v


def _sublane_total(x):
    shift = SUBLANES // 2
    while shift >= 1:
        x = x + pltpu.roll(x, shift, 0)
        shift //= 2
    return x


def _unambiguous(blocks, top):
    ok = top[0] > top[1]
    for a in range(1, PEER_TOPK - 1):
        ok = ok & (top[a] > top[a + 1])
    ge = None
    for blk in blocks:
        one = jnp.where(blk >= top[PEER_TOPK - 1], 1.0, 0.0)
        ge = one if ge is None else ge + one
    return ok & (_sublane_total(ge) == float(PEER_TOPK))


def _top_select(scores, iota):
    n = float(scores[0].shape[0])
    cur = list(scores)
    idxs = [[] for _ in cur]
    for _ in range(PEER_TOPK):
        for j in range(len(cur)):
            m = _col_max(cur[j])
            cand = jnp.where(cur[j] == m, iota, n)
            idx = _col_min(cand)
            cur[j] = jnp.where(cand == idx, NEG_INF, cur[j])
            idxs[j].append(idx)
    return [jnp.concatenate(i, axis=0) for i in idxs]


def _peer_route_kernel(x_ref, modv_ref, nw_ref, wq_ref, keys_ref, hn_ref, n1_ref, e1_ref, r2_ref, e2_ref,
                       q_scr, *, seq):
    g = _group_of_row(pl.program_id(0) * ROW_TILE, seq)
    mod = modv_ref[pl.ds(g, 1), :]
    hn = _norm_mod(x_ref[...], nw_ref[...], mod, 3).astype(BF16)
    hn_ref[...] = hn
    q = jnp.dot(hn, wq_ref[...], preferred_element_type=F32)
    nsub = ROW_TILE // LANES
    for j in range(2 * PEER_HEADS):
        for t in range(nsub):
            q_scr[j, t] = q[t * LANES:(t + 1) * LANES, j * PEER_HALF:(j + 1) * PEER_HALF]
    k = PEER_TOPK
    nblk = N_KEYS // SUBLANES
    iota_k = lax.broadcasted_iota(jnp.int32, (N_KEYS, LANES), 0).astype(F32)
    iota_a = lax.broadcasted_iota(jnp.int32, (k, LANES), 0).astype(F32)
    iota_s = lax.broadcasted_iota(jnp.int32, (SUBLANES, LANES), 0).astype(F32)
    split = lambda s: [s[SUBLANES * j:SUBLANES * (j + 1), :] for j in range(nblk)]
    rep = lambda row: jnp.broadcast_to(row, (SUBLANES, LANES))

    def head(hh, carry):
        for t in range(nsub):
            lanes = slice(t * LANES, (t + 1) * LANES)
            s1 = lax.dot_general(keys_ref[2 * hh], q_scr[2 * hh, t], NT_DIMS, preferred_element_type=F32)
            s2 = lax.dot_general(keys_ref[2 * hh + 1], q_scr[2 * hh + 1, t], NT_DIMS, preferred_element_type=F32)
            b1, b2 = split(s1), split(s2)
            top1, top2 = _sorted_top(b1), _sorted_top(b2)
            c = jnp.concatenate([x[0:1, :] for x in top1], axis=0)
            d = jnp.concatenate([x[0:1, :] for x in top2], axis=0)

            cells = [c[0:1, :] + d[0:SUBLANES, :], c[0:1, :] + d[SUBLANES:k, :]]
            for a in range(1, SUBLANES):
                cells.append(jnp.where(iota_s < float(k // (a + 1)), c[a:a + 1, :] + d[0:SUBLANES, :], NEG_INF))
            cells.append(c[SUBLANES:k, :] + d[0:1, :])
            cells += [jnp.full((SUBLANES, LANES), NEG_INF, F32)] * (k - len(cells))
            cut = _sorted_top(cells)[k - 1][0:1, :]
            cnt_cut = jnp.zeros((k, LANES), F32)
            for b in range(k):
                cnt_cut = cnt_cut + jnp.where(c + d[b:b + 1, :] >= cut, 1.0, 0.0)

            def by_merge():
                cnt = jnp.zeros((k, LANES), F32)
                for _ in range(k):
                    dn = jnp.full((k, LANES), NEG_INF, F32)
                    for b in range(k):
                        dn = jnp.where(cnt == float(b), d[b:b + 1, :], dn)
                    f = c + dn
                    cand = jnp.where(f == _col_max(f), iota_a, float(k))
                    cnt = cnt + (cand == _col_min(cand)).astype(F32)
                return cnt

            exact_cut = jnp.all(jnp.sum(cnt_cut, axis=0, keepdims=True) == float(k))
            cnt = lax.cond(exact_cut, lambda: cnt_cut, by_merge)
            e1c = jnp.exp(c - c[0:1, :])
            e2d = jnp.exp(d - d[0:1, :])
            part = jnp.zeros((k, LANES), F32)
            for b in range(k):
                part = part + jnp.where(cnt > float(b), e2d[b:b + 1, :], 0.0)
            zsum = jnp.sum(e1c * part, axis=0, keepdims=True)

            def by_value():
                n1b = [jnp.zeros((SUBLANES, LANES), F32)] * nblk
                r2b = [jnp.full((SUBLANES, LANES), float(k), F32)] * nblk
                for a in range(k):
                    cnt_a = rep(cnt[a:a + 1, :])
                    n1b = [jnp.where(blk == top1[a], cnt_a, acc) for blk, acc in zip(b1, n1b)]
                    r2b = [jnp.where(blk == top2[a], float(a), acc) for blk, acc in zip(b2, r2b)]
                return jnp.concatenate(n1b, axis=0), jnp.concatenate(r2b, axis=0)

            def by_index():
                idx1, idx2 = _top_select([s1, s2], iota_k)
                n1 = jnp.zeros((N_KEYS, LANES), F32)
                r2 = jnp.full((N_KEYS, LANES), float(k), F32)
                for a in range(k):
                    n1 = jnp.where(iota_k == idx1[a:a + 1, :], cnt[a:a + 1, :], n1)
                    r2 = jnp.where(iota_k == idx2[a:a + 1, :], float(a), r2)
                return n1, r2

            clean = jnp.all(_unambiguous(b1, top1) & _unambiguous(b2, top2))
            n1, r2 = lax.cond(clean, by_value, by_index)
            n1_ref[hh, 0, :, lanes] = _bf16_pair_word(n1)
            e1_ref[hh, 0, :, lanes] = _bf16_pair_word(0.5 * jnp.exp(s1 - c[0:1, :]) / zsum)
            r2_ref[hh, 0, :, lanes] = r2.astype(BF16)
            e2_ref[hh, 0, :, lanes] = jnp.exp(s2 - d[0:1, :]).astype(BF16)
        return carry

    lax.fori_loop(0, PEER_HEADS, head, 0)


def _peer_route(xall, modv, norm_w, wq, keys, seq, rows_out):
    row = lambda n: pl.BlockSpec((ROW_TILE, n), lambda i: (i, 0))
    full = lambda *s: pl.BlockSpec(s, lambda i: (0,) * len(s))
    tab = pl.BlockSpec((PEER_HEADS, 1, N_KEYS, ROW_TILE), lambda i: (0, i, 0, 0))
    tab_shape = lambda dt: jax.ShapeDtypeStruct((PEER_HEADS, rows_out // ROW_TILE, N_KEYS, ROW_TILE), dt)
    return pl.pallas_call(
        functools.partial(_peer_route_kernel, seq=seq),
        grid=(rows_out // ROW_TILE,),
        in_specs=[row(D_MODEL), full(8, 6 * D_MODEL), full(1, D_MODEL), full(D_MODEL, PEER_HEADS * PEER_QDIM),
                  full(2 * PEER_HEADS, N_KEYS, PEER_HALF)],
        out_specs=[row(D_MODEL), tab, tab, tab, tab],
        out_shape=[jax.ShapeDtypeStruct((rows_out, D_MODEL), BF16), tab_shape(jnp.uint32), tab_shape(jnp.uint32),
                   tab_shape(BF16), tab_shape(BF16)],
        scratch_shapes=[pltpu.VMEM((2 * PEER_HEADS, ROW_TILE // LANES, LANES, PEER_HALF), F32)],
        compiler_params=_cparams("arbitrary"),
        name="peer_route",
    )(xall, modv, norm_w.reshape(1, D_MODEL), wq, keys.reshape(2 * PEER_HEADS, N_KEYS, PEER_HALF))


PEER_PIPE_LAG = 2
PEER_MXU_BLOCK = 256


def _peer_tile(n, lag, n_tiles, n_exp):
    m = jnp.clip(n - lag, 0, n_tiles - 1)
    return m // n_exp, m % n_exp


def _peer_dense_kernel(x_ref, modv_ref, hn_ref, u_ref, vt_ref, n1_ref, e1_ref, r2_ref, e2_ref, fn_ref, o_ref,
                       acc_ref, a0_ref, a1_ref, act0_ref, act1_ref, hn_scr, *, seq, final, n_tiles, n_exp):
    n = pl.program_id(0)
    t_c, e_c = _peer_tile(n, PEER_PIPE_LAG, n_tiles, n_exp)

    @pl.when(n == 0)
    def _():
        for ref in (acc_ref, a0_ref, a1_ref, act0_ref, act1_ref):
            ref[...] = jnp.zeros_like(ref)

    @pl.when(_peer_tile(n, 0, n_tiles, n_exp)[1] == 0)
    def _():
        hn_scr[...] = hn_ref[...]

    def step(a_next, a_cur, act_next, act_cur):
        zero = jnp.zeros((), BF16)
        blk = PEER_MXU_BLOCK

        def stage_a(rb, cb):
            rows, cols = slice(rb * blk, (rb + 1) * blk), slice(cb * blk, (cb + 1) * blk)
            a_next[rows, cols] = lax.dot_general(u_ref[rows, :], hn_scr[cols, :], NT_DIMS,
                                                 preferred_element_type=F32)

        def stage_b(t, i):
            lanes = slice(t * ROW_TILE, (t + 1) * ROW_TILE)
            wsum = None
            for hh in range(PEER_HEADS):
                keep = r2_ref[hh, t] < _pair_word_rows(n1_ref[hh, t, i:i + 1, :])
                term = jnp.where(keep, e2_ref[hh, t], zero) * _pair_word_rows(e1_ref[hh, t, i:i + 1, :])
                wsum = term if wsum is None else wsum + term
            rows = slice(i * N_KEYS, (i + 1) * N_KEYS)
            a = a_cur[rows, lanes]
            gelu = a * (1.0 + lax.erf(a * (2.0 ** -0.5)))
            act_next[rows, lanes] = gelu.astype(BF16) * wsum

        def stage_c(rb, cb):
            rows, cols = slice(rb * blk, (rb + 1) * blk), slice(cb * blk, (cb + 1) * blk)
            prev = jnp.where(e_c == 0, 0.0, acc_ref[rows, cols])
            acc_ref[rows, cols] = prev + jnp.dot(vt_ref[rows, :], act_cur[:, cols], preferred_element_type=F32)

        a_pieces = [(rb, cb) for cb in range(PEER_TOK_TILE // blk) for rb in range(PEER_EXP_TILE // blk)]
        c_pieces = [(rb, cb) for cb in range(PEER_TOK_TILE // blk) for rb in range(D_MODEL // blk)]
        b_pieces = [(t, i) for t in range(PEER_TOK_TILE // ROW_TILE) for i in range(PEER_EXP_TILE // N_KEYS)]
        per_mxu = len(b_pieces) // (len(a_pieces) + len(c_pieces))
        mxu_pieces = [x for pair in zip([(stage_a, p) for p in a_pieces], [(stage_c, p) for p in c_pieces])
                      for x in pair]
        for k, (fn, piece) in enumerate(mxu_pieces):
            fn(*piece)
            for t, i in b_pieces[k * per_mxu:(k + 1) * per_mxu]:
                stage_b(t, i)

    step(a0_ref, a1_ref, act0_ref, act1_ref)
    a1_ref[...] = a0_ref[...]

    @pl.when(n >= 0)
    def _():
        act1_ref[...] = act0_ref[...]

    @pl.when((e_c == n_exp - 1) & (n >= PEER_PIPE_LAG))
    def _():
        g = _group_of_row(t_c * PEER_TOK_TILE, seq)
        mod = modv_ref[pl.ds(g, 1), :]
        y = x_ref[...] + mod[:, 5 * D_MODEL:6 * D_MODEL] * acc_ref[...].T
        if final:
            y = y * lax.rsqrt(jnp.mean(y * y, axis=-1, keepdims=True) + EPS) * fn_ref[...]
        o_ref[...] = y


def _peer_dense(xall, modv, hn, u_bf, vt_bf, n1, e1, r2, e2, final_norm, seq, rows_out, final):
    tt, et = PEER_TOK_TILE, PEER_EXP_TILE
    n_first = et // N_KEYS
    n_exp = N_EXPERTS // et
    n_tiles = (rows_out // tt) * n_exp
    tile = lambda lag: (lambda n: _peer_tile(n, lag, n_tiles, n_exp))
    ta, tb, tc = tile(0), tile(1), tile(PEER_PIPE_LAG)
    full = lambda *s: pl.BlockSpec(s, lambda n: (0,) * len(s))
    per_first = pl.BlockSpec((PEER_HEADS, tt // ROW_TILE, n_first, ROW_TILE), lambda n: (0, tb(n)[0], tb(n)[1], 0))
    per_second = pl.BlockSpec((PEER_HEADS, tt // ROW_TILE, N_KEYS, ROW_TILE), lambda n: (0, tb(n)[0], 0, 0))
    return pl.pallas_call(
        functools.partial(_peer_dense_kernel, seq=seq, final=final, n_tiles=n_tiles, n_exp=n_exp),
        grid=(n_tiles + PEER_PIPE_LAG,),
        in_specs=[pl.BlockSpec((tt, D_MODEL), lambda n: (tc(n)[0], 0)),
                  full(8, 6 * D_MODEL),
                  pl.BlockSpec((tt, D_MODEL), lambda n: (ta(n)[0], 0)),
                  pl.BlockSpec((et, D_MODEL), lambda n: (ta(n)[1], 0)),
                  pl.BlockSpec((D_MODEL, et), lambda n: (0, tc(n)[1])),
                  per_first, per_first, per_second, per_second, full(1, D_MODEL)],
        out_specs=pl.BlockSpec((tt, D_MODEL), lambda n: (tc(n)[0], 0)),
        out_shape=jax.ShapeDtypeStruct((rows_out, D_MODEL), F32),
        scratch_shapes=[pltpu.VMEM((D_MODEL, tt), F32), pltpu.VMEM((et, tt), F32), pltpu.VMEM((et, tt), F32),
                        pltpu.VMEM((et, tt), BF16), pltpu.VMEM((et, tt), BF16), pltpu.VMEM((tt, D_MODEL), BF16)],
        compiler_params=_cparams("arbitrary"),
        name="peer_dense",
    )(xall, modv, hn, u_bf, vt_bf, n1, e1, r2, e2, final_norm.reshape(1, D_MODEL))


def _reorder_in_weight(w):
    s = np.cumsum((0, 3 * DN_WIDTH, DN_WIDTH, POOL_WIDTH, 2 * DN_HEADS, 2 * DN_HEADS, D_MODEL, D_MODEL))
    qkv, z, pin, b, a, gd, gp = (w[:, s[i]:s[i + 1]] for i in range(7))
    pad = jnp.zeros((w.shape[0], BA_PAD - 4 * DN_HEADS), w.dtype)
    return jnp.concatenate([qkv, z, pin, gd, gp, b, a, pad], axis=1).astype(BF16)


def _forward(x, c, ctx, c_ctx, w_mod, b_mod, norm_mix, w_in, conv_w, a_log, dt_bias, dn_out_norm, pool_w, pool_scale,
             w_up_dn, w_up_pool, w_out, norm_ffn, peer_wq, peer_keys, peer_u, peer_v, final_norm):
    seq, nctx = x.shape[1], ctx.shape[1]
    nlat = BATCH * seq
    xall = jnp.concatenate([x.reshape(nlat, D_MODEL), ctx.reshape(BATCH * nctx, D_MODEL)], axis=0)
    rows = xall.shape[0]
    cvec = jnp.concatenate([c, c_ctx[None, :], jnp.zeros((8 - BATCH - 1, D_MODEL), F32)], axis=0)
    modv_all = _modulation(cvec, w_mod, b_mod)
    for i in range(DEPTH):
        last = i == DEPTH - 1
        rows_out = nlat if last else rows
        modv = modv_all[i]
        qkv, z, pin, gd, gp, ba = _inproj(xall, modv, norm_mix[i], _reorder_in_weight(w_in[i]), seq)
        q, k, v, bg = _dnconv(qkv, ba, conv_w[i], a_log[i], dt_bias[i], seq, nctx)
        o_f, o_b = _dnscan(*_dnprep(q, k, v, bg, seq, nctx), seq, nctx)
        xall = _merge(xall, o_f, o_b, z, pin, gd, gp, modv, dn_out_norm[i], pool_w[i], pool_scale[i],
                      w_up_dn[i].astype(BF16), w_up_pool[i].astype(BF16), w_out[i].astype(BF16), seq, nctx, rows_out)
        hn, n1, e1, r2, e2 = _peer_route(xall, modv, norm_ffn[i], peer_wq[i].astype(BF16), peer_keys[i], seq, rows_out)
        xall = _peer_dense(xall, modv, hn, peer_u[i].astype(BF16), peer_v[i].T.astype(BF16), n1, e1, r2, e2,
                           final_norm, seq, rows_out, last)
    return xall.reshape(BATCH, seq, D_MODEL)


def kernel(x, c, ctx, c_ctx, w_mod, b_mod, norm_mix, w_in, conv_w, a_log, dt_bias, dn_out_norm, pool_w, pool_scale, w_up_dn, w_up_pool, w_out, norm_ffn, peer_wq, peer_keys, peer_u, peer_v, final_norm):
    return _forward(x, c, ctx, c_ctx, w_mod, b_mod, norm_mix, w_in, conv_w, a_log, dt_bias, dn_out_norm, pool_w,
                    pool_scale, w_up_dn, w_up_pool, w_out, norm_ffn, peer_wq, peer_keys, peer_u, peer_v, final_norm)
```

```python
import functools

import numpy as np
import jax
import jax.numpy as jnp
from jax import lax
from jax.experimental import pallas as pl
from jax.experimental.pallas import tpu as pltpu

D_MODEL = 1024
BATCH = 2
DEPTH = 2
GRID_W = 64
EPS = 1e-6

DN_HEADS = 4
DN_HEAD_DIM = 128
DN_WIDTH = DN_HEADS * DN_HEAD_DIM
SHORT_CONV = 4
DN_CHUNK = 64
DN_CHAINS = 2 * DN_HEADS
DN_PREP_CHUNKS = 2
DN_LOCKSTEP = 16

POOL_WINDOWS = (2, 4, 8, 16)
POOL_GROUPS = 4
POOL_WIDTH = D_MODEL // 2
POOL_GW = POOL_WIDTH // POOL_GROUPS

PEER_HEADS = 8
N_KEYS = 128
N_EXPERTS = N_KEYS * N_KEYS
PEER_TOPK = 16
PEER_QDIM = 256
PEER_HALF = PEER_QDIM // 2

BA_PAD = 128
IN_COLS_R = 3 * DN_WIDTH + DN_WIDTH + POOL_WIDTH + 2 * D_MODEL + BA_PAD

LANES = 128
SUBLANES = 8
ROW_TILE = 256
PEER_TOK_TILE = 512
PEER_EXP_TILE = 1024
PEER_SLAB = 256
HALO = 8
VMEM_LIMIT = 56 * 1024 * 1024

F32 = jnp.float32
BF16 = jnp.bfloat16
HIGHEST = lax.Precision.HIGHEST
NEG_INF = float("-inf")
NT_DIMS = (((1,), (1,)), ((), ()))


def _cparams(*sem):
    return pltpu.CompilerParams(dimension_semantics=sem, vmem_limit_bytes=VMEM_LIMIT)


def _group_of_row(row0, seq):
    return jnp.where(row0 < seq, 0, jnp.where(row0 < 2 * seq, 1, 2))


def _silu(x):
    return x * jax.nn.sigmoid(x)


def _split_bf16(a):
    hi = a.astype(BF16)
    lo = (a - hi.astype(F32)).astype(BF16)
    return hi, lo


def _bf16_pair_word(x):
    hi = lax.bitcast_convert_type(x.astype(BF16).astype(F32), jnp.uint32)
    return hi | (hi >> 16)


def _pair_word_rows(row):
    tile = pltpu.bitcast(jnp.broadcast_to(row, (SUBLANES, row.shape[1])), BF16)
    return jnp.concatenate([tile] * (N_KEYS // tile.shape[0]), axis=0)


def _dot_split(a, b):
    ah, al = _split_bf16(a)
    bh, bl = _split_bf16(b)
    a4 = jnp.concatenate([ah, al, ah, al], axis=1)
    b4 = jnp.concatenate([bh, bh, bl, bl], axis=0)
    return jnp.dot(a4, b4, preferred_element_type=F32)


def _mod_kernel(c_ref, w_ref, b_ref, o_ref):
    o_ref[0] = jnp.dot(_silu(c_ref[...]), w_ref[0], preferred_element_type=F32) + b_ref[0]


def _modulation(cvec, w_mod, b_mod):
    tn = 1536
    return pl.pallas_call(
        _mod_kernel,
        grid=(DEPTH, 6 * D_MODEL // tn),
        in_specs=[
            pl.BlockSpec((8, D_MODEL), lambda l, j: (0, 0)),
            pl.BlockSpec((1, D_MODEL, tn), lambda l, j: (l, 0, j)),
            pl.BlockSpec((1, 1, tn), lambda l, j: (l, 0, j)),
        ],
        out_specs=pl.BlockSpec((1, 8, tn), lambda l, j: (l, 0, j)),
        out_shape=jax.ShapeDtypeStruct((DEPTH, 8, 6 * D_MODEL), F32),
        compiler_params=_cparams("arbitrary", "arbitrary"),
        name="modulation",
    )(cvec, w_mod, b_mod.reshape(DEPTH, 1, 6 * D_MODEL))


def _norm_mod(x, nw, mod, k):
    ms = jnp.mean(x * x, axis=-1, keepdims=True)
    xn = x * lax.rsqrt(ms + EPS) * nw
    sh = mod[:, k * D_MODEL:(k + 1) * D_MODEL]
    sc = mod[:, (k + 1) * D_MODEL:(k + 2) * D_MODEL]
    return xn * (1 + sc) + sh


def _inproj_kernel(x_ref, modv_ref, nw_ref, w_ref, qkv_ref, z_ref, pin_ref, gd_ref, gp_ref, ba_ref, *, seq):
    g = _group_of_row(pl.program_id(0) * ROW_TILE, seq)
    mod = modv_ref[pl.ds(g, 1), :]
    h = _norm_mod(x_ref[...], nw_ref[...], mod, 0)
    y = jnp.dot(h.astype(BF16), w_ref[...], preferred_element_type=F32)
    o = 0
    for ref in (qkv_ref, z_ref, pin_ref, gd_ref, gp_ref, ba_ref):
        n = ref.shape[1]
        ref[...] = y[:, o:o + n]
        o += n


def _inproj(xall, modv, norm_w, w_in_r, seq):
    rows = xall.shape[0]
    widths = (3 * DN_WIDTH, DN_WIDTH, POOL_WIDTH, D_MODEL, D_MODEL, BA_PAD)
    return pl.pallas_call(
        functools.partial(_inproj_kernel, seq=seq),
        grid=(rows // ROW_TILE,),
        in_specs=[
            pl.BlockSpec((ROW_TILE, D_MODEL), lambda i: (i, 0)),
            pl.BlockSpec((8, 6 * D_MODEL), lambda i: (0, 0)),
            pl.BlockSpec((1, D_MODEL), lambda i: (0, 0)),
            pl.BlockSpec((D_MODEL, IN_COLS_R), lambda i: (0, 0)),
        ],
        out_specs=[pl.BlockSpec((ROW_TILE, n), lambda i: (i, 0)) for n in widths],
        out_shape=[jax.ShapeDtypeStruct((rows, n), F32) for n in widths],
        compiler_params=_cparams("arbitrary"),
        name="inproj",
    )(xall, modv, norm_w.reshape(1, D_MODEL), w_in_r)


def _dnconv_kernel(cur_ref, prev_ref, next_ref, cw_ref, ba_ref, alog_ref, dtb_ref,
                   q_ref, k_ref, v_ref, bg_ref, *, seq, ctx):
    row0 = pl.program_id(0) * ROW_TILE
    nlat = BATCH * seq
    is_start = (row0 == 0) | (row0 == seq) | (row0 == nlat) | (row0 == nlat + ctx)
    row1 = row0 + ROW_TILE
    is_end = (row1 == seq) | (row1 == nlat) | (row1 == nlat + ctx) | (row1 == nlat + BATCH * ctx)
    prev = jnp.where(is_start, 0.0, prev_ref[...])
    nxt = jnp.where(is_end, 0.0, next_ref[...])
    ext = jnp.concatenate([prev, cur_ref[...], nxt], axis=0)
    left = SHORT_CONV // 2
    cw = cw_ref[...]
    y = None
    for j in range(SHORT_CONV):
        o = HALO - left + j
        term = ext[o:o + ROW_TILE, :] * cw[j:j + 1, :]
        y = term if y is None else y + term
    y = _silu(y)
    for hh in range(DN_HEADS):
        lo, hi = hh * DN_HEAD_DIM, (hh + 1) * DN_HEAD_DIM
        qh = y[:, lo:hi]
        kh = y[:, DN_WIDTH + lo:DN_WIDTH + hi]
        q_ref[:, lo:hi] = qh * lax.rsqrt(jnp.sum(qh * qh, axis=-1, keepdims=True) + EPS) * (DN_HEAD_DIM ** -0.5)
        k_ref[:, lo:hi] = kh * lax.rsqrt(jnp.sum(kh * kh, axis=-1, keepdims=True) + EPS)
    v_ref[...] = y[:, 2 * DN_WIDTH:]
    ba = ba_ref[...]
    beta = jax.nn.sigmoid(ba)
    xs = ba + dtb_ref[...]
    softplus = jnp.maximum(xs, 0.0) + jnp.log(1.0 + jnp.exp(-jnp.abs(xs)))
    gdec = -jnp.exp(alog_ref[...]) * softplus
    col = lax.broadcasted_iota(jnp.int32, ba.shape, 1)
    bg_ref[...] = jnp.where(col < 2 * DN_HEADS, beta, jnp.where(col < 4 * DN_HEADS, gdec, 0.0))


def _dnconv(qkv, ba, conv_w, a_log, dt_bias, seq, ctx):
    rows = qkv.shape[0]
    nh = ROW_TILE // HALO
    last = rows // HALO - 1
    pad = jnp.zeros((2 * DN_HEADS,), F32)
    tail = jnp.zeros((BA_PAD - 4 * DN_HEADS,), F32)
    alog = jnp.concatenate([pad, a_log.reshape(-1), tail]).reshape(1, BA_PAD)
    dtb = jnp.concatenate([pad, dt_bias.reshape(-1), tail]).reshape(1, BA_PAD)
    w3 = 3 * DN_WIDTH
    return pl.pallas_call(
        functools.partial(_dnconv_kernel, seq=seq, ctx=ctx),
        grid=(rows // ROW_TILE,),
        in_specs=[
            pl.BlockSpec((ROW_TILE, w3), lambda i: (i, 0)),
            pl.BlockSpec((HALO, w3), lambda i: (jnp.maximum(i * nh - 1, 0), 0)),
            pl.BlockSpec((HALO, w3), lambda i: (jnp.minimum((i + 1) * nh, last), 0)),
            pl.BlockSpec((SHORT_CONV, w3), lambda i: (0, 0)),
            pl.BlockSpec((ROW_TILE, BA_PAD), lambda i: (i, 0)),
            pl.BlockSpec((1, BA_PAD), lambda i: (0, 0)),
            pl.BlockSpec((1, BA_PAD), lambda i: (0, 0)),
        ],
        out_specs=[pl.BlockSpec((ROW_TILE, DN_WIDTH), lambda i: (i, 0))] * 3
        + [pl.BlockSpec((ROW_TILE, BA_PAD), lambda i: (i, 0))],
        out_shape=[jax.ShapeDtypeStruct((rows, DN_WIDTH), F32)] * 3
        + [jax.ShapeDtypeStruct((rows, BA_PAD), F32)],
        compiler_params=_cparams("arbitrary"),
        name="dnconv",
    )(qkv, qkv, qkv, conv_w, ba, alog, dtb)


def _dnprep_kernel(q_ref, k_ref, v_ref, bg_ref, bgt_ref, wq_ref, u_ref, lk_ref, eg_ref):
    c = DN_CHUNK
    ri = lax.broadcasted_iota(jnp.int32, (c, c), 0)
    ci = lax.broadcasted_iota(jnp.int32, (c, c), 1)
    dirs = ((ri >= ci, ri > ci, ci >= ri, c - 1), (ri <= ci, ri < ci, ci <= ri, 0))
    chains = []
    for j in range(DN_PREP_CHUNKS):
        rows = slice(j * c, (j + 1) * c)
        bg = bg_ref[rows, :]
        bgh, bgl = _split_bf16(bg)
        bgth, bgtl = _split_bf16(bgt_ref[j])
        bg2 = jnp.concatenate([bgh, bgl], axis=0)
        bgt2 = jnp.concatenate([bgth, bgtl], axis=1)
        egs = []
        for d, (incl, strict, incl_t, last) in enumerate(dirs):
            m = incl.astype(BF16)
            mt = incl_t.astype(BF16)
            gc_all = jnp.dot(jnp.concatenate([m, m], axis=1), bg2, preferred_element_type=F32)
            gr_all = jnp.dot(bgt2, jnp.concatenate([mt, mt], axis=0), preferred_element_type=F32)
            for hh in range(DN_HEADS):
                lo, hi = hh * DN_HEAD_DIM, (hh + 1) * DN_HEAD_DIM
                ch = d * DN_HEADS + hh
                gcol = 2 * DN_HEADS + ch
                q = q_ref[rows, lo:hi]
                k = k_ref[rows, lo:hi]
                v = v_ref[rows, lo:hi]
                beta = bg[:, ch:ch + 1]
                gc = gc_all[:, gcol:gcol + 1]
                gr = gr_all[gcol:gcol + 1, :]
                glast = gc_all[last:last + 1, gcol:gcol + 1]
                decay = jnp.where(incl, jnp.exp(jnp.minimum(gc - gr, 0.0)), 0.0)
                kk = lax.dot_general(k, k, NT_DIMS, preferred_element_type=F32)
                qk = lax.dot_general(q, k, NT_DIMS, preferred_element_type=F32)
                egc = jnp.exp(gc)
                cols = slice(ch * DN_HEAD_DIM, (ch + 1) * DN_HEAD_DIM)
                wq_ref[0, j, c:2 * c, cols] = (q * egc).astype(BF16)
                lk_ref[0, j, ch, 0:c, :] = jnp.where(incl, qk * decay, 0.0).astype(BF16)
                lk_ref[0, j, ch, c:, :] = (k * jnp.exp(glast - gc)).T.astype(BF16)
                egs.append(jnp.broadcast_to(jnp.exp(glast), (1, LANES)))
                chains.append((j, cols, jnp.where(strict, -(beta * kk * decay), 0.0),
                               jnp.concatenate([k * (beta * egc), v * beta], axis=1)))
        eg_ref[0, j] = jnp.concatenate(egs, axis=0)
    for g0 in range(0, len(chains), DN_LOCKSTEP):
        group = chains[g0:g0 + DN_LOCKSTEP]
        ps = [ch[2] for ch in group]
        ys = [ch[3] for ch in group]
        ys = [y + _dot_split(p, y) for p, y in zip(ps, ys)]
        for _ in range(int(np.log2(c)) - 1):
            ps = [_dot_split(p, p) for p in ps]
            ys = [y + _dot_split(p, y) for p, y in zip(ps, ys)]
        for (j, cols, _, _), y in zip(group, ys):
            wq_ref[0, j, 0:c, cols] = y[:, :DN_HEAD_DIM].astype(BF16)
            u_ref[0, j, :, cols] = y[:, DN_HEAD_DIM:]


def _dnprep(q, k, v, bg, seq, ctx):
    rows = q.shape[0]
    c = DN_CHUNK
    n = DN_PREP_CHUNKS
    nct, nlt = ctx // c, seq // c
    assert nct % n == 0 and nlt % n == 0
    nch = nct + nlt
    bgt = bg[:, :4 * DN_HEADS].reshape(rows // c, c, 4 * DN_HEADS).transpose(0, 2, 1)

    def seq_pos(i):
        i = i * n
        lat = i < BATCH * nlt
        j = i - BATCH * nlt
        return jnp.where(lat, i // nlt, j // nct), jnp.where(lat, nct + i % nlt, j % nct) // n

    def out_spec(*tail):
        return pl.BlockSpec((1, n) + tail, lambda i: seq_pos(i) + (0,) * len(tail))

    wide = pl.BlockSpec((n * c, DN_WIDTH), lambda i: (i, 0))
    width = DN_CHAINS * DN_HEAD_DIM
    return pl.pallas_call(
        _dnprep_kernel,
        grid=(rows // (n * c),),
        in_specs=[wide, wide, wide,
                  pl.BlockSpec((n * c, BA_PAD), lambda i: (i, 0)),
                  pl.BlockSpec((n, 4 * DN_HEADS, c), lambda i: (i, 0, 0))],
        out_specs=[out_spec(2 * c, width), out_spec(c, width), out_spec(DN_CHAINS, c + DN_HEAD_DIM, c),
                   out_spec(DN_CHAINS, LANES)],
        out_shape=[jax.ShapeDtypeStruct((BATCH, nch, 2 * c, width), BF16),
                   jax.ShapeDtypeStruct((BATCH, nch, c, width), F32),
                   jax.ShapeDtypeStruct((BATCH, nch, DN_CHAINS, c + DN_HEAD_DIM, c), BF16),
                   jax.ShapeDtypeStruct((BATCH, nch, DN_CHAINS, LANES), F32)],
        compiler_params=_cparams("arbitrary"),
        name="dnprep",
    )(q, k, v, bg, bgt)


def _dnscan_kernel(wqf_ref, uf_ref, lkf_ref, egf_ref, wqb_ref, ub_ref, lkb_ref, egb_ref, of_ref, ob_ref, s_ref):
    @pl.when(pl.program_id(0) == 0)
    def _():
        s_ref[...] = jnp.zeros_like(s_ref)

    c = DN_CHUNK
    dirs = ((wqf_ref, uf_ref, lkf_ref, egf_ref, of_ref), (wqb_ref, ub_ref, lkb_ref, egb_ref, ob_ref))
    chains = [(d, b, hh) for d in range(2) for b in range(BATCH) for hh in range(DN_HEADS)]
    sidx = lambda d, b, hh: (b * 2 + d) * DN_HEADS + hh
    cols = lambda hh: slice(hh * DN_HEAD_DIM, (hh + 1) * DN_HEAD_DIM)
    ss = [s_ref[sidx(*ch)] for ch in chains]
    r1 = [jnp.dot(dirs[d][0][b, 0, :, cols(hh)], s.astype(BF16), preferred_element_type=F32)
          for (d, b, hh), s in zip(chains, ss)]
    vn = [dirs[d][1][b, 0, :, cols(hh)] - r[:c] for (d, b, hh), r in zip(chains, r1)]
    r2 = [jnp.dot(dirs[d][2][b, 0, hh], v.astype(BF16), preferred_element_type=F32)
          for (d, b, hh), v in zip(chains, vn)]
    for (d, b, hh), s, a1, a2 in zip(chains, ss, r1, r2):
        dirs[d][4][b, 0, :, cols(hh)] = a1[c:] + a2[:c]
        row = d * DN_HEADS + hh
        s_ref[sidx(d, b, hh)] = s * dirs[d][3][b, 0, row:row + 1, :] + a2[c:]


def _dnscan(wq, u, lk, eg, seq, ctx):
    c = DN_CHUNK
    nct, nlt = ctx // c, seq // c
    nch = nct + nlt

    def bwd(s):
        return jnp.where(s < nct, nct - 1 - s, 2 * nct + nlt - 1 - s)

    def specs(pos, d):
        return [pl.BlockSpec((BATCH, 1, 2 * c, DN_WIDTH), lambda s: (0, pos(s), 0, d)),
                pl.BlockSpec((BATCH, 1, c, DN_WIDTH), lambda s: (0, pos(s), 0, d)),
                pl.BlockSpec((BATCH, 1, DN_HEADS, c + DN_HEAD_DIM, c), lambda s: (0, pos(s), d, 0, 0)),
                pl.BlockSpec((BATCH, 1, DN_CHAINS, LANES), lambda s: (0, pos(s), 0, 0))]

    fwd = lambda s: s
    return pl.pallas_call(
        _dnscan_kernel,
        grid=(nch,),
        in_specs=specs(fwd, 0) + specs(bwd, 1),
        out_specs=[pl.BlockSpec((BATCH, 1, c, DN_WIDTH), lambda s: (0, s, 0, 0)),
                   pl.BlockSpec((BATCH, 1, c, DN_WIDTH), lambda s: (0, bwd(s), 0, 0))],
        out_shape=[jax.ShapeDtypeStruct((BATCH, nch, c, DN_WIDTH), F32)] * 2,
        scratch_shapes=[pltpu.VMEM((BATCH * DN_CHAINS, DN_HEAD_DIM, DN_HEAD_DIM), F32)],
        compiler_params=_cparams("arbitrary"),
        name="dnscan",
    )(wq, u, lk, eg, wq, u, lk, eg)


def _pool_matrices(tile, seg):
    t = np.arange(tile)
    p = t % seg
    mats = []
    for win in POOL_WINDOWS:
        lo = np.clip(p - win // 2, 0, seg)
        hi = np.clip(p + win - win // 2, 0, seg)
        same = (t[:, None] // seg) == (t[None, :] // seg)
        inside = same & (p[None, :] >= lo[:, None]) & (p[None, :] < hi[:, None])
        mats.append(inside / (hi - lo)[:, None].astype(np.float64) - np.eye(tile))
    return np.stack(mats).astype(np.float32)


def _merge_kernel(x_ref, of_ref, ob_ref, z_ref, pin_ref, gd_ref, gp_ref, modv_ref, on_ref, pm_ref, pw_ref, ps_ref,
                  wud_ref, wup_ref, wo_ref, o_ref, *, seq):
    g = _group_of_row(pl.program_id(0) * ROW_TILE, seq)
    mod = modv_ref[pl.ds(g, 1), :]
    gate = mod[:, 2 * D_MODEL:3 * D_MODEL]
    o = (of_ref[0] + ob_ref[0]).reshape(ROW_TILE, DN_WIDTH)
    z = z_ref[...]
    onw = on_ref[...]
    ys = []
    for hh in range(DN_HEADS):
        lo, hi = hh * DN_HEAD_DIM, (hh + 1) * DN_HEAD_DIM
        oh = o[:, lo:hi]
        ys.append(oh * lax.rsqrt(jnp.mean(oh * oh, axis=-1, keepdims=True) + EPS) * onw * _silu(z[:, lo:hi]))
    y_dn = jnp.concatenate(ys, axis=1)
    pin = pin_ref[...]
    yp = []
    for gi in range(POOL_GROUPS):
        lo, hi = gi * POOL_GW, (gi + 1) * POOL_GW
        pooled = jnp.dot(pm_ref[0, gi], pin[:, lo:hi], precision=HIGHEST, preferred_element_type=F32)
        yp.append(jnp.dot(pooled, pw_ref[gi], preferred_element_type=F32))
    y_pool = jnp.concatenate(yp, axis=1) * ps_ref[...]
    m = (jax.nn.sigmoid(gd_ref[...]) * jnp.dot(y_dn.astype(BF16), wud_ref[...], preferred_element_type=F32)
         + jax.nn.sigmoid(gp_ref[...]) * jnp.dot(y_pool.astype(BF16), wup_ref[...], preferred_element_type=F32))
    out = jnp.dot(m.astype(BF16), wo_ref[...], preferred_element_type=F32)
    o_ref[...] = x_ref[...] + gate * out


def _merge(xall, o_f, o_b, z, pin, gd, gp, modv, onorm, pool_w, pool_scale, wud, wup, wo, seq, ctx, rows_out):
    assert ctx == ROW_TILE and seq % ROW_TILE == 0
    lat_tiles = BATCH * seq // ROW_TILE
    tiles_per_seq = seq // ROW_TILE
    cpt = ROW_TILE // DN_CHUNK
    pm = jnp.asarray(np.stack([_pool_matrices(ROW_TILE, GRID_W), _pool_matrices(ROW_TILE, ctx)]))
    row = lambda n: pl.BlockSpec((ROW_TILE, n), lambda i: (i, 0))
    full = lambda *s: pl.BlockSpec(s, lambda i: (0,) * len(s))

    def o_pos(i):
        lat = i < lat_tiles
        return (jnp.where(lat, i // tiles_per_seq, i - lat_tiles), jnp.where(lat, 1 + i % tiles_per_seq, 0), 0, 0)

    o_spec = pl.BlockSpec((1, cpt, DN_CHUNK, DN_WIDTH), o_pos)
    return pl.pallas_call(
        functools.partial(_merge_kernel, seq=seq),
        grid=(rows_out // ROW_TILE,),
        in_specs=[
            row(D_MODEL), o_spec, o_spec, row(DN_WIDTH), row(POOL_WIDTH), row(D_MODEL), row(D_MODEL),
            full(8, 6 * D_MODEL), full(1, DN_HEAD_DIM),
            pl.BlockSpec((1, POOL_GROUPS, ROW_TILE, ROW_TILE), lambda i: (jnp.where(i >= lat_tiles, 1, 0), 0, 0, 0)),
            full(POOL_GROUPS, POOL_GW, POOL_GW), full(1, POOL_WIDTH),
            full(DN_WIDTH, D_MODEL), full(POOL_WIDTH, D_MODEL), full(D_MODEL, D_MODEL),
        ],
        out_specs=row(D_MODEL),
        out_shape=jax.ShapeDtypeStruct((rows_out, D_MODEL), F32),
        compiler_params=_cparams("arbitrary"),
        name="merge",
    )(xall, o_f, o_b, z, pin, gd, gp, modv, onorm.reshape(1, DN_HEAD_DIM), pm, pool_w,
      pool_scale.reshape(1, POOL_WIDTH), wud, wup, wo)


def _col_max(x):
    return jnp.max(x, axis=0, keepdims=True)


def _col_min(x):
    return jnp.min(x, axis=0, keepdims=True)


def _oddeven_merge(lo, hi, r):
    step = r * 2
    if step < hi - lo:
        yield from _oddeven_merge(lo, hi, step)
        yield from _oddeven_merge(lo + r, hi, step)
        yield from [(i, i + r) for i in range(lo + r, hi - r, step)]
    else:
        yield (lo, lo + r)


def _oddeven_merge_sort(lo, hi):
    if hi - lo >= 1:
        mid = lo + (hi - lo) // 2
        yield from _oddeven_merge_sort(lo, mid)
        yield from _oddeven_merge_sort(mid + 1, hi)
        yield from _oddeven_merge(lo, hi, 1)


def _exchange(v, i, j):
    v[i], v[j] = jnp.maximum(v[i], v[j]), jnp.minimum(v[i], v[j])


def _sorted_top_many(block_lists):
    k = PEER_TOPK
    vs = [list(blocks) for blocks in block_lists]
    for i, j in _oddeven_merge_sort(0, k - 1):
        for v in vs:
            _exchange(v, i, j)
    shift = SUBLANES // 2
    while shift >= 1:
        ws = [[pltpu.roll(x, shift, 0) for x in v] for v in vs]
        vs = [[jnp.maximum(v[j], w[k - 1 - j]) for j in range(k)] for v, w in zip(vs, ws)]
        stride = k // 2
        while stride >= 1:
            for i in range(k):
                if i & stride == 0:
                    for v in vs:
                        _exchange(v, i, i + stride)
            stride //= 2
        shift //= 2
    return vs


def _sublane_total(x):
    shift = SUBLANES // 2
    while shift >= 1:
        x = x + pltpu.roll(x, shift, 0)
        shift //= 2
    return x


def _unambiguous(blocks, top):
    ok = top[0] > top[1]
    for a in range(1, PEER_TOPK - 1):
        ok = ok & (top[a] > top[a + 1])
    ge = None
    for blk in blocks:
        one = jnp.where(blk >= top[PEER_TOPK - 1], 1.0, 0.0)
        ge = one if ge is None else ge + one
    return ok & (_sublane_total(ge) == float(PEER_TOPK))


def _top_select(scores, iota):
    n = float(scores[0].shape[0])
    cur = list(scores)
    idxs = [[] for _ in cur]
    for _ in range(PEER_TOPK):
        for j in range(len(cur)):
            m = _col_max(cur[j])
            cand = jnp.where(cur[j] == m, iota, n)
            idx = _col_min(cand)
            cur[j] = jnp.where(cand == idx, NEG_INF, cur[j])
            idxs[j].append(idx)
    return [jnp.concatenate(i, axis=0) for i in idxs]


def _peer_route_kernel(x_ref, modv_ref, nw_ref, wq_ref, keys_ref, hn_ref, n1_ref, e1_ref, r2_ref, e2_ref,
                       q_scr, *, seq):
    g = _group_of_row(pl.program_id(0) * ROW_TILE, seq)
    mod = modv_ref[pl.ds(g, 1), :]
    hn = _norm_mod(x_ref[...], nw_ref[...], mod, 3).astype(BF16)
    hn_ref[...] = hn
    q = jnp.dot(hn, wq_ref[...], preferred_element_type=F32)
    nsub = ROW_TILE // LANES
    for j in range(2 * PEER_HEADS):
        for t in range(nsub):
            q_scr[j, t] = q[t * LANES:(t + 1) * LANES, j * PEER_HALF:(j + 1) * PEER_HALF]
    k = PEER_TOPK
    nblk = N_KEYS // SUBLANES
    iota_k = lax.broadcasted_iota(jnp.int32, (N_KEYS, LANES), 0).astype(F32)
    iota_a = lax.broadcasted_iota(jnp.int32, (k, LANES), 0).astype(F32)
    iota_s = lax.broadcasted_iota(jnp.int32, (SUBLANES, LANES), 0).astype(F32)
    split = lambda s: [s[SUBLANES * j:SUBLANES * (j + 1), :] for j in range(nblk)]
    rep = lambda row: jnp.broadcast_to(row, (SUBLANES, LANES))

    def head(hh, carry):
        subs = range(nsub)
        s1 = [lax.dot_general(keys_ref[2 * hh], q_scr[2 * hh, t], NT_DIMS, preferred_element_type=F32) for t in subs]
        s2 = [lax.dot_general(keys_ref[2 * hh + 1], q_scr[2 * hh + 1, t], NT_DIMS, preferred_element_type=F32)
              for t in subs]
        b1, b2 = [split(x) for x in s1], [split(x) for x in s2]
        tops = _sorted_top_many(b1 + b2)
        top1, top2 = tops[:nsub], tops[nsub:]
        c = [jnp.concatenate([x[0:1, :] for x in top], axis=0) for top in top1]
        d = [jnp.concatenate([x[0:1, :] for x in top], axis=0) for top in top2]

        def grid_cells(c, d):
            cells = [c[0:1, :] + d[0:SUBLANES, :], c[0:1, :] + d[SUBLANES:k, :]]
            for a in range(1, SUBLANES):
                cells.append(jnp.where(iota_s < float(k // (a + 1)), c[a:a + 1, :] + d[0:SUBLANES, :], NEG_INF))
            cells.append(c[SUBLANES:k, :] + d[0:1, :])
            return cells + [jnp.full((SUBLANES, LANES), NEG_INF, F32)] * (k - len(cells))

        cuts = [x[k - 1][0:1, :] for x in _sorted_top_many([grid_cells(c[t], d[t]) for t in subs])]
        cnt_cut = [jnp.zeros((k, LANES), F32) for _ in subs]
        for b in range(k):
            cnt_cut = [acc + jnp.where(c[t] + d[t][b:b + 1, :] >= cuts[t], 1.0, 0.0) for t, acc in enumerate(cnt_cut)]

        def by_merge():
            cnts = [jnp.zeros((k, LANES), F32) for _ in subs]
            for _ in range(k):
                for t in subs:
                    dn = jnp.full((k, LANES), NEG_INF, F32)
                    for b in range(k):
                        dn = jnp.where(cnts[t] == float(b), d[t][b:b + 1, :], dn)
                    f = c[t] + dn
                    cand = jnp.where(f == _col_max(f), iota_a, float(k))
                    cnts[t] = cnts[t] + (cand == _col_min(cand)).astype(F32)
            return tuple(cnts)

        exact_cut = jnp.all(jnp.concatenate([jnp.sum(x, axis=0, keepdims=True) for x in cnt_cut], axis=0) == float(k))
        cnt = lax.cond(exact_cut, lambda: tuple(cnt_cut), by_merge)
        zsum = []
        for t in subs:
            e1c = jnp.exp(c[t] - c[t][0:1, :])
            e2d = jnp.exp(d[t] - d[t][0:1, :])
            part = jnp.zeros((k, LANES), F32)
            for b in range(k):
                part = part + jnp.where(cnt[t] > float(b), e2d[b:b + 1, :], 0.0)
            zsum.append(jnp.sum(e1c * part, axis=0, keepdims=True))

        def by_value():
            n1b = [[jnp.zeros((SUBLANES, LANES), F32)] * nblk for _ in subs]
            r2b = [[jnp.full((SUBLANES, LANES), float(k), F32)] * nblk for _ in subs]
            for a in range(k):
                for t in subs:
                    cnt_a = rep(cnt[t][a:a + 1, :])
                    n1b[t] = [jnp.where(blk == top1[t][a], cnt_a, acc) for blk, acc in zip(b1[t], n1b[t])]
                    r2b[t] = [jnp.where(blk == top2[t][a], float(a), acc) for blk, acc in zip(b2[t], r2b[t])]
            return tuple(jnp.concatenate(x, axis=0) for x in n1b + r2b)

        def by_index():
            idx = _top_select(s1 + s2, iota_k)
            out1, out2 = [], []
            for t in subs:
                n1 = jnp.zeros((N_KEYS, LANES), F32)
                r2 = jnp.full((N_KEYS, LANES), float(k), F32)
                for a in range(k):
                    n1 = jnp.where(iota_k == idx[t][a:a + 1, :], cnt[t][a:a + 1, :], n1)
                    r2 = jnp.where(iota_k == idx[nsub + t][a:a + 1, :], float(a), r2)
                out1.append(n1)
                out2.append(r2)
            return tuple(out1 + out2)

        clean = None
        for t in subs:
            ok = _unambiguous(b1[t], top1[t]) & _unambiguous(b2[t], top2[t])
            clean = ok if clean is None else clean & ok
        ranks = lax.cond(jnp.all(clean), by_value, by_index)
        for t in subs:
            lanes = slice(t * LANES, (t + 1) * LANES)
            n1_ref[hh, 0, :, lanes] = _bf16_pair_word(ranks[t])
            e1_ref[hh, 0, :, lanes] = _bf16_pair_word(0.5 * jnp.exp(s1[t] - c[t][0:1, :]) / zsum[t])
            r2_ref[hh, 0, :, lanes] = ranks[nsub + t].astype(BF16)
            e2_ref[hh, 0, :, lanes] = jnp.exp(s2[t] - d[t][0:1, :]).astype(BF16)
        return carry

    lax.fori_loop(0, PEER_HEADS, head, 0)


def _peer_route(xall, modv, norm_w, wq, keys, seq, rows_out):
    row = lambda n: pl.BlockSpec((ROW_TILE, n), lambda i: (i, 0))
    full = lambda *s: pl.BlockSpec(s, lambda i: (0,) * len(s))
    tab = pl.BlockSpec((PEER_HEADS, 1, N_KEYS, ROW_TILE), lambda i: (0, i, 0, 0))
    tab_shape = lambda dt: jax.ShapeDtypeStruct((PEER_HEADS, rows_out // ROW_TILE, N_KEYS, ROW_TILE), dt)
    return pl.pallas_call(
        functools.partial(_peer_route_kernel, seq=seq),
        grid=(rows_out // ROW_TILE,),
        in_specs=[row(D_MODEL), full(8, 6 * D_MODEL), full(1, D_MODEL), full(D_MODEL, PEER_HEADS * PEER_QDIM),
                  full(2 * PEER_HEADS, N_KEYS, PEER_HALF)],
        out_specs=[row(D_MODEL), tab, tab, tab, tab],
        out_shape=[jax.ShapeDtypeStruct((rows_out, D_MODEL), BF16), tab_shape(jnp.uint32), tab_shape(jnp.uint32),
                   tab_shape(BF16), tab_shape(BF16)],
        scratch_shapes=[pltpu.VMEM((2 * PEER_HEADS, ROW_TILE // LANES, LANES, PEER_HALF), F32)],
        compiler_params=_cparams("arbitrary"),
        name="peer_route",
    )(xall, modv, norm_w.reshape(1, D_MODEL), wq, keys.reshape(2 * PEER_HEADS, N_KEYS, PEER_HALF))


PEER_PIPE_LAG = 2
PEER_MXU_BLOCK = 256


def _peer_tile(n, lag, n_tiles, n_exp):
    m = jnp.clip(n - lag, 0, n_tiles - 1)
    return m // n_exp, m % n_exp


def _peer_dense_kernel(x_ref, modv_ref, hn_ref, u_ref, vt_ref, n1_ref, e1_ref, r2_ref, e2_ref, fn_ref, o_ref,
                       acc_ref, a0_ref, a1_ref, act0_ref, act1_ref, hn_scr, *, seq, final, n_tiles, n_exp):
    n = pl.program_id(0)
    t_c, e_c = _peer_tile(n, PEER_PIPE_LAG, n_tiles, n_exp)

    @pl.when(n == 0)
    def _():
        for ref in (acc_ref, a0_ref, a1_ref, act0_ref, act1_ref):
            ref[...] = jnp.zeros_like(ref)

    @pl.when(_peer_tile(n, 0, n_tiles, n_exp)[1] == 0)
    def _():
        hn_scr[...] = hn_ref[...]

    def step(a_next, a_cur, act_next, act_cur):
        zero = jnp.zeros((), BF16)
        blk = PEER_MXU_BLOCK

        def stage_a(rb, cb):
            rows, cols = slice(rb * blk, (rb + 1) * blk), slice(cb * blk, (cb + 1) * blk)
            a_next[rows, cols] = lax.dot_general(u_ref[rows, :], hn_scr[cols, :], NT_DIMS,
                                                 preferred_element_type=F32)

        def stage_b(t, i):
            lanes = slice(t * ROW_TILE, (t + 1) * ROW_TILE)
            wsum = None
            for hh in range(PEER_HEADS):
                keep = r2_ref[hh, t] < _pair_word_rows(n1_ref[hh, t, i:i + 1, :])
                term = jnp.where(keep, e2_ref[hh, t], zero) * _pair_word_rows(e1_ref[hh, t, i:i + 1, :])
                wsum = term if wsum is None else wsum + term
            rows = slice(i * N_KEYS, (i + 1) * N_KEYS)
            a = a_cur[rows, lanes]
            gelu = a * (1.0 + lax.erf(a * (2.0 ** -0.5)))
            act_next[rows, lanes] = gelu.astype(BF16) * wsum

        def stage_c(rb, cb):
            rows, cols = slice(rb * blk, (rb + 1) * blk), slice(cb * blk, (cb + 1) * blk)
            prev = jnp.where(e_c == 0, 0.0, acc_ref[rows, cols])
            acc_ref[rows, cols] = prev + jnp.dot(vt_ref[rows, :], act_cur[:, cols], preferred_element_type=F32)

        a_pieces = [(rb, cb) for cb in range(PEER_TOK_TILE // blk) for rb in range(PEER_EXP_TILE // blk)]
        c_pieces = [(rb, cb) for cb in range(PEER_TOK_TILE // blk) for rb in range(D_MODEL // blk)]
        b_pieces = [(t, i) for t in range(PEER_TOK_TILE // ROW_TILE) for i in range(PEER_EXP_TILE // N_KEYS)]
        per_mxu = len(b_pieces) // (len(a_pieces) + len(c_pieces))
        mxu_pieces = [x for pair in zip([(stage_a, p) for p in a_pieces], [(stage_c, p) for p in c_pieces])
                      for x in pair]
        for k, (fn, piece) in enumerate(mxu_pieces):
            fn(*piece)
            for t, i in b_pieces[k * per_mxu:(k + 1) * per_mxu]:
                stage_b(t, i)

    step(a0_ref, a1_ref, act0_ref, act1_ref)
    a1_ref[...] = a0_ref[...]

    @pl.when(n >= 0)
    def _():
        act1_ref[...] = act0_ref[...]

    @pl.when((e_c == n_exp - 1) & (n >= PEER_PIPE_LAG))
    def _():
        g = _group_of_row(t_c * PEER_TOK_TILE, seq)
        mod = modv_ref[pl.ds(g, 1), :]
        y = x_ref[...] + mod[:, 5 * D_MODEL:6 * D_MODEL] * acc_ref[...].T
        if final:
            y = y * lax.rsqrt(jnp.mean(y * y, axis=-1, keepdims=True) + EPS) * fn_ref[...]
        o_ref[...] = y


def _peer_dense(xall, modv, hn, u_bf, vt_bf, n1, e1, r2, e2, final_norm, seq, rows_out, final):
    tt, et = PEER_TOK_TILE, PEER_EXP_TILE
    n_first = et // N_KEYS
    n_exp = N_EXPERTS // et
    n_tiles = (rows_out // tt) * n_exp
    tile = lambda lag: (lambda n: _peer_tile(n, lag, n_tiles, n_exp))
    ta, tb, tc = tile(0), tile(1), tile(PEER_PIPE_LAG)
    full = lambda *s: pl.BlockSpec(s, lambda n: (0,) * len(s))
    per_first = pl.BlockSpec((PEER_HEADS, tt // ROW_TILE, n_first, ROW_TILE), lambda n: (0, tb(n)[0], tb(n)[1], 0))
    per_second = pl.BlockSpec((PEER_HEADS, tt // ROW_TILE, N_KEYS, ROW_TILE), lambda n: (0, tb(n)[0], 0, 0))
    return pl.pallas_call(
        functools.partial(_peer_dense_kernel, seq=seq, final=final, n_tiles=n_tiles, n_exp=n_exp),
        grid=(n_tiles + PEER_PIPE_LAG,),
        in_specs=[pl.BlockSpec((tt, D_MODEL), lambda n: (tc(n)[0], 0)),
                  full(8, 6 * D_MODEL),
                  pl.BlockSpec((tt, D_MODEL), lambda n: (ta(n)[0], 0)),
                  pl.BlockSpec((et, D_MODEL), lambda n: (ta(n)[1], 0)),
                  pl.BlockSpec((D_MODEL, et), lambda n: (0, tc(n)[1])),
                  per_first, per_first, per_second, per_second, full(1, D_MODEL)],
        out_specs=pl.BlockSpec((tt, D_MODEL), lambda n: (tc(n)[0], 0)),
        out_shape=jax.ShapeDtypeStruct((rows_out, D_MODEL), F32),
        scratch_shapes=[pltpu.VMEM((D_MODEL, tt), F32), pltpu.VMEM((et, tt), F32), pltpu.VMEM((et, tt), F32),
                        pltpu.VMEM((et, tt), BF16), pltpu.VMEM((et, tt), BF16), pltpu.VMEM((tt, D_MODEL), BF16)],
        compiler_params=_cparams("arbitrary"),
        name="peer_dense",
    )(xall, modv, hn, u_bf, vt_bf, n1, e1, r2, e2, final_norm.reshape(1, D_MODEL))


def _reorder_in_weight(w):
    s = np.cumsum((0, 3 * DN_WIDTH, DN_WIDTH, POOL_WIDTH, 2 * DN_HEADS, 2 * DN_HEADS, D_MODEL, D_MODEL))
    qkv, z, pin, b, a, gd, gp = (w[:, s[i]:s[i + 1]] for i in range(7))
    pad = jnp.zeros((w.shape[0], BA_PAD - 4 * DN_HEADS), w.dtype)
    return jnp.concatenate([qkv, z, pin, gd, gp, b, a, pad], axis=1).astype(BF16)


def _forward(x, c, ctx, c_ctx, w_mod, b_mod, norm_mix, w_in, conv_w, a_log, dt_bias, dn_out_norm, pool_w, pool_scale,
             w_up_dn, w_up_pool, w_out, norm_ffn, peer_wq, peer_keys, peer_u, peer_v, final_norm):
    seq, nctx = x.shape[1], ctx.shape[1]
    nlat = BATCH * seq
    xall = jnp.concatenate([x.reshape(nlat, D_MODEL), ctx.reshape(BATCH * nctx, D_MODEL)], axis=0)
    rows = xall.shape[0]
    cvec = jnp.concatenate([c, c_ctx[None, :], jnp.zeros((8 - BATCH - 1, D_MODEL), F32)], axis=0)
    modv_all = _modulation(cvec, w_mod, b_mod)
    for i in range(DEPTH):
        last = i == DEPTH - 1
        rows_out = nlat if last else rows
        modv = modv_all[i]
        qkv, z, pin, gd, gp, ba = _inproj(xall, modv, norm_mix[i], _reorder_in_weight(w_in[i]), seq)
        q, k, v, bg = _dnconv(qkv, ba, conv_w[i], a_log[i], dt_bias[i], seq, nctx)
        o_f, o_b = _dnscan(*_dnprep(q, k, v, bg, seq, nctx), seq, nctx)
        xall = _merge(xall, o_f, o_b, z, pin, gd, gp, modv, dn_out_norm[i], pool_w[i], pool_scale[i],
                      w_up_dn[i].astype(BF16), w_up_pool[i].astype(BF16), w_out[i].astype(BF16), seq, nctx, rows_out)
        hn, n1, e1, r2, e2 = _peer_route(xall, modv, norm_ffn[i], peer_wq[i].astype(BF16), peer_keys[i], seq, rows_out)
        xall = _peer_dense(xall, modv, hn, peer_u[i].astype(BF16), peer_v[i].T.astype(BF16), n1, e1, r2, e2,
                           final_norm, seq, rows_out, last)
    return xall.reshape(BATCH, seq, D_MODEL)


def kernel(x, c, ctx, c_ctx, w_mod, b_mod, norm_mix, w_in, conv_w, a_log, dt_bias, dn_out_norm, pool_w, pool_scale, w_up_dn, w_up_pool, w_out, norm_ffn, peer_wq, peer_keys, peer_u, peer_v, final_norm):
    return _forward(x, c, ctx, c_ctx, w_mod, b_mod, norm_mix, w_in, conv_w, a_log, dt_bias, dn_out_norm, pool_w,
                    pool_scale, w_up_dn, w_up_pool, w_out, norm_ffn, peer_wq, peer_keys, peer_u, peer_v, final_norm)
```

```python
import functools

import numpy as np
import jax
import jax.numpy as jnp
from jax import lax
from jax.experimental import pallas as pl
from jax.experimental.pallas import tpu as pltpu

D_MODEL = 1024
BATCH = 2
DEPTH = 2
GRID_W = 64
EPS = 1e-6

DN_HEADS = 4
DN_HEAD_DIM = 128
DN_WIDTH = DN_HEADS * DN_HEAD_DIM
SHORT_CONV = 4
DN_CHUNK = 64
DN_CHAINS = 2 * DN_HEADS
DN_PREP_CHUNKS = 2
DN_LOCKSTEP = 16

POOL_WINDOWS = (2, 4, 8, 16)
POOL_GROUPS = 4
POOL_WIDTH = D_MODEL // 2
POOL_GW = POOL_WIDTH // POOL_GROUPS

PEER_HEADS = 8
N_KEYS = 128
N_EXPERTS = N_KEYS * N_KEYS
PEER_TOPK = 16
PEER_QDIM = 256
PEER_HALF = PEER_QDIM // 2

BA_PAD = 128
IN_COLS_R = 3 * DN_WIDTH + DN_WIDTH + POOL_WIDTH + 2 * D_MODEL + BA_PAD

LANES = 128
SUBLANES = 8
ROW_TILE = 256
PEER_TOK_TILE = 512
PEER_EXP_TILE = 2048
PEER_SLAB = 256
HALO = 8
VMEM_LIMIT = 56 * 1024 * 1024

F32 = jnp.float32
BF16 = jnp.bfloat16
HIGHEST = lax.Precision.HIGHEST
NEG_INF = float("-inf")
NT_DIMS = (((1,), (1,)), ((), ()))


def _cparams(*sem):
    return pltpu.CompilerParams(dimension_semantics=sem, vmem_limit_bytes=VMEM_LIMIT)


def _group_of_row(row0, seq):
    return jnp.where(row0 < seq, 0, jnp.where(row0 < 2 * seq, 1, 2))


def _silu(x):
    return x * jax.nn.sigmoid(x)


def _split_bf16(a):
    hi = a.astype(BF16)
    lo = (a - hi.astype(F32)).astype(BF16)
    return hi, lo


def _bf16_pair_word(x):
    hi = lax.bitcast_convert_type(x.astype(BF16).astype(F32), jnp.uint32)
    return hi | (hi >> 16)


def _pair_word_rows(row):
    tile = pltpu.bitcast(jnp.broadcast_to(row, (SUBLANES, row.shape[1])), BF16)
    return jnp.concatenate([tile] * (N_KEYS // tile.shape[0]), axis=0)


def _dot_split(a, b):
    ah, al = _split_bf16(a)
    bh, bl = _split_bf16(b)
    a4 = jnp.concatenate([ah, al, ah, al], axis=1)
    b4 = jnp.concatenate([bh, bh, bl, bl], axis=0)
    return jnp.dot(a4, b4, preferred_element_type=F32)


def _mod_kernel(c_ref, w_ref, b_ref, o_ref):
    o_ref[0] = jnp.dot(_silu(c_ref[...]), w_ref[0], preferred_element_type=F32) + b_ref[0]


def _modulation(cvec, w_mod, b_mod):
    tn = 1536
    return pl.pallas_call(
        _mod_kernel,
        grid=(DEPTH, 6 * D_MODEL // tn),
        in_specs=[
            pl.BlockSpec((8, D_MODEL), lambda l, j: (0, 0)),
            pl.BlockSpec((1, D_MODEL, tn), lambda l, j: (l, 0, j)),
            pl.BlockSpec((1, 1, tn), lambda l, j: (l, 0, j)),
        ],
        out_specs=pl.BlockSpec((1, 8, tn), lambda l, j: (l, 0, j)),
        out_shape=jax.ShapeDtypeStruct((DEPTH, 8, 6 * D_MODEL), F32),
        compiler_params=_cparams("arbitrary", "arbitrary"),
        name="modulation",
    )(cvec, w_mod, b_mod.reshape(DEPTH, 1, 6 * D_MODEL))


def _norm_mod(x, nw, mod, k):
    ms = jnp.mean(x * x, axis=-1, keepdims=True)
    xn = x * lax.rsqrt(ms + EPS) * nw
    sh = mod[:, k * D_MODEL:(k + 1) * D_MODEL]
    sc = mod[:, (k + 1) * D_MODEL:(k + 2) * D_MODEL]
    return xn * (1 + sc) + sh


def _inproj_kernel(x_ref, modv_ref, nw_ref, w_ref, qkv_ref, z_ref, pin_ref, gd_ref, gp_ref, ba_ref, *, seq):
    g = _group_of_row(pl.program_id(0) * ROW_TILE, seq)
    mod = modv_ref[pl.ds(g, 1), :]
    h = _norm_mod(x_ref[...], nw_ref[...], mod, 0)
    y = jnp.dot(h.astype(BF16), w_ref[...], preferred_element_type=F32)
    o = 0
    for ref in (qkv_ref, z_ref, pin_ref, gd_ref, gp_ref, ba_ref):
        n = ref.shape[1]
        ref[...] = y[:, o:o + n]
        o += n


def _inproj(xall, modv, norm_w, w_in_r, seq):
    rows = xall.shape[0]
    widths = (3 * DN_WIDTH, DN_WIDTH, POOL_WIDTH, D_MODEL, D_MODEL, BA_PAD)
    return pl.pallas_call(
        functools.partial(_inproj_kernel, seq=seq),
        grid=(rows // ROW_TILE,),
        in_specs=[
            pl.BlockSpec((ROW_TILE, D_MODEL), lambda i: (i, 0)),
            pl.BlockSpec((8, 6 * D_MODEL), lambda i: (0, 0)),
            pl.BlockSpec((1, D_MODEL), lambda i: (0, 0)),
            pl.BlockSpec((D_MODEL, IN_COLS_R), lambda i: (0, 0)),
        ],
        out_specs=[pl.BlockSpec((ROW_TILE, n), lambda i: (i, 0)) for n in widths],
        out_shape=[jax.ShapeDtypeStruct((rows, n), F32) for n in widths],
        compiler_params=_cparams("arbitrary"),
        name="inproj",
    )(xall, modv, norm_w.reshape(1, D_MODEL), w_in_r)


def _dnconv_kernel(cur_ref, prev_ref, next_ref, cw_ref, ba_ref, alog_ref, dtb_ref,
                   q_ref, k_ref, v_ref, bg_ref, *, seq, ctx):
    row0 = pl.program_id(0) * ROW_TILE
    nlat = BATCH * seq
    is_start = (row0 == 0) | (row0 == seq) | (row0 == nlat) | (row0 == nlat + ctx)
    row1 = row0 + ROW_TILE
    is_end = (row1 == seq) | (row1 == nlat) | (row1 == nlat + ctx) | (row1 == nlat + BATCH * ctx)
    prev = jnp.where(is_start, 0.0, prev_ref[...])
    nxt = jnp.where(is_end, 0.0, next_ref[...])
    ext = jnp.concatenate([prev, cur_ref[...], nxt], axis=0)
    left = SHORT_CONV // 2
    cw = cw_ref[...]
    y = None
    for j in range(SHORT_CONV):
        o = HALO - left + j
        term = ext[o:o + ROW_TILE, :] * cw[j:j + 1, :]
        y = term if y is None else y + term
    y = _silu(y)
    for hh in range(DN_HEADS):
        lo, hi = hh * DN_HEAD_DIM, (hh + 1) * DN_HEAD_DIM
        qh = y[:, lo:hi]
        kh = y[:, DN_WIDTH + lo:DN_WIDTH + hi]
        q_ref[:, lo:hi] = qh * lax.rsqrt(jnp.sum(qh * qh, axis=-1, keepdims=True) + EPS) * (DN_HEAD_DIM ** -0.5)
        k_ref[:, lo:hi] = kh * lax.rsqrt(jnp.sum(kh * kh, axis=-1, keepdims=True) + EPS)
    v_ref[...] = y[:, 2 * DN_WIDTH:]
    ba = ba_ref[...]
    beta = jax.nn.sigmoid(ba)
    xs = ba + dtb_ref[...]
    softplus = jnp.maximum(xs, 0.0) + jnp.log(1.0 + jnp.exp(-jnp.abs(xs)))
    gdec = -jnp.exp(alog_ref[...]) * softplus
    col = lax.broadcasted_iota(jnp.int32, ba.shape, 1)
    bg_ref[...] = jnp.where(col < 2 * DN_HEADS, beta, jnp.where(col < 4 * DN_HEADS, gdec, 0.0))


def _dnconv(qkv, ba, conv_w, a_log, dt_bias, seq, ctx):
    rows = qkv.shape[0]
    nh = ROW_TILE // HALO
    last = rows // HALO - 1
    pad = jnp.zeros((2 * DN_HEADS,), F32)
    tail = jnp.zeros((BA_PAD - 4 * DN_HEADS,), F32)
    alog = jnp.concatenate([pad, a_log.reshape(-1), tail]).reshape(1, BA_PAD)
    dtb = jnp.concatenate([pad, dt_bias.reshape(-1), tail]).reshape(1, BA_PAD)
    w3 = 3 * DN_WIDTH
    return pl.pallas_call(
        functools.partial(_dnconv_kernel, seq=seq, ctx=ctx),
        grid=(rows // ROW_TILE,),
        in_specs=[
            pl.BlockSpec((ROW_TILE, w3), lambda i: (i, 0)),
            pl.BlockSpec((HALO, w3), lambda i: (jnp.maximum(i * nh - 1, 0), 0)),
            pl.BlockSpec((HALO, w3), lambda i: (jnp.minimum((i + 1) * nh, last), 0)),
            pl.BlockSpec((SHORT_CONV, w3), lambda i: (0, 0)),
            pl.BlockSpec((ROW_TILE, BA_PAD), lambda i: (i, 0)),
            pl.BlockSpec((1, BA_PAD), lambda i: (0, 0)),
            pl.BlockSpec((1, BA_PAD), lambda i: (0, 0)),
        ],
        out_specs=[pl.BlockSpec((ROW_TILE, DN_WIDTH), lambda i: (i, 0))] * 3
        + [pl.BlockSpec((ROW_TILE, BA_PAD), lambda i: (i, 0))],
        out_shape=[jax.ShapeDtypeStruct((rows, DN_WIDTH), F32)] * 3
        + [jax.ShapeDtypeStruct((rows, BA_PAD), F32)],
        compiler_params=_cparams("arbitrary"),
        name="dnconv",
    )(qkv, qkv, qkv, conv_w, ba, alog, dtb)


def _dnprep_kernel(q_ref, k_ref, v_ref, bg_ref, bgt_ref, wq_ref, u_ref, lk_ref, eg_ref):
    c = DN_CHUNK
    ri = lax.broadcasted_iota(jnp.int32, (c, c), 0)
    ci = lax.broadcasted_iota(jnp.int32, (c, c), 1)
    dirs = ((ri >= ci, ri > ci, ci >= ri, c - 1), (ri <= ci, ri < ci, ci <= ri, 0))
    chains = []
    for j in range(DN_PREP_CHUNKS):
        rows = slice(j * c, (j + 1) * c)
        bg = bg_ref[rows, :]
        bgh, bgl = _split_bf16(bg)
        bgth, bgtl = _split_bf16(bgt_ref[j])
        bg2 = jnp.concatenate([bgh, bgl], axis=0)
        bgt2 = jnp.concatenate([bgth, bgtl], axis=1)
        egs = []
        for d, (incl, strict, incl_t, last) in enumerate(dirs):
            m = incl.astype(BF16)
            mt = incl_t.astype(BF16)
            gc_all = jnp.dot(jnp.concatenate([m, m], axis=1), bg2, preferred_element_type=F32)
            gr_all = jnp.dot(bgt2, jnp.concatenate([mt, mt], axis=0), preferred_element_type=F32)
            for hh in range(DN_HEADS):
                lo, hi = hh * DN_HEAD_DIM, (hh + 1) * DN_HEAD_DIM
                ch = d * DN_HEADS + hh
                gcol = 2 * DN_HEADS + ch
                q = q_ref[rows, lo:hi]
                k = k_ref[rows, lo:hi]
                v = v_ref[rows, lo:hi]
                beta = bg[:, ch:ch + 1]
                gc = gc_all[:, gcol:gcol + 1]
                gr = gr_all[gcol:gcol + 1, :]
                glast = gc_all[last:last + 1, gcol:gcol + 1]
                decay = jnp.where(incl, jnp.exp(jnp.minimum(gc - gr, 0.0)), 0.0)
                kk = lax.dot_general(k, k, NT_DIMS, preferred_element_type=F32)
                qk = lax.dot_general(q, k, NT_DIMS, preferred_element_type=F32)
                egc = jnp.exp(gc)
                cols = slice(ch * DN_HEAD_DIM, (ch + 1) * DN_HEAD_DIM)
                wq_ref[0, j, c:2 * c, cols] = (q * egc).astype(BF16)
                lk_ref[0, j, ch, 0:c, :] = jnp.where(incl, qk * decay, 0.0).astype(BF16)
                lk_ref[0, j, ch, c:, :] = (k * jnp.exp(glast - gc)).T.astype(BF16)
                egs.append(jnp.broadcast_to(jnp.exp(glast), (1, LANES)))
                chains.append((j, cols, jnp.where(strict, -(beta * kk * decay), 0.0),
                               jnp.concatenate([k * (beta * egc), v * beta], axis=1)))
        eg_ref[0, j] = jnp.concatenate(egs, axis=0)
    for g0 in range(0, len(chains), DN_LOCKSTEP):
        group = chains[g0:g0 + DN_LOCKSTEP]
        ps = [ch[2] for ch in group]
        ys = [ch[3] for ch in group]
        ys = [y + _dot_split(p, y) for p, y in zip(ps, ys)]
        for _ in range(int(np.log2(c)) - 1):
            ps = [_dot_split(p, p) for p in ps]
            ys = [y + _dot_split(p, y) for p, y in zip(ps, ys)]
        for (j, cols, _, _), y in zip(group, ys):
            wq_ref[0, j, 0:c, cols] = y[:, :DN_HEAD_DIM].astype(BF16)
            u_ref[0, j, :, cols] = y[:, DN_HEAD_DIM:]


def _dnprep(q, k, v, bg, seq, ctx):
    rows = q.shape[0]
    c = DN_CHUNK
    n = DN_PREP_CHUNKS
    nct, nlt = ctx // c, seq // c
    assert nct % n == 0 and nlt % n == 0
    nch = nct + nlt
    bgt = bg[:, :4 * DN_HEADS].reshape(rows // c, c, 4 * DN_HEADS).transpose(0, 2, 1)

    def seq_pos(i):
        i = i * n
        lat = i < BATCH * nlt
        j = i - BATCH * nlt
        return jnp.where(lat, i // nlt, j // nct), jnp.where(lat, nct + i % nlt, j % nct) // n

    def out_spec(*tail):
        return pl.BlockSpec((1, n) + tail, lambda i: seq_pos(i) + (0,) * len(tail))

    wide = pl.BlockSpec((n * c, DN_WIDTH), lambda i: (i, 0))
    width = DN_CHAINS * DN_HEAD_DIM
    return pl.pallas_call(
        _dnprep_kernel,
        grid=(rows // (n * c),),
        in_specs=[wide, wide, wide,
                  pl.BlockSpec((n * c, BA_PAD), lambda i: (i, 0)),
                  pl.BlockSpec((n, 4 * DN_HEADS, c), lambda i: (i, 0, 0))],
        out_specs=[out_spec(2 * c, width), out_spec(c, width), out_spec(DN_CHAINS, c + DN_HEAD_DIM, c),
                   out_spec(DN_CHAINS, LANES)],
        out_shape=[jax.ShapeDtypeStruct((BATCH, nch, 2 * c, width), BF16),
                   jax.ShapeDtypeStruct((BATCH, nch, c, width), F32),
                   jax.ShapeDtypeStruct((BATCH, nch, DN_CHAINS, c + DN_HEAD_DIM, c), BF16),
                   jax.ShapeDtypeStruct((BATCH, nch, DN_CHAINS, LANES), F32)],
        compiler_params=_cparams("arbitrary"),
        name="dnprep",
    )(q, k, v, bg, bgt)


def _dnscan_kernel(wqf_ref, uf_ref, lkf_ref, egf_ref, wqb_ref, ub_ref, lkb_ref, egb_ref, of_ref, ob_ref, s_ref):
    @pl.when(pl.program_id(0) == 0)
    def _():
        s_ref[...] = jnp.zeros_like(s_ref)

    c = DN_CHUNK
    dirs = ((wqf_ref, uf_ref, lkf_ref, egf_ref, of_ref), (wqb_ref, ub_ref, lkb_ref, egb_ref, ob_ref))
    chains = [(d, b, hh) for d in range(2) for b in range(BATCH) for hh in range(DN_HEADS)]
    sidx = lambda d, b, hh: (b * 2 + d) * DN_HEADS + hh
    cols = lambda hh: slice(hh * DN_HEAD_DIM, (hh + 1) * DN_HEAD_DIM)
    ss = [s_ref[sidx(*ch)] for ch in chains]
    r1 = [jnp.dot(dirs[d][0][b, 0, :, cols(hh)], s.astype(BF16), preferred_element_type=F32)
          for (d, b, hh), s in zip(chains, ss)]
    vn = [dirs[d][1][b, 0, :, cols(hh)] - r[:c] for (d, b, hh), r in zip(chains, r1)]
    r2 = [jnp.dot(dirs[d][2][b, 0, hh], v.astype(BF16), preferred_element_type=F32)
          for (d, b, hh), v in zip(chains, vn)]
    for (d, b, hh), s, a1, a2 in zip(chains, ss, r1, r2):
        dirs[d][4][b, 0, :, cols(hh)] = a1[c:] + a2[:c]
        row = d * DN_HEADS + hh
        s_ref[sidx(d, b, hh)] = s * dirs[d][3][b, 0, row:row + 1, :] + a2[c:]


def _dnscan(wq, u, lk, eg, seq, ctx):
    c = DN_CHUNK
    nct, nlt = ctx // c, seq // c
    nch = nct + nlt

    def bwd(s):
        return jnp.where(s < nct, nct - 1 - s, 2 * nct + nlt - 1 - s)

    def specs(pos, d):
        return [pl.BlockSpec((BATCH, 1, 2 * c, DN_WIDTH), lambda s: (0, pos(s), 0, d)),
                pl.BlockSpec((BATCH, 1, c, DN_WIDTH), lambda s: (0, pos(s), 0, d)),
                pl.BlockSpec((BATCH, 1, DN_HEADS, c + DN_HEAD_DIM, c), lambda s: (0, pos(s), d, 0, 0)),
                pl.BlockSpec((BATCH, 1, DN_CHAINS, LANES), lambda s: (0, pos(s), 0, 0))]

    fwd = lambda s: s
    return pl.pallas_call(
        _dnscan_kernel,
        grid=(nch,),
        in_specs=specs(fwd, 0) + specs(bwd, 1),
        out_specs=[pl.BlockSpec((BATCH, 1, c, DN_WIDTH), lambda s: (0, s, 0, 0)),
                   pl.BlockSpec((BATCH, 1, c, DN_WIDTH), lambda s: (0, bwd(s), 0, 0))],
        out_shape=[jax.ShapeDtypeStruct((BATCH, nch, c, DN_WIDTH), F32)] * 2,
        scratch_shapes=[pltpu.VMEM((BATCH * DN_CHAINS, DN_HEAD_DIM, DN_HEAD_DIM), F32)],
        compiler_params=_cparams("arbitrary"),
        name="dnscan",
    )(wq, u, lk, eg, wq, u, lk, eg)


def _pool_matrices(tile, seg):
    t = np.arange(tile)
    p = t % seg
    mats = []
    for win in POOL_WINDOWS:
        lo = np.clip(p - win // 2, 0, seg)
        hi = np.clip(p + win - win // 2, 0, seg)
        same = (t[:, None] // seg) == (t[None, :] // seg)
        inside = same & (p[None, :] >= lo[:, None]) & (p[None, :] < hi[:, None])
        mats.append(inside / (hi - lo)[:, None].astype(np.float64) - np.eye(tile))
    return np.stack(mats).astype(np.float32)


def _merge_kernel(x_ref, of_ref, ob_ref, z_ref, pin_ref, gd_ref, gp_ref, modv_ref, on_ref, pm_ref, pw_ref, ps_ref,
                  wud_ref, wup_ref, wo_ref, o_ref, *, seq):
    g = _group_of_row(pl.program_id(0) * ROW_TILE, seq)
    mod = modv_ref[pl.ds(g, 1), :]
    gate = mod[:, 2 * D_MODEL:3 * D_MODEL]
    o = (of_ref[0] + ob_ref[0]).reshape(ROW_TILE, DN_WIDTH)
    z = z_ref[...]
    onw = on_ref[...]
    ys = []
    for hh in range(DN_HEADS):
        lo, hi = hh * DN_HEAD_DIM, (hh + 1) * DN_HEAD_DIM
        oh = o[:, lo:hi]
        ys.append(oh * lax.rsqrt(jnp.mean(oh * oh, axis=-1, keepdims=True) + EPS) * onw * _silu(z[:, lo:hi]))
    y_dn = jnp.concatenate(ys, axis=1)
    pin = pin_ref[...]
    yp = []
    for gi in range(POOL_GROUPS):
        lo, hi = gi * POOL_GW, (gi + 1) * POOL_GW
        pooled = jnp.dot(pm_ref[0, gi], pin[:, lo:hi], precision=HIGHEST, preferred_element_type=F32)
        yp.append(jnp.dot(pooled, pw_ref[gi], preferred_element_type=F32))
    y_pool = jnp.concatenate(yp, axis=1) * ps_ref[...]
    m = (jax.nn.sigmoid(gd_ref[...]) * jnp.dot(y_dn.astype(BF16), wud_ref[...], preferred_element_type=F32)
         + jax.nn.sigmoid(gp_ref[...]) * jnp.dot(y_pool.astype(BF16), wup_ref[...], preferred_element_type=F32))
    out = jnp.dot(m.astype(BF16), wo_ref[...], preferred_element_type=F32)
    o_ref[...] = x_ref[...] + gate * out


def _merge(xall, o_f, o_b, z, pin, gd, gp, modv, onorm, pool_w, pool_scale, wud, wup, wo, seq, ctx, rows_out):
    assert ctx == ROW_TILE and seq % ROW_TILE == 0
    lat_tiles = BATCH * seq // ROW_TILE
    tiles_per_seq = seq // ROW_TILE
    cpt = ROW_TILE // DN_CHUNK
    pm = jnp.asarray(np.stack([_pool_matrices(ROW_TILE, GRID_W), _pool_matrices(ROW_TILE, ctx)]))
    row = lambda n: pl.BlockSpec((ROW_TILE, n), lambda i: (i, 0))
    full = lambda *s: pl.BlockSpec(s, lambda i: (0,) * len(s))

    def o_pos(i):
        lat = i < lat_tiles
        return (jnp.where(lat, i // tiles_per_seq, i - lat_tiles), jnp.where(lat, 1 + i % tiles_per_seq, 0), 0, 0)

    o_spec = pl.BlockSpec((1, cpt, DN_CHUNK, DN_WIDTH), o_pos)
    return pl.pallas_call(
        functools.partial(_merge_kernel, seq=seq),
        grid=(rows_out // ROW_TILE,),
        in_specs=[
            row(D_MODEL), o_spec, o_spec, row(DN_WIDTH), row(POOL_WIDTH), row(D_MODEL), row(D_MODEL),
            full(8, 6 * D_MODEL), full(1, DN_HEAD_DIM),
            pl.BlockSpec((1, POOL_GROUPS, ROW_TILE, ROW_TILE), lambda i: (jnp.where(i >= lat_tiles, 1, 0), 0, 0, 0)),
            full(POOL_GROUPS, POOL_GW, POOL_GW), full(1, POOL_WIDTH),
            full(DN_WIDTH, D_MODEL), full(POOL_WIDTH, D_MODEL), full(D_MODEL, D_MODEL),
        ],
        out_specs=row(D_MODEL),
        out_shape=jax.ShapeDtypeStruct((rows_out, D_MODEL), F32),
        compiler_params=_cparams("arbitrary"),
        name="merge",
    )(xall, o_f, o_b, z, pin, gd, gp, modv, onorm.reshape(1, DN_HEAD_DIM), pm, pool_w,
      pool_scale.reshape(1, POOL_WIDTH), wud, wup, wo)


def _col_max(x):
    return jnp.max(x, axis=0, keepdims=True)


def _col_min(x):
    return jnp.min(x, axis=0, keepdims=True)


def _oddeven_merge(lo, hi, r):
    step = r * 2
    if step < hi - lo:
        yield from _oddeven_merge(lo, hi, step)
        yield from _oddeven_merge(lo + r, hi, step)
        yield from [(i, i + r) for i in range(lo + r, hi - r, step)]
    else:
        yield (lo, lo + r)


def _oddeven_merge_sort(lo, hi):
    if hi - lo >= 1:
        mid = lo + (hi - lo) // 2
        yield from _oddeven_merge_sort(lo, mid)
        yield from _oddeven_merge_sort(mid + 1, hi)
        yield from _oddeven_merge(lo, hi, 1)


def _exchange(v, i, j):
    v[i], v[j] = jnp.maximum(v[i], v[j]), jnp.minimum(v[i], v[j])


def _sorted_top_many(block_lists):
    k = PEER_TOPK
    vs = [list(blocks) for blocks in block_lists]
    for i, j in _oddeven_merge_sort(0, k - 1):
        for v in vs:
            _exchange(v, i, j)
    shift = SUBLANES // 2
    while shift >= 1:
        ws = [[pltpu.roll(x, shift, 0) for x in v] for v in vs]
        vs = [[jnp.maximum(v[j], w[k - 1 - j]) for j in range(k)] for v, w in zip(vs, ws)]
        stride = k // 2
        while stride >= 1:
            for i in range(k):
                if i & stride == 0:
                    for v in vs:
                        _exchange(v, i, i + stride)
            stride //= 2
        shift //= 2
    return vs


def _sublane_total(x):
    shift = SUBLANES // 2
    while shift >= 1:
        x = x + pltpu.roll(x, shift, 0)
        shift //= 2
    return x


def _unambiguous(blocks, top):
    ok = top[0] > top[1]
    for a in range(1, PEER_TOPK - 1):
        ok = ok & (top[a] > top[a + 1])
    ge = None
    for blk in blocks:
        one = jnp.where(blk >= top[PEER_TOPK - 1], 1.0, 0.0)
        ge = one if ge is None else ge + one
    return ok & (_sublane_total(ge) == float(PEER_TOPK))


def _top_select(scores, iota):
    n = float(scores[0].shape[0])
    cur = list(scores)
    idxs = [[] for _ in cur]
    for _ in range(PEER_TOPK):
        for j in range(len(cur)):
            m = _col_max(cur[j])
            cand = jnp.where(cur[j] == m, iota, n)
            idx = _col_min(cand)
            cur[j] = jnp.where(cand == idx, NEG_INF, cur[j])
            idxs[j].append(idx)
    return [jnp.concatenate(i, axis=0) for i in idxs]


def _peer_route_kernel(x_ref, modv_ref, nw_ref, wq_ref, keys_ref, hn_ref, n1_ref, e1_ref, r2_ref, e2_ref,
                       q_scr, *, seq):
    g = _group_of_row(pl.program_id(0) * ROW_TILE, seq)
    mod = modv_ref[pl.ds(g, 1), :]
    hn = _norm_mod(x_ref[...], nw_ref[...], mod, 3).astype(BF16)
    hn_ref[...] = hn
    q = jnp.dot(hn, wq_ref[...], preferred_element_type=F32)
    nsub = ROW_TILE // LANES
    for j in range(2 * PEER_HEADS):
        for t in range(nsub):
            q_scr[j, t] = q[t * LANES:(t + 1) * LANES, j * PEER_HALF:(j + 1) * PEER_HALF]
    k = PEER_TOPK
    nblk = N_KEYS // SUBLANES
    iota_k = lax.broadcasted_iota(jnp.int32, (N_KEYS, LANES), 0).astype(F32)
    iota_a = lax.broadcasted_iota(jnp.int32, (k, LANES), 0).astype(F32)
    iota_s = lax.broadcasted_iota(jnp.int32, (SUBLANES, LANES), 0).astype(F32)
    split = lambda s: [s[SUBLANES * j:SUBLANES * (j + 1), :] for j in range(nblk)]
    rep = lambda row: jnp.broadcast_to(row, (SUBLANES, LANES))

    def head(hh, carry):
        subs = range(nsub)
        s1 = [lax.dot_general(keys_ref[2 * hh], q_scr[2 * hh, t], NT_DIMS, preferred_element_type=F32) for t in subs]
        s2 = [lax.dot_general(keys_ref[2 * hh + 1], q_scr[2 * hh + 1, t], NT_DIMS, preferred_element_type=F32)
              for t in subs]
        b1, b2 = [split(x) for x in s1], [split(x) for x in s2]
        tops = _sorted_top_many(b1 + b2)
        top1, top2 = tops[:nsub], tops[nsub:]
        c = [jnp.concatenate([x[0:1, :] for x in top], axis=0) for top in top1]
        d = [jnp.concatenate([x[0:1, :] for x in top], axis=0) for top in top2]

        def grid_cells(c, d):
            cells = [c[0:1, :] + d[0:SUBLANES, :], c[0:1, :] + d[SUBLANES:k, :]]
            for a in range(1, SUBLANES):
                cells.append(jnp.where(iota_s < float(k // (a + 1)), c[a:a + 1, :] + d[0:SUBLANES, :], NEG_INF))
            cells.append(c[SUBLANES:k, :] + d[0:1, :])
            return cells + [jnp.full((SUBLANES, LANES), NEG_INF, F32)] * (k - len(cells))

        cuts = [x[k - 1][0:1, :] for x in _sorted_top_many([grid_cells(c[t], d[t]) for t in subs])]
        cnt_cut = [jnp.zeros((k, LANES), F32) for _ in subs]
        for b in range(k):
            cnt_cut = [acc + jnp.where(c[t] + d[t][b:b + 1, :] >= cuts[t], 1.0, 0.0) for t, acc in enumerate(cnt_cut)]

        def by_merge():
            cnts = [jnp.zeros((k, LANES), F32) for _ in subs]
            for _ in range(k):
                for t in subs:
                    dn = jnp.full((k, LANES), NEG_INF, F32)
                    for b in range(k):
                        dn = jnp.where(cnts[t] == float(b), d[t][b:b + 1, :], dn)
                    f = c[t] + dn
                    cand = jnp.where(f == _col_max(f), iota_a, float(k))
                    cnts[t] = cnts[t] + (cand == _col_min(cand)).astype(F32)
            return tuple(cnts)

        exact_cut = jnp.all(jnp.concatenate([jnp.sum(x, axis=0, keepdims=True) for x in cnt_cut], axis=0) == float(k))
        cnt = lax.cond(exact_cut, lambda: tuple(cnt_cut), by_merge)
        zsum = []
        for t in subs:
            e1c = jnp.exp(c[t] - c[t][0:1, :])
            e2d = jnp.exp(d[t] - d[t][0:1, :])
            part = jnp.zeros((k, LANES), F32)
            for b in range(k):
                part = part + jnp.where(cnt[t] > float(b), e2d[b:b + 1, :], 0.0)
            zsum.append(jnp.sum(e1c * part, axis=0, keepdims=True))

        def by_value():
            n1b = [[jnp.zeros((SUBLANES, LANES), F32)] * nblk for _ in subs]
            r2b = [[jnp.full((SUBLANES, LANES), float(k), F32)] * nblk for _ in subs]
            for a in range(k):
                for t in subs:
                    cnt_a = rep(cnt[t][a:a + 1, :])
                    n1b[t] = [jnp.where(blk == top1[t][a], cnt_a, acc) for blk, acc in zip(b1[t], n1b[t])]
                    r2b[t] = [jnp.where(blk == top2[t][a], float(a), acc) for blk, acc in zip(b2[t], r2b[t])]
            return tuple(jnp.concatenate(x, axis=0) for x in n1b + r2b)

        def by_index():
            idx = _top_select(s1 + s2, iota_k)
            out1, out2 = [], []
            for t in subs:
                n1 = jnp.zeros((N_KEYS, LANES), F32)
                r2 = jnp.full((N_KEYS, LANES), float(k), F32)
                for a in range(k):
                    n1 = jnp.where(iota_k == idx[t][a:a + 1, :], cnt[t][a:a + 1, :], n1)
                    r2 = jnp.where(iota_k == idx[nsub + t][a:a + 1, :], float(a), r2)
                out1.append(n1)
                out2.append(r2)
            return tuple(out1 + out2)

        clean = None
        for t in subs:
            ok = _unambiguous(b1[t], top1[t]) & _unambiguous(b2[t], top2[t])
            clean = ok if clean is None else clean & ok
        ranks = lax.cond(jnp.all(clean), by_value, by_index)
        for t in subs:
            lanes = slice(t * LANES, (t + 1) * LANES)
            n1_ref[hh, 0, :, lanes] = _bf16_pair_word(ranks[t])
            e1_ref[hh, 0, :, lanes] = _bf16_pair_word(0.5 * jnp.exp(s1[t] - c[t][0:1, :]) / zsum[t])
            r2_ref[hh, 0, :, lanes] = ranks[nsub + t].astype(BF16)
            e2_ref[hh, 0, :, lanes] = jnp.exp(s2[t] - d[t][0:1, :]).astype(BF16)
        return carry

    lax.fori_loop(0, PEER_HEADS, head, 0)


def _peer_route(xall, modv, norm_w, wq, keys, seq, rows_out):
    row = lambda n: pl.BlockSpec((ROW_TILE, n), lambda i: (i, 0))
    full = lambda *s: pl.BlockSpec(s, lambda i: (0,) * len(s))
    tab = pl.BlockSpec((PEER_HEADS, 1, N_KEYS, ROW_TILE), lambda i: (0, i, 0, 0))
    tab_shape = lambda dt: jax.ShapeDtypeStruct((PEER_HEADS, rows_out // ROW_TILE, N_KEYS, ROW_TILE), dt)
    return pl.pallas_call(
        functools.partial(_peer_route_kernel, seq=seq),
        grid=(rows_out // ROW_TILE,),
        in_specs=[row(D_MODEL), full(8, 6 * D_MODEL), full(1, D_MODEL), full(D_MODEL, PEER_HEADS * PEER_QDIM),
                  full(2 * PEER_HEADS, N_KEYS, PEER_HALF)],
        out_specs=[row(D_MODEL), tab, tab, tab, tab],
        out_shape=[jax.ShapeDtypeStruct((rows_out, D_MODEL), BF16), tab_shape(jnp.uint32), tab_shape(jnp.uint32),
                   tab_shape(BF16), tab_shape(BF16)],
        scratch_shapes=[pltpu.VMEM((2 * PEER_HEADS, ROW_TILE // LANES, LANES, PEER_HALF), F32)],
        compiler_params=_cparams("arbitrary"),
        name="peer_route",
    )(xall, modv, norm_w.reshape(1, D_MODEL), wq, keys.reshape(2 * PEER_HEADS, N_KEYS, PEER_HALF))


PEER_PIPE_LAG = 2
PEER_MXU_BLOCK = 256


def _peer_tile(n, lag, n_tiles, n_exp):
    m = jnp.clip(n - lag, 0, n_tiles - 1)
    return m // n_exp, m % n_exp


def _peer_dense_kernel(x_ref, modv_ref, hn_ref, u_ref, vt_ref, n1_ref, e1_ref, r2_ref, e2_ref, fn_ref, o_ref,
                       acc_ref, a0_ref, a1_ref, act0_ref, act1_ref, hn_scr, *, seq, final, n_tiles, n_exp):
    n = pl.program_id(0)
    t_c, e_c = _peer_tile(n, PEER_PIPE_LAG, n_tiles, n_exp)

    @pl.when(n == 0)
    def _():
        for ref in (acc_ref, a0_ref, a1_ref, act0_ref, act1_ref):
            ref[...] = jnp.zeros_like(ref)

    @pl.when(_peer_tile(n, 0, n_tiles, n_exp)[1] == 0)
    def _():
        hn_scr[...] = hn_ref[...]

    def step(a_next, a_cur, act_next, act_cur):
        zero = jnp.zeros((), BF16)
        blk = PEER_MXU_BLOCK

        def stage_a(rb, cb):
            rows, cols = slice(rb * blk, (rb + 1) * blk), slice(cb * blk, (cb + 1) * blk)
            a_next[rows, cols] = lax.dot_general(u_ref[rows, :], hn_scr[cols, :], NT_DIMS,
                                                 preferred_element_type=F32)

        def stage_b(t, i):
            lanes = slice(t * ROW_TILE, (t + 1) * ROW_TILE)
            wsum = None
            for hh in range(PEER_HEADS):
                keep = r2_ref[hh, t] < _pair_word_rows(n1_ref[hh, t, i:i + 1, :])
                term = jnp.where(keep, e2_ref[hh, t], zero) * _pair_word_rows(e1_ref[hh, t, i:i + 1, :])
                wsum = term if wsum is None else wsum + term
            rows = slice(i * N_KEYS, (i + 1) * N_KEYS)
            a = a_cur[rows, lanes]
            gelu = a * (1.0 + lax.erf(a * (2.0 ** -0.5)))
            act_next[rows, lanes] = gelu.astype(BF16) * wsum

        def stage_c(rb, cb):
            rows, cols = slice(rb * blk, (rb + 1) * blk), slice(cb * blk, (cb + 1) * blk)
            prev = jnp.where(e_c == 0, 0.0, acc_ref[rows, cols])
            acc_ref[rows, cols] = prev + jnp.dot(vt_ref[rows, :], act_cur[:, cols], preferred_element_type=F32)

        a_pieces = [(rb, cb) for cb in range(PEER_TOK_TILE // blk) for rb in range(PEER_EXP_TILE // blk)]
        c_pieces = [(rb, cb) for cb in range(PEER_TOK_TILE // blk) for rb in range(D_MODEL // blk)]
        b_pieces = [(t, i) for t in range(PEER_TOK_TILE // ROW_TILE) for i in range(PEER_EXP_TILE // N_KEYS)]
        a_per_c = len(a_pieces) // len(c_pieces)
        b_per_c = len(b_pieces) // len(c_pieces)
        assert a_per_c * len(c_pieces) == len(a_pieces) and b_per_c * len(c_pieces) == len(b_pieces)
        b_iter = iter(b_pieces)
        for k, c_piece in enumerate(c_pieces):
            mxu = [(stage_a, p) for p in a_pieces[k * a_per_c:(k + 1) * a_per_c]]
            mxu.insert(1, (stage_c, c_piece))
            quota = [b_per_c // len(mxu) + (1 if m < b_per_c % len(mxu) else 0) for m in range(len(mxu))]
            for (fn, piece), nb in zip(mxu, quota):
                fn(*piece)
                for _ in range(nb):
                    stage_b(*next(b_iter))

    step(a0_ref, a1_ref, act0_ref, act1_ref)
    a1_ref[...] = a0_ref[...]

    @pl.when(n >= 0)
    def _():
        act1_ref[...] = act0_ref[...]

    @pl.when((e_c == n_exp - 1) & (n >= PEER_PIPE_LAG))
    def _():
        g = _group_of_row(t_c * PEER_TOK_TILE, seq)
        mod = modv_ref[pl.ds(g, 1), :]
        y = x_ref[...] + mod[:, 5 * D_MODEL:6 * D_MODEL] * acc_ref[...].T
        if final:
            y = y * lax.rsqrt(jnp.mean(y * y, axis=-1, keepdims=True) + EPS) * fn_ref[...]
        o_ref[...] = y


def _peer_dense(xall, modv, hn, u_bf, vt_bf, n1, e1, r2, e2, final_norm, seq, rows_out, final):
    tt, et = PEER_TOK_TILE, PEER_EXP_TILE
    n_first = et // N_KEYS
    n_exp = N_EXPERTS // et
    n_tiles = (rows_out // tt) * n_exp
    tile = lambda lag: (lambda n: _peer_tile(n, lag, n_tiles, n_exp))
    ta, tb, tc = tile(0), tile(1), tile(PEER_PIPE_LAG)
    full = lambda *s: pl.BlockSpec(s, lambda n: (0,) * len(s))
    per_first = pl.BlockSpec((PEER_HEADS, tt // ROW_TILE, n_first, ROW_TILE), lambda n: (0, tb(n)[0], tb(n)[1], 0))
    per_second = pl.BlockSpec((PEER_HEADS, tt // ROW_TILE, N_KEYS, ROW_TILE), lambda n: (0, tb(n)[0], 0, 0))
    return pl.pallas_call(
        functools.partial(_peer_dense_kernel, seq=seq, final=final, n_tiles=n_tiles, n_exp=n_exp),
        grid=(n_tiles + PEER_PIPE_LAG,),
        in_specs=[pl.BlockSpec((tt, D_MODEL), lambda n: (tc(n)[0], 0)),
                  full(8, 6 * D_MODEL),
                  pl.BlockSpec((tt, D_MODEL), lambda n: (ta(n)[0], 0)),
                  pl.BlockSpec((et, D_MODEL), lambda n: (ta(n)[1], 0)),
                  pl.BlockSpec((D_MODEL, et), lambda n: (0, tc(n)[1])),
                  per_first, per_first, per_second, per_second, full(1, D_MODEL)],
        out_specs=pl.BlockSpec((tt, D_MODEL), lambda n: (tc(n)[0], 0)),
        out_shape=jax.ShapeDtypeStruct((rows_out, D_MODEL), F32),
        scratch_shapes=[pltpu.VMEM((D_MODEL, tt), F32), pltpu.VMEM((et, tt), F32), pltpu.VMEM((et, tt), F32),
                        pltpu.VMEM((et, tt), BF16), pltpu.VMEM((et, tt), BF16), pltpu.VMEM((tt, D_MODEL), BF16)],
        compiler_params=_cparams("arbitrary"),
        name="peer_dense",
    )(xall, modv, hn, u_bf, vt_bf, n1, e1, r2, e2, final_norm.reshape(1, D_MODEL))


def _reorder_in_weight(w):
    s = np.cumsum((0, 3 * DN_WIDTH, DN_WIDTH, POOL_WIDTH, 2 * DN_HEADS, 2 * DN_HEADS, D_MODEL, D_MODEL))
    qkv, z, pin, b, a, gd, gp = (w[:, s[i]:s[i + 1]] for i in range(7))
    pad = jnp.zeros((w.shape[0], BA_PAD - 4 * DN_HEADS), w.dtype)
    return jnp.concatenate([qkv, z, pin, gd, gp, b, a, pad], axis=1).astype(BF16)


def _forward(x, c, ctx, c_ctx, w_mod, b_mod, norm_mix, w_in, conv_w, a_log, dt_bias, dn_out_norm, pool_w, pool_scale,
             w_up_dn, w_up_pool, w_out, norm_ffn, peer_wq, peer_keys, peer_u, peer_v, final_norm):
    seq, nctx = x.shape[1], ctx.shape[1]
    nlat = BATCH * seq
    xall = jnp.concatenate([x.reshape(nlat, D_MODEL), ctx.reshape(BATCH * nctx, D_MODEL)], axis=0)
    rows = xall.shape[0]
    cvec = jnp.concatenate([c, c_ctx[None, :], jnp.zeros((8 - BATCH - 1, D_MODEL), F32)], axis=0)
    modv_all = _modulation(cvec, w_mod, b_mod)
    for i in range(DEPTH):
        last = i == DEPTH - 1
        rows_out = nlat if last else rows
        modv = modv_all[i]
        qkv, z, pin, gd, gp, ba = _inproj(xall, modv, norm_mix[i], _reorder_in_weight(w_in[i]), seq)
        q, k, v, bg = _dnconv(qkv, ba, conv_w[i], a_log[i], dt_bias[i], seq, nctx)
        o_f, o_b = _dnscan(*_dnprep(q, k, v, bg, seq, nctx), seq, nctx)
        xall = _merge(xall, o_f, o_b, z, pin, gd, gp, modv, dn_out_norm[i], pool_w[i], pool_scale[i],
                      w_up_dn[i].astype(BF16), w_up_pool[i].astype(BF16), w_out[i].astype(BF16), seq, nctx, rows_out)
        hn, n1, e1, r2, e2 = _peer_route(xall, modv, norm_ffn[i], peer_wq[i].astype(BF16), peer_keys[i], seq, rows_out)
        xall = _peer_dense(xall, modv, hn, peer_u[i].astype(BF16), peer_v[i].T.astype(BF16), n1, e1, r2, e2,
                           final_norm, seq, rows_out, last)
    return xall.reshape(BATCH, seq, D_MODEL)


def kernel(x, c, ctx, c_ctx, w_mod, b_mod, norm_mix, w_in, conv_w, a_log, dt_bias, dn_out_norm, pool_w, pool_scale, w_up_dn, w_up_pool, w_out, norm_ffn, peer_wq, peer_keys, peer_u, peer_v, final_norm):
    return _forward(x, c, ctx, c_ctx, w_mod, b_mod, norm_mix, w_in, conv_w, a_log, dt_bias, dn_out_norm, pool_w,
                    pool_scale, w_up_dn, w_up_pool, w_out, norm_ffn, peer_wq, peer_keys, peer_u, peer_v, final_norm)
```

```python
import functools

import numpy as np
import jax
import jax.numpy as jnp
from jax import lax
from jax.experimental import pallas as pl
from jax.experimental.pallas import tpu as pltpu

D_MODEL = 1024
BATCH = 2
DEPTH = 2
GRID_W = 64
EPS = 1e-6

DN_HEADS = 4
DN_HEAD_DIM = 128
DN_WIDTH = DN_HEADS * DN_HEAD_DIM
SHORT_CONV = 4
DN_CHUNK = 64
DN_CHAINS = 2 * DN_HEADS
DN_PREP_CHUNKS = 2
DN_SCAN_CHUNKS = 4
DN_LOCKSTEP = 16

POOL_WINDOWS = (2, 4, 8, 16)
POOL_GROUPS = 4
POOL_WIDTH = D_MODEL // 2
POOL_GW = POOL_WIDTH // POOL_GROUPS

PEER_HEADS = 8
N_KEYS = 128
N_EXPERTS = N_KEYS * N_KEYS
PEER_TOPK = 16
PEER_QDIM = 256
PEER_HALF = PEER_QDIM // 2

BA_PAD = 128
IN_COLS_R = 3 * DN_WIDTH + DN_WIDTH + POOL_WIDTH + 2 * D_MODEL + BA_PAD

LANES = 128
SUBLANES = 8
ROW_TILE = 256
INPROJ_TILE = 512
PEER_TOK_TILE = 512
PEER_EXP_TILE = 2048
PEER_SLAB = 256
HALO = 8
VMEM_LIMIT = 56 * 1024 * 1024

F32 = jnp.float32
BF16 = jnp.bfloat16
HIGHEST = lax.Precision.HIGHEST
NEG_INF = float("-inf")
NT_DIMS = (((1,), (1,)), ((), ()))


def _cparams(*sem):
    return pltpu.CompilerParams(dimension_semantics=sem, vmem_limit_bytes=VMEM_LIMIT)


def _group_of_row(row0, seq):
    return jnp.where(row0 < seq, 0, jnp.where(row0 < 2 * seq, 1, 2))


def _silu(x):
    return x * jax.nn.sigmoid(x)


def _split_bf16(a):
    hi = a.astype(BF16)
    lo = (a - hi.astype(F32)).astype(BF16)
    return hi, lo


def _bf16_pair_word(x):
    hi = lax.bitcast_convert_type(x.astype(BF16).astype(F32), jnp.uint32)
    return hi | (hi >> 16)


def _pair_word_rows(row):
    tile = pltpu.bitcast(jnp.broadcast_to(row, (SUBLANES, row.shape[1])), BF16)
    return jnp.concatenate([tile] * (N_KEYS // tile.shape[0]), axis=0)


def _dot_split(a, b):
    ah, al = _split_bf16(a)
    bh, bl = _split_bf16(b)
    a4 = jnp.concatenate([ah, al, ah, al], axis=1)
    b4 = jnp.concatenate([bh, bh, bl, bl], axis=0)
    return jnp.dot(a4, b4, preferred_element_type=F32)


def _mod_kernel(c_ref, w_ref, b_ref, o_ref):
    o_ref[0] = jnp.dot(_silu(c_ref[...]), w_ref[0], preferred_element_type=F32) + b_ref[0]


def _modulation(cvec, w_mod, b_mod):
    tn = 1536
    return pl.pallas_call(
        _mod_kernel,
        grid=(DEPTH, 6 * D_MODEL // tn),
        in_specs=[
            pl.BlockSpec((8, D_MODEL), lambda l, j: (0, 0)),
            pl.BlockSpec((1, D_MODEL, tn), lambda l, j: (l, 0, j)),
            pl.BlockSpec((1, 1, tn), lambda l, j: (l, 0, j)),
        ],
        out_specs=pl.BlockSpec((1, 8, tn), lambda l, j: (l, 0, j)),
        out_shape=jax.ShapeDtypeStruct((DEPTH, 8, 6 * D_MODEL), F32),
        compiler_params=_cparams("arbitrary", "arbitrary"),
        name="modulation",
    )(cvec, w_mod, b_mod.reshape(DEPTH, 1, 6 * D_MODEL))


def _norm_mod(x, nw, mod, k):
    ms = jnp.mean(x * x, axis=-1, keepdims=True)
    xn = x * lax.rsqrt(ms + EPS) * nw
    sh = mod[:, k * D_MODEL:(k + 1) * D_MODEL]
    sc = mod[:, (k + 1) * D_MODEL:(k + 2) * D_MODEL]
    return xn * (1 + sc) + sh


def _inproj_kernel(x_ref, modv_ref, nw_ref, w_ref, qkv_ref, z_ref, pin_ref, gd_ref, gp_ref, ba_ref, *, seq):
    g = _group_of_row(pl.program_id(0) * INPROJ_TILE, seq)
    mod = modv_ref[pl.ds(g, 1), :]
    h = _norm_mod(x_ref[...], nw_ref[...], mod, 0)
    y = jnp.dot(h.astype(BF16), w_ref[...], preferred_element_type=F32)
    o = 0
    for ref in (qkv_ref, z_ref, pin_ref, gd_ref, gp_ref, ba_ref):
        n = ref.shape[1]
        ref[...] = y[:, o:o + n]
        o += n


def _inproj(xall, modv, norm_w, w_in_r, seq):
    rows = xall.shape[0]
    widths = (3 * DN_WIDTH, DN_WIDTH, POOL_WIDTH, D_MODEL, D_MODEL, BA_PAD)
    return pl.pallas_call(
        functools.partial(_inproj_kernel, seq=seq),
        grid=(rows // INPROJ_TILE,),
        in_specs=[
            pl.BlockSpec((INPROJ_TILE, D_MODEL), lambda i: (i, 0)),
            pl.BlockSpec((8, 6 * D_MODEL), lambda i: (0, 0)),
            pl.BlockSpec((1, D_MODEL), lambda i: (0, 0)),
            pl.BlockSpec((D_MODEL, IN_COLS_R), lambda i: (0, 0)),
        ],
        out_specs=[pl.BlockSpec((INPROJ_TILE, n), lambda i: (i, 0)) for n in widths],
        out_shape=[jax.ShapeDtypeStruct((rows, n), F32) for n in widths],
        compiler_params=_cparams("arbitrary"),
        name="inproj",
    )(xall, modv, norm_w.reshape(1, D_MODEL), w_in_r)


def _dnconv_kernel(cur_ref, prev_ref, next_ref, cw_ref, ba_ref, alog_ref, dtb_ref,
                   q_ref, k_ref, v_ref, bg_ref, *, seq, ctx):
    row0 = pl.program_id(0) * ROW_TILE
    nlat = BATCH * seq
    is_start = (row0 == 0) | (row0 == seq) | (row0 == nlat) | (row0 == nlat + ctx)
    row1 = row0 + ROW_TILE
    is_end = (row1 == seq) | (row1 == nlat) | (row1 == nlat + ctx) | (row1 == nlat + BATCH * ctx)
    prev = jnp.where(is_start, 0.0, prev_ref[...])
    nxt = jnp.where(is_end, 0.0, next_ref[...])
    ext = jnp.concatenate([prev, cur_ref[...], nxt], axis=0)
    left = SHORT_CONV // 2
    cw = cw_ref[...]
    y = None
    for j in range(SHORT_CONV):
        o = HALO - left + j
        term = ext[o:o + ROW_TILE, :] * cw[j:j + 1, :]
        y = term if y is None else y + term
    y = _silu(y)
    for hh in range(DN_HEADS):
        lo, hi = hh * DN_HEAD_DIM, (hh + 1) * DN_HEAD_DIM
        qh = y[:, lo:hi]
        kh = y[:, DN_WIDTH + lo:DN_WIDTH + hi]
        q_ref[:, lo:hi] = qh * lax.rsqrt(jnp.sum(qh * qh, axis=-1, keepdims=True) + EPS) * (DN_HEAD_DIM ** -0.5)
        k_ref[:, lo:hi] = kh * lax.rsqrt(jnp.sum(kh * kh, axis=-1, keepdims=True) + EPS)
    v_ref[...] = y[:, 2 * DN_WIDTH:]
    ba = ba_ref[...]
    beta = jax.nn.sigmoid(ba)
    xs = ba + dtb_ref[...]
    softplus = jnp.maximum(xs, 0.0) + jnp.log(1.0 + jnp.exp(-jnp.abs(xs)))
    gdec = -jnp.exp(alog_ref[...]) * softplus
    col = lax.broadcasted_iota(jnp.int32, ba.shape, 1)
    bg_ref[...] = jnp.where(col < 2 * DN_HEADS, beta, jnp.where(col < 4 * DN_HEADS, gdec, 0.0))


def _dnconv(qkv, ba, conv_w, a_log, dt_bias, seq, ctx):
    rows = qkv.shape[0]
    nh = ROW_TILE // HALO
    last = rows // HALO - 1
    pad = jnp.zeros((2 * DN_HEADS,), F32)
    tail = jnp.zeros((BA_PAD - 4 * DN_HEADS,), F32)
    alog = jnp.concatenate([pad, a_log.reshape(-1), tail]).reshape(1, BA_PAD)
    dtb = jnp.concatenate([pad, dt_bias.reshape(-1), tail]).reshape(1, BA_PAD)
    w3 = 3 * DN_WIDTH
    return pl.pallas_call(
        functools.partial(_dnconv_kernel, seq=seq, ctx=ctx),
        grid=(rows // ROW_TILE,),
        in_specs=[
            pl.BlockSpec((ROW_TILE, w3), lambda i: (i, 0)),
            pl.BlockSpec((HALO, w3), lambda i: (jnp.maximum(i * nh - 1, 0), 0)),
            pl.BlockSpec((HALO, w3), lambda i: (jnp.minimum((i + 1) * nh, last), 0)),
            pl.BlockSpec((SHORT_CONV, w3), lambda i: (0, 0)),
            pl.BlockSpec((ROW_TILE, BA_PAD), lambda i: (i, 0)),
            pl.BlockSpec((1, BA_PAD), lambda i: (0, 0)),
            pl.BlockSpec((1, BA_PAD), lambda i: (0, 0)),
        ],
        out_specs=[pl.BlockSpec((ROW_TILE, DN_WIDTH), lambda i: (i, 0))] * 3
        + [pl.BlockSpec((ROW_TILE, BA_PAD), lambda i: (i, 0))],
        out_shape=[jax.ShapeDtypeStruct((rows, DN_WIDTH), F32)] * 3
        + [jax.ShapeDtypeStruct((rows, BA_PAD), F32)],
        compiler_params=_cparams("arbitrary"),
        name="dnconv",
    )(qkv, qkv, qkv, conv_w, ba, alog, dtb)


def _dnprep_kernel(q_ref, k_ref, v_ref, bg_ref, bgt_ref, wq_ref, u_ref, lk_ref, eg_ref):
    c = DN_CHUNK
    ri = lax.broadcasted_iota(jnp.int32, (c, c), 0)
    ci = lax.broadcasted_iota(jnp.int32, (c, c), 1)
    dirs = ((ri >= ci, ri > ci, ci >= ri, c - 1), (ri <= ci, ri < ci, ci <= ri, 0))
    chains = []
    for j in range(DN_PREP_CHUNKS):
        rows = slice(j * c, (j + 1) * c)
        bg = bg_ref[rows, :]
        bgh, bgl = _split_bf16(bg)
        bgth, bgtl = _split_bf16(bgt_ref[j])
        bg2 = jnp.concatenate([bgh, bgl], axis=0)
        bgt2 = jnp.concatenate([bgth, bgtl], axis=1)
        egs = []
        for d, (incl, strict, incl_t, last) in enumerate(dirs):
            m = incl.astype(BF16)
            mt = incl_t.astype(BF16)
            gc_all = jnp.dot(jnp.concatenate([m, m], axis=1), bg2, preferred_element_type=F32)
            gr_all = jnp.dot(bgt2, jnp.concatenate([mt, mt], axis=0), preferred_element_type=F32)
            for hh in range(DN_HEADS):
                lo, hi = hh * DN_HEAD_DIM, (hh + 1) * DN_HEAD_DIM
                ch = d * DN_HEADS + hh
                gcol = 2 * DN_HEADS + ch
                q = q_ref[rows, lo:hi]
                k = k_ref[rows, lo:hi]
                v = v_ref[rows, lo:hi]
                beta = bg[:, ch:ch + 1]
                gc = gc_all[:, gcol:gcol + 1]
                gr = gr_all[gcol:gcol + 1, :]
                glast = gc_all[last:last + 1, gcol:gcol + 1]
                decay = jnp.where(incl, jnp.exp(jnp.minimum(gc - gr, 0.0)), 0.0)
                kk = lax.dot_general(k, k, NT_DIMS, preferred_element_type=F32)
                qk = lax.dot_general(q, k, NT_DIMS, preferred_element_type=F32)
                egc = jnp.exp(gc)
                cols = slice(ch * DN_HEAD_DIM, (ch + 1) * DN_HEAD_DIM)
                wq_ref[0, j, c:2 * c, cols] = (q * egc).astype(BF16)
                lk_ref[0, j, ch, 0:c, :] = jnp.where(incl, qk * decay, 0.0).astype(BF16)
                lk_ref[0, j, ch, c:, :] = (k * jnp.exp(glast - gc)).T.astype(BF16)
                egs.append(jnp.broadcast_to(jnp.exp(glast), (1, LANES)))
                chains.append((j, cols, jnp.where(strict, -(beta * kk * decay), 0.0),
                               jnp.concatenate([k * (beta * egc), v * beta], axis=1)))
        eg_ref[0, j] = jnp.concatenate(egs, axis=0)
    for g0 in range(0, len(chains), DN_LOCKSTEP):
        group = chains[g0:g0 + DN_LOCKSTEP]
        ps = [ch[2] for ch in group]
        eye = (ri == ci).astype(F32)
        xs = [eye + p for p in ps]
        for _ in range(int(np.log2(c)) - 1):
            ps = [_dot_split(p, p) for p in ps]
            xs = [x + _dot_split(x, p) for x, p in zip(xs, ps)]
        ys = [_dot_split(x, ch[3]) for x, ch in zip(xs, group)]
        for (j, cols, _, _), y in zip(group, ys):
            wq_ref[0, j, 0:c, cols] = y[:, :DN_HEAD_DIM].astype(BF16)
            u_ref[0, j, :, cols] = y[:, DN_HEAD_DIM:]


def _dnprep(q, k, v, bg, seq, ctx):
    rows = q.shape[0]
    c = DN_CHUNK
    n = DN_PREP_CHUNKS
    nct, nlt = ctx // c, seq // c
    assert nct % n == 0 and nlt % n == 0
    nch = nct + nlt
    bgt = bg[:, :4 * DN_HEADS].reshape(rows // c, c, 4 * DN_HEADS).transpose(0, 2, 1)

    def seq_pos(i):
        i = i * n
        lat = i < BATCH * nlt
        j = i - BATCH * nlt
        return jnp.where(lat, i // nlt, j // nct), jnp.where(lat, nct + i % nlt, j % nct) // n

    def out_spec(*tail):
        return pl.BlockSpec((1, n) + tail, lambda i: seq_pos(i) + (0,) * len(tail))

    wide = pl.BlockSpec((n * c, DN_WIDTH), lambda i: (i, 0))
    width = DN_CHAINS * DN_HEAD_DIM
    return pl.pallas_call(
        _dnprep_kernel,
        grid=(rows // (n * c),),
        in_specs=[wide, wide, wide,
                  pl.BlockSpec((n * c, BA_PAD), lambda i: (i, 0)),
                  pl.BlockSpec((n, 4 * DN_HEADS, c), lambda i: (i, 0, 0))],
        out_specs=[out_spec(2 * c, width), out_spec(c, width), out_spec(DN_CHAINS, c + DN_HEAD_DIM, c),
                   out_spec(DN_CHAINS, LANES)],
        out_shape=[jax.ShapeDtypeStruct((BATCH, nch, 2 * c, width), BF16),
                   jax.ShapeDtypeStruct((BATCH, nch, c, width), F32),
                   jax.ShapeDtypeStruct((BATCH, nch, DN_CHAINS, c + DN_HEAD_DIM, c), BF16),
                   jax.ShapeDtypeStruct((BATCH, nch, DN_CHAINS, LANES), F32)],
        compiler_params=_cparams("arbitrary"),
        name="dnprep",
    )(q, k, v, bg, bgt)


def _dnscan_kernel(wqf_ref, uf_ref, lkf_ref, egf_ref, wqb_ref, ub_ref, lkb_ref, egb_ref, of_ref, ob_ref, s_ref):
    @pl.when(pl.program_id(0) == 0)
    def _():
        s_ref[...] = jnp.zeros_like(s_ref)

    c = DN_CHUNK
    dirs = ((wqf_ref, uf_ref, lkf_ref, egf_ref, of_ref), (wqb_ref, ub_ref, lkb_ref, egb_ref, ob_ref))
    chains = [(d, b, hh) for d in range(2) for b in range(BATCH) for hh in range(DN_HEADS)]
    sidx = lambda d, b, hh: (b * 2 + d) * DN_HEADS + hh
    cols = lambda hh: slice(hh * DN_HEAD_DIM, (hh + 1) * DN_HEAD_DIM)
    for j in range(DN_SCAN_CHUNKS):
        pos = (j, DN_SCAN_CHUNKS - 1 - j)
        ss = [s_ref[sidx(*ch)] for ch in chains]
        r1 = [jnp.dot(dirs[d][0][b, pos[d], :, cols(hh)], s.astype(BF16), preferred_element_type=F32)
              for (d, b, hh), s in zip(chains, ss)]
        vn = [dirs[d][1][b, pos[d], :, cols(hh)] - r[:c] for (d, b, hh), r in zip(chains, r1)]
        r2 = [jnp.dot(dirs[d][2][b, pos[d], hh], v.astype(BF16), preferred_element_type=F32)
              for (d, b, hh), v in zip(chains, vn)]
        for (d, b, hh), s, a1, a2 in zip(chains, ss, r1, r2):
            dirs[d][4][b, pos[d], :, cols(hh)] = a1[c:] + a2[:c]
            row = d * DN_HEADS + hh
            s_ref[sidx(d, b, hh)] = s * dirs[d][3][b, pos[d], row:row + 1, :] + a2[c:]


def _dnscan(wq, u, lk, eg, seq, ctx):
    c = DN_CHUNK
    n = DN_SCAN_CHUNKS
    nct, nlt = ctx // c, seq // c
    assert nct % n == 0 and nlt % n == 0
    nch = nct + nlt
    nblk, nctb = nch // n, nct // n

    def bwd(s):
        return jnp.where(s < nctb, nctb - 1 - s, nctb + nblk - 1 - s)

    def specs(pos, d):
        return [pl.BlockSpec((BATCH, n, 2 * c, DN_WIDTH), lambda s: (0, pos(s), 0, d)),
                pl.BlockSpec((BATCH, n, c, DN_WIDTH), lambda s: (0, pos(s), 0, d)),
                pl.BlockSpec((BATCH, n, DN_HEADS, c + DN_HEAD_DIM, c), lambda s: (0, pos(s), d, 0, 0)),
                pl.BlockSpec((BATCH, n, DN_CHAINS, LANES), lambda s: (0, pos(s), 0, 0))]

    fwd = lambda s: s
    return pl.pallas_call(
        _dnscan_kernel,
        grid=(nblk,),
        in_specs=specs(fwd, 0) + specs(bwd, 1),
        out_specs=[pl.BlockSpec((BATCH, n, c, DN_WIDTH), lambda s: (0, s, 0, 0)),
                   pl.BlockSpec((BATCH, n, c, DN_WIDTH), lambda s: (0, bwd(s), 0, 0))],
        out_shape=[jax.ShapeDtypeStruct((BATCH, nch, c, DN_WIDTH), F32)] * 2,
        scratch_shapes=[pltpu.VMEM((BATCH * DN_CHAINS, DN_HEAD_DIM, DN_HEAD_DIM), F32)],
        compiler_params=_cparams("arbitrary"),
        name="dnscan",
    )(wq, u, lk, eg, wq, u, lk, eg)


def _pool_matrices(tile, seg):
    t = np.arange(tile)
    p = t % seg
    mats = []
    for win in POOL_WINDOWS:
        lo = np.clip(p - win // 2, 0, seg)
        hi = np.clip(p + win - win // 2, 0, seg)
        same = (t[:, None] // seg) == (t[None, :] // seg)
        inside = same & (p[None, :] >= lo[:, None]) & (p[None, :] < hi[:, None])
        mats.append(inside / (hi - lo)[:, None].astype(np.float64) - np.eye(tile))
    return np.stack(mats).astype(np.float32)


def _merge_kernel(x_ref, of_ref, ob_ref, z_ref, pin_ref, gd_ref, gp_ref, modv_ref, on_ref, pm_ref, pw_ref, ps_ref,
                  wud_ref, wup_ref, wo_ref, o_ref, *, seq):
    g = _group_of_row(pl.program_id(0) * ROW_TILE, seq)
    mod = modv_ref[pl.ds(g, 1), :]
    gate = mod[:, 2 * D_MODEL:3 * D_MODEL]
    o = (of_ref[0] + ob_ref[0]).reshape(ROW_TILE, DN_WIDTH)
    z = z_ref[...]
    onw = on_ref[...]
    ys = []
    for hh in range(DN_HEADS):
        lo, hi = hh * DN_HEAD_DIM, (hh + 1) * DN_HEAD_DIM
        oh = o[:, lo:hi]
        ys.append(oh * lax.rsqrt(jnp.mean(oh * oh, axis=-1, keepdims=True) + EPS) * onw * _silu(z[:, lo:hi]))
    y_dn = jnp.concatenate(ys, axis=1)
    pin = pin_ref[...]
    yp = []
    for gi in range(POOL_GROUPS):
        lo, hi = gi * POOL_GW, (gi + 1) * POOL_GW
        pooled = jnp.dot(pm_ref[0, gi], pin[:, lo:hi], precision=HIGHEST, preferred_element_type=F32)
        yp.append(jnp.dot(pooled, pw_ref[gi], preferred_element_type=F32))
    y_pool = jnp.concatenate(yp, axis=1) * ps_ref[...]
    m = (jax.nn.sigmoid(gd_ref[...]) * jnp.dot(y_dn.astype(BF16), wud_ref[...], preferred_element_type=F32)
         + jax.nn.sigmoid(gp_ref[...]) * jnp.dot(y_pool.astype(BF16), wup_ref[...], preferred_element_type=F32))
    out = jnp.dot(m.astype(BF16), wo_ref[...], preferred_element_type=F32)
    o_ref[...] = x_ref[...] + gate * out


def _merge(xall, o_f, o_b, z, pin, gd, gp, modv, onorm, pool_w, pool_scale, wud, wup, wo, seq, ctx, rows_out):
    assert ctx == ROW_TILE and seq % ROW_TILE == 0
    lat_tiles = BATCH * seq // ROW_TILE
    tiles_per_seq = seq // ROW_TILE
    cpt = ROW_TILE // DN_CHUNK
    pm = jnp.asarray(np.stack([_pool_matrices(ROW_TILE, GRID_W), _pool_matrices(ROW_TILE, ctx)]))
    row = lambda n: pl.BlockSpec((ROW_TILE, n), lambda i: (i, 0))
    full = lambda *s: pl.BlockSpec(s, lambda i: (0,) * len(s))

    def o_pos(i):
        lat = i < lat_tiles
        return (jnp.where(lat, i // tiles_per_seq, i - lat_tiles), jnp.where(lat, 1 + i % tiles_per_seq, 0), 0, 0)

    o_spec = pl.BlockSpec((1, cpt, DN_CHUNK, DN_WIDTH), o_pos)
    return pl.pallas_call(
        functools.partial(_merge_kernel, seq=seq),
        grid=(rows_out // ROW_TILE,),
        in_specs=[
            row(D_MODEL), o_spec, o_spec, row(DN_WIDTH), row(POOL_WIDTH), row(D_MODEL), row(D_MODEL),
            full(8, 6 * D_MODEL), full(1, DN_HEAD_DIM),
            pl.BlockSpec((1, POOL_GROUPS, ROW_TILE, ROW_TILE), lambda i: (jnp.where(i >= lat_tiles, 1, 0), 0, 0, 0)),
            full(POOL_GROUPS, POOL_GW, POOL_GW), full(1, POOL_WIDTH),
            full(DN_WIDTH, D_MODEL), full(POOL_WIDTH, D_MODEL), full(D_MODEL, D_MODEL),
        ],
        out_specs=row(D_MODEL),
        out_shape=jax.ShapeDtypeStruct((rows_out, D_MODEL), F32),
        compiler_params=_cparams("arbitrary"),
        name="merge",
    )(xall, o_f, o_b, z, pin, gd, gp, modv, onorm.reshape(1, DN_HEAD_DIM), pm, pool_w,
      pool_scale.reshape(1, POOL_WIDTH), wud, wup, wo)


def _col_max(x):
    return jnp.max(x, axis=0, keepdims=True)


def _col_min(x):
    return jnp.min(x, axis=0, keepdims=True)


def _oddeven_merge(lo, hi, r):
    step = r * 2
    if step < hi - lo:
        yield from _oddeven_merge(lo, hi, step)
        yield from _oddeven_merge(lo + r, hi, step)
        yield from [(i, i + r) for i in range(lo + r, hi - r, step)]
    else:
        yield (lo, lo + r)


def _oddeven_merge_sort(lo, hi):
    if hi - lo >= 1:
        mid = lo + (hi - lo) // 2
        yield from _oddeven_merge_sort(lo, mid)
        yield from _oddeven_merge_sort(mid + 1, hi)
        yield from _oddeven_merge(lo, hi, 1)


def _exchange(v, i, j):
    v[i], v[j] = jnp.maximum(v[i], v[j]), jnp.minimum(v[i], v[j])


def _sorted_top_many(block_lists):
    k = PEER_TOPK
    vs = [list(blocks) for blocks in block_lists]
    for i, j in _oddeven_merge_sort(0, k - 1):
        for v in vs:
            _exchange(v, i, j)
    shift = SUBLANES // 2
    while shift >= 1:
        ws = [[pltpu.roll(x, shift, 0) for x in v] for v in vs]
        vs = [[jnp.maximum(v[j], w[k - 1 - j]) for j in range(k)] for v, w in zip(vs, ws)]
        stride = k // 2
        while stride >= 1:
            for i in range(k):
                if i & stride == 0:
                    for v in vs:
                        _exchange(v, i, i + stride)
            stride //= 2
        shift //= 2
    return vs


def _sublane_total(x):
    shift = SUBLANES // 2
    while shift >= 1:
        x = x + pltpu.roll(x, shift, 0)
        shift //= 2
    return x


def _unambiguous(blocks, top):
    ok = top[0] > top[1]
    for a in range(1, PEER_TOPK - 1):
        ok = ok & (top[a] > top[a + 1])
    ge = None
    for blk in blocks:
        one = jnp.where(blk >= top[PEER_TOPK - 1], 1.0, 0.0)
        ge = one if ge is None else ge + one
    return ok & (_sublane_total(ge) == float(PEER_TOPK))


def _top_select(scores, iota):
    n = float(scores[0].shape[0])
    cur = list(scores)
    idxs = [[] for _ in cur]
    for _ in range(PEER_TOPK):
        for j in range(len(cur)):
            m = _col_max(cur[j])
            cand = jnp.where(cur[j] == m, iota, n)
            idx = _col_min(cand)
            cur[j] = jnp.where(cand == idx, NEG_INF, cur[j])
            idxs[j].append(idx)
    return [jnp.concatenate(i, axis=0) for i in idxs]


def _peer_route_kernel(x_ref, modv_ref, nw_ref, wq_ref, keys_ref, hn_ref, n1_ref, e1_ref, r2_ref, e2_ref,
                       q_scr, *, seq):
    g = _group_of_row(pl.program_id(0) * ROW_TILE, seq)
    mod = modv_ref[pl.ds(g, 1), :]
    hn = _norm_mod(x_ref[...], nw_ref[...], mod, 3).astype(BF16)
    hn_ref[...] = hn
    q = jnp.dot(hn, wq_ref[...], preferred_element_type=F32)
    nsub = ROW_TILE // LANES
    for j in range(2 * PEER_HEADS):
        for t in range(nsub):
            q_scr[j, t] = q[t * LANES:(t + 1) * LANES, j * PEER_HALF:(j + 1) * PEER_HALF]
    k = PEER_TOPK
    nblk = N_KEYS // SUBLANES
    iota_k = lax.broadcasted_iota(jnp.int32, (N_KEYS, LANES), 0).astype(F32)
    iota_a = lax.broadcasted_iota(jnp.int32, (k, LANES), 0).astype(F32)
    iota_s = lax.broadcasted_iota(jnp.int32, (SUBLANES, LANES), 0).astype(F32)
    split = lambda s: [s[SUBLANES * j:SUBLANES * (j + 1), :] for j in range(nblk)]
    rep = lambda row: jnp.broadcast_to(row, (SUBLANES, LANES))

    def head(hh, carry):
        subs = range(nsub)
        s1 = [lax.dot_general(keys_ref[2 * hh], q_scr[2 * hh, t], NT_DIMS, preferred_element_type=F32) for t in subs]
        s2 = [lax.dot_general(keys_ref[2 * hh + 1], q_scr[2 * hh + 1, t], NT_DIMS, preferred_element_type=F32)
              for t in subs]
        b1, b2 = [split(x) for x in s1], [split(x) for x in s2]
        tops = _sorted_top_many(b1 + b2)
        top1, top2 = tops[:nsub], tops[nsub:]
        c = [jnp.concatenate([x[0:1, :] for x in top], axis=0) for top in top1]
        d = [jnp.concatenate([x[0:1, :] for x in top], axis=0) for top in top2]

        def grid_cells(c, d):
            cells = [c[0:1, :] + d[0:SUBLANES, :], c[0:1, :] + d[SUBLANES:k, :]]
            for a in range(1, SUBLANES):
                cells.append(jnp.where(iota_s < float(k // (a + 1)), c[a:a + 1, :] + d[0:SUBLANES, :], NEG_INF))
            cells.append(c[SUBLANES:k, :] + d[0:1, :])
            return cells + [jnp.full((SUBLANES, LANES), NEG_INF, F32)] * (k - len(cells))

        cuts = [x[k - 1][0:1, :] for x in _sorted_top_many([grid_cells(c[t], d[t]) for t in subs])]
        cnt_cut = [jnp.zeros((k, LANES), F32) for _ in subs]
        for b in range(k):
            cnt_cut = [acc + jnp.where(c[t] + d[t][b:b + 1, :] >= cuts[t], 1.0, 0.0) for t, acc in enumerate(cnt_cut)]

        def by_merge():
            cnts = [jnp.zeros((k, LANES), F32) for _ in subs]
            for _ in range(k):
                for t in subs:
                    dn = jnp.full((k, LANES), NEG_INF, F32)
                    for b in range(k):
                        dn = jnp.where(cnts[t] == float(b), d[t][b:b + 1, :], dn)
                    f = c[t] + dn
                    cand = jnp.where(f == _col_max(f), iota_a, float(k))
                    cnts[t] = cnts[t] + (cand == _col_min(cand)).astype(F32)
            return tuple(cnts)

        exact_cut = jnp.all(jnp.concatenate([jnp.sum(x, axis=0, keepdims=True) for x in cnt_cut], axis=0) == float(k))
        cnt = lax.cond(exact_cut, lambda: tuple(cnt_cut), by_merge)
        zsum = []
        for t in subs:
            e1c = jnp.exp(c[t] - c[t][0:1, :])
            e2d = jnp.exp(d[t] - d[t][0:1, :])
            part = jnp.zeros((k, LANES), F32)
            for b in range(k):
                part = part + jnp.where(cnt[t] > float(b), e2d[b:b + 1, :], 0.0)
            zsum.append(jnp.sum(e1c * part, axis=0, keepdims=True))

        def by_value():
            n1b = [[jnp.zeros((SUBLANES, LANES), F32)] * nblk for _ in subs]
            r2b = [[jnp.full((SUBLANES, LANES), float(k), F32)] * nblk for _ in subs]
            for a in range(k):
                for t in subs:
                    cnt_a = rep(cnt[t][a:a + 1, :])
                    n1b[t] = [jnp.where(blk == top1[t][a], cnt_a, acc) for blk, acc in zip(b1[t], n1b[t])]
                    r2b[t] = [jnp.where(blk == top2[t][a], float(a), acc) for blk, acc in zip(b2[t], r2b[t])]
            return tuple(jnp.concatenate(x, axis=0) for x in n1b + r2b)

        def by_index():
            idx = _top_select(s1 + s2, iota_k)
            out1, out2 = [], []
            for t in subs:
                n1 = jnp.zeros((N_KEYS, LANES), F32)
                r2 = jnp.full((N_KEYS, LANES), float(k), F32)
                for a in range(k):
                    n1 = jnp.where(iota_k == idx[t][a:a + 1, :], cnt[t][a:a + 1, :], n1)
                    r2 = jnp.where(iota_k == idx[nsub + t][a:a + 1, :], float(a), r2)
                out1.append(n1)
                out2.append(r2)
            return tuple(out1 + out2)

        clean = None
        for t in subs:
            ok = _unambiguous(b1[t], top1[t]) & _unambiguous(b2[t], top2[t])
            clean = ok if clean is None else clean & ok
        ranks = lax.cond(jnp.all(clean), by_value, by_index)
        for t in subs:
            lanes = slice(t * LANES, (t + 1) * LANES)
            n1_ref[hh, 0, :, lanes] = _bf16_pair_word(ranks[t])
            e1_ref[hh, 0, :, lanes] = _bf16_pair_word(0.5 * jnp.exp(s1[t] - c[t][0:1, :]) / zsum[t])
            r2_ref[hh, 0, :, lanes] = ranks[nsub + t].astype(BF16)
            e2_ref[hh, 0, :, lanes] = jnp.exp(s2[t] - d[t][0:1, :]).astype(BF16)
        return carry

    lax.fori_loop(0, PEER_HEADS, head, 0)


def _peer_route(xall, modv, norm_w, wq, keys, seq, rows_out):
    row = lambda n: pl.BlockSpec((ROW_TILE, n), lambda i: (i, 0))
    full = lambda *s: pl.BlockSpec(s, lambda i: (0,) * len(s))
    tab = pl.BlockSpec((PEER_HEADS, 1, N_KEYS, ROW_TILE), lambda i: (0, i, 0, 0))
    tab_shape = lambda dt: jax.ShapeDtypeStruct((PEER_HEADS, rows_out // ROW_TILE, N_KEYS, ROW_TILE), dt)
    return pl.pallas_call(
        functools.partial(_peer_route_kernel, seq=seq),
        grid=(rows_out // ROW_TILE,),
        in_specs=[row(D_MODEL), full(8, 6 * D_MODEL), full(1, D_MODEL), full(D_MODEL, PEER_HEADS * PEER_QDIM),
                  full(2 * PEER_HEADS, N_KEYS, PEER_HALF)],
        out_specs=[row(D_MODEL), tab, tab, tab, tab],
        out_shape=[jax.ShapeDtypeStruct((rows_out, D_MODEL), BF16), tab_shape(jnp.uint32), tab_shape(jnp.uint32),
                   tab_shape(BF16), tab_shape(BF16)],
        scratch_shapes=[pltpu.VMEM((2 * PEER_HEADS, ROW_TILE // LANES, LANES, PEER_HALF), F32)],
        compiler_params=_cparams("arbitrary"),
        name="peer_route",
    )(xall, modv, norm_w.reshape(1, D_MODEL), wq, keys.reshape(2 * PEER_HEADS, N_KEYS, PEER_HALF))


PEER_PIPE_LAG = 2
PEER_MXU_BLOCK = 256


def _peer_tile(n, lag, n_tiles, n_exp):
    m = jnp.clip(n - lag, 0, n_tiles - 1)
    return m // n_exp, m % n_exp


def _peer_dense_kernel(x_ref, modv_ref, hn_ref, u_ref, vt_ref, n1_ref, e1_ref, r2_ref, e2_ref, fn_ref, o_ref,
                       acc_ref, a0_ref, a1_ref, act0_ref, act1_ref, hn_scr, *, seq, final, n_tiles, n_exp):
    n = pl.program_id(0)
    t_c, e_c = _peer_tile(n, PEER_PIPE_LAG, n_tiles, n_exp)

    @pl.when(n == 0)
    def _():
        for ref in (acc_ref, a0_ref, a1_ref, act0_ref, act1_ref):
            ref[...] = jnp.zeros_like(ref)

    @pl.when(_peer_tile(n, 0, n_tiles, n_exp)[1] == 0)
    def _():
        hn_scr[...] = hn_ref[...]

    def step(a_next, a_cur, act_next, act_cur):
        zero = jnp.zeros((), BF16)
        blk = PEER_MXU_BLOCK

        def stage_a(rb, cb):
            rows, cols = slice(rb * blk, (rb + 1) * blk), slice(cb * blk, (cb + 1) * blk)
            a_next[rows, cols] = lax.dot_general(u_ref[rows, :], hn_scr[cols, :], NT_DIMS,
                                                 preferred_element_type=F32)

        def stage_b(t, i):
            lanes = slice(t * ROW_TILE, (t + 1) * ROW_TILE)
            wsum = None
            for hh in range(PEER_HEADS):
                keep = r2_ref[hh, t] < _pair_word_rows(n1_ref[hh, t, i:i + 1, :])
                term = jnp.where(keep, e2_ref[hh, t], zero) * _pair_word_rows(e1_ref[hh, t, i:i + 1, :])
                wsum = term if wsum is None else wsum + term
            rows = slice(i * N_KEYS, (i + 1) * N_KEYS)
            a = a_cur[rows, lanes]
            gelu = a * (1.0 + lax.erf(a * (2.0 ** -0.5)))
            act_next[rows, lanes] = gelu.astype(BF16) * wsum

        def stage_c(rb, cb):
            rows, cols = slice(rb * blk, (rb + 1) * blk), slice(cb * blk, (cb + 1) * blk)
            prev = jnp.where(e_c == 0, 0.0, acc_ref[rows, cols])
            acc_ref[rows, cols] = prev + jnp.dot(vt_ref[rows, :], act_cur[:, cols], preferred_element_type=F32)

        a_pieces = [(rb, cb) for cb in range(PEER_TOK_TILE // blk) for rb in range(PEER_EXP_TILE // blk)]
        c_pieces = [(rb, cb) for cb in range(PEER_TOK_TILE // blk) for rb in range(D_MODEL // blk)]
        b_pieces = [(t, i) for t in range(PEER_TOK_TILE // ROW_TILE) for i in range(PEER_EXP_TILE // N_KEYS)]
        a_per_c = len(a_pieces) // len(c_pieces)
        b_per_c = len(b_pieces) // len(c_pieces)
        assert a_per_c * len(c_pieces) == len(a_pieces) and b_per_c * len(c_pieces) == len(b_pieces)
        b_iter = iter(b_pieces)
        for k, c_piece in enumerate(c_pieces):
            mxu = [(stage_a, p) for p in a_pieces[k * a_per_c:(k + 1) * a_per_c]]
            mxu.insert(1, (stage_c, c_piece))
            quota = [b_per_c // len(mxu) + (1 if m < b_per_c % len(mxu) else 0) for m in range(len(mxu))]
            for (fn, piece), nb in zip(mxu, quota):
                fn(*piece)
                for _ in range(nb):
                    stage_b(*next(b_iter))

    step(a0_ref, a1_ref, act0_ref, act1_ref)
    a1_ref[...] = a0_ref[...]

    @pl.when(n >= 0)
    def _():
        act1_ref[...] = act0_ref[...]

    @pl.when((e_c == n_exp - 1) & (n >= PEER_PIPE_LAG))
    def _():
        g = _group_of_row(t_c * PEER_TOK_TILE, seq)
        mod = modv_ref[pl.ds(g, 1), :]
        y = x_ref[...] + mod[:, 5 * D_MODEL:6 * D_MODEL] * acc_ref[...].T
        if final:
            y = y * lax.rsqrt(jnp.mean(y * y, axis=-1, keepdims=True) + EPS) * fn_ref[...]
        o_ref[...] = y


def _peer_dense(xall, modv, hn, u_bf, vt_bf, n1, e1, r2, e2, final_norm, seq, rows_out, final):
    tt, et = PEER_TOK_TILE, PEER_EXP_TILE
    n_first = et // N_KEYS
    n_exp = N_EXPERTS // et
    n_tiles = (rows_out // tt) * n_exp
    tile = lambda lag: (lambda n: _peer_tile(n, lag, n_tiles, n_exp))
    ta, tb, tc = tile(0), tile(1), tile(PEER_PIPE_LAG)
    full = lambda *s: pl.BlockSpec(s, lambda n: (0,) * len(s))
    per_first = pl.BlockSpec((PEER_HEADS, tt // ROW_TILE, n_first, ROW_TILE), lambda n: (0, tb(n)[0], tb(n)[1], 0))
    per_second = pl.BlockSpec((PEER_HEADS, tt // ROW_TILE, N_KEYS, ROW_TILE), lambda n: (0, tb(n)[0], 0, 0))
    return pl.pallas_call(
        functools.partial(_peer_dense_kernel, seq=seq, final=final, n_tiles=n_tiles, n_exp=n_exp),
        grid=(n_tiles + PEER_PIPE_LAG,),
        in_specs=[pl.BlockSpec((tt, D_MODEL), lambda n: (tc(n)[0], 0)),
                  full(8, 6 * D_MODEL),
                  pl.BlockSpec((tt, D_MODEL), lambda n: (ta(n)[0], 0)),
                  pl.BlockSpec((et, D_MODEL), lambda n: (ta(n)[1], 0)),
                  pl.BlockSpec((D_MODEL, et), lambda n: (0, tc(n)[1])),
                  per_first, per_first, per_second, per_second, full(1, D_MODEL)],
        out_specs=pl.BlockSpec((tt, D_MODEL), lambda n: (tc(n)[0], 0)),
        out_shape=jax.ShapeDtypeStruct((rows_out, D_MODEL), F32),
        scratch_shapes=[pltpu.VMEM((D_MODEL, tt), F32), pltpu.VMEM((et, tt), F32), pltpu.VMEM((et, tt), F32),
                        pltpu.VMEM((et, tt), BF16), pltpu.VMEM((et, tt), BF16), pltpu.VMEM((tt, D_MODEL), BF16)],
        compiler_params=_cparams("arbitrary"),
        name="peer_dense",
    )(xall, modv, hn, u_bf, vt_bf, n1, e1, r2, e2, final_norm.reshape(1, D_MODEL))


def _reorder_in_weight(w):
    s = np.cumsum((0, 3 * DN_WIDTH, DN_WIDTH, POOL_WIDTH, 2 * DN_HEADS, 2 * DN_HEADS, D_MODEL, D_MODEL))
    qkv, z, pin, b, a, gd, gp = (w[:, s[i]:s[i + 1]] for i in range(7))
    pad = jnp.zeros((w.shape[0], BA_PAD - 4 * DN_HEADS), w.dtype)
    return jnp.concatenate([qkv, z, pin, gd, gp, b, a, pad], axis=1).astype(BF16)


def _forward(x, c, ctx, c_ctx, w_mod, b_mod, norm_mix, w_in, conv_w, a_log, dt_bias, dn_out_norm, pool_w, pool_scale,
             w_up_dn, w_up_pool, w_out, norm_ffn, peer_wq, peer_keys, peer_u, peer_v, final_norm):
    seq, nctx = x.shape[1], ctx.shape[1]
    nlat = BATCH * seq
    xall = jnp.concatenate([x.reshape(nlat, D_MODEL), ctx.reshape(BATCH * nctx, D_MODEL)], axis=0)
    rows = xall.shape[0]
    cvec = jnp.concatenate([c, c_ctx[None, :], jnp.zeros((8 - BATCH - 1, D_MODEL), F32)], axis=0)
    modv_all = _modulation(cvec, w_mod, b_mod)
    for i in range(DEPTH):
        last = i == DEPTH - 1
        rows_out = nlat if last else rows
        modv = modv_all[i]
        qkv, z, pin, gd, gp, ba = _inproj(xall, modv, norm_mix[i], _reorder_in_weight(w_in[i]), seq)
        q, k, v, bg = _dnconv(qkv, ba, conv_w[i], a_log[i], dt_bias[i], seq, nctx)
        o_f, o_b = _dnscan(*_dnprep(q, k, v, bg, seq, nctx), seq, nctx)
        xall = _merge(xall, o_f, o_b, z, pin, gd, gp, modv, dn_out_norm[i], pool_w[i], pool_scale[i],
                      w_up_dn[i].astype(BF16), w_up_pool[i].astype(BF16), w_out[i].astype(BF16), seq, nctx, rows_out)
        hn, n1, e1, r2, e2 = _peer_route(xall, modv, norm_ffn[i], peer_wq[i].astype(BF16), peer_keys[i], seq, rows_out)
        xall = _peer_dense(xall, modv, hn, peer_u[i].astype(BF16), peer_v[i].T.astype(BF16), n1, e1, r2, e2,
                           final_norm, seq, rows_out, last)
    return xall.reshape(BATCH, seq, D_MODEL)


def kernel(x, c, ctx, c_ctx, w_mod, b_mod, norm_mix, w_in, conv_w, a_log, dt_bias, dn_out_norm, pool_w, pool_scale, w_up_dn, w_up_pool, w_out, norm_ffn, peer_wq, peer_keys, peer_u, peer_v, final_norm):
    return _forward(x, c, ctx, c_ctx, w_mod, b_mod, norm_mix, w_in, conv_w, a_log, dt_bias, dn_out_norm, pool_w,
                    pool_scale, w_up_dn, w_up_pool, w_out, norm_ffn, peer_wq, peer_keys, peer_u, peer_v, final_norm)
```

```python
import functools

import numpy as np
import jax
import jax.numpy as jnp
from jax import lax
from jax.experimental import pallas as pl
from jax.experimental.pallas import tpu as pltpu

D_MODEL = 1024
BATCH = 2
DEPTH = 2
GRID_W = 64
EPS = 1e-6

DN_HEADS = 4
DN_HEAD_DIM = 128
DN_WIDTH = DN_HEADS * DN_HEAD_DIM
SHORT_CONV = 4
DN_CHUNK = 64
DN_CHAINS = 2 * DN_HEADS
DN_PREP_CHUNKS = 2
DN_SCAN_CHUNKS = 4
DN_LOCKSTEP = 16

POOL_WINDOWS = (2, 4, 8, 16)
POOL_GROUPS = 4
POOL_WIDTH = D_MODEL // 2
POOL_GW = POOL_WIDTH // POOL_GROUPS

PEER_HEADS = 8
N_KEYS = 128
N_EXPERTS = N_KEYS * N_KEYS
PEER_TOPK = 16
PEER_QDIM = 256
PEER_HALF = PEER_QDIM // 2

BA_PAD = 128
IN_COLS_R = 3 * DN_WIDTH + DN_WIDTH + POOL_WIDTH + 2 * D_MODEL + BA_PAD

LANES = 128
SUBLANES = 8
ROW_TILE = 256
INPROJ_TILE = 512
PEER_TOK_TILE = 512
PEER_EXP_TILE = 2048
PEER_SLAB = 256
HALO = 8
VMEM_LIMIT = 56 * 1024 * 1024

F32 = jnp.float32
BF16 = jnp.bfloat16
HIGHEST = lax.Precision.HIGHEST
NEG_INF = float("-inf")
NT_DIMS = (((1,), (1,)), ((), ()))


def _cparams(*sem):
    return pltpu.CompilerParams(dimension_semantics=sem, vmem_limit_bytes=VMEM_LIMIT)


def _group_of_row(row0, seq):
    return jnp.where(row0 < seq, 0, jnp.where(row0 < 2 * seq, 1, 2))


def _silu(x):
    return x * jax.nn.sigmoid(x)


def _split_bf16(a):
    hi = a.astype(BF16)
    lo = (a - hi.astype(F32)).astype(BF16)
    return hi, lo


def _bf16_pair_word(x):
    hi = lax.bitcast_convert_type(x.astype(BF16).astype(F32), jnp.uint32)
    return hi | (hi >> 16)


def _pair_word_rows(row):
    tile = pltpu.bitcast(jnp.broadcast_to(row, (SUBLANES, row.shape[1])), BF16)
    return jnp.concatenate([tile] * (N_KEYS // tile.shape[0]), axis=0)


def _dot_split(a, b):
    ah, al = _split_bf16(a)
    bh, bl = _split_bf16(b)
    a4 = jnp.concatenate([ah, al, ah, al], axis=1)
    b4 = jnp.concatenate([bh, bh, bl, bl], axis=0)
    return jnp.dot(a4, b4, preferred_element_type=F32)


def _mod_kernel(c_ref, w_ref, b_ref, o_ref):
    o_ref[0] = jnp.dot(_silu(c_ref[...]), w_ref[0], preferred_element_type=F32) + b_ref[0]


def _modulation(cvec, w_mod, b_mod):
    tn = 1536
    return pl.pallas_call(
        _mod_kernel,
        grid=(DEPTH, 6 * D_MODEL // tn),
        in_specs=[
            pl.BlockSpec((8, D_MODEL), lambda l, j: (0, 0)),
            pl.BlockSpec((1, D_MODEL, tn), lambda l, j: (l, 0, j)),
            pl.BlockSpec((1, 1, tn), lambda l, j: (l, 0, j)),
        ],
        out_specs=pl.BlockSpec((1, 8, tn), lambda l, j: (l, 0, j)),
        out_shape=jax.ShapeDtypeStruct((DEPTH, 8, 6 * D_MODEL), F32),
        compiler_params=_cparams("arbitrary", "arbitrary"),
        name="modulation",
    )(cvec, w_mod, b_mod.reshape(DEPTH, 1, 6 * D_MODEL))


def _norm_mod(x, nw, mod, k):
    ms = jnp.mean(x * x, axis=-1, keepdims=True)
    xn = x * lax.rsqrt(ms + EPS) * nw
    sh = mod[:, k * D_MODEL:(k + 1) * D_MODEL]
    sc = mod[:, (k + 1) * D_MODEL:(k + 2) * D_MODEL]
    return xn * (1 + sc) + sh


def _inproj_kernel(x_ref, modv_ref, nw_ref, w_ref, qkv_ref, z_ref, pin_ref, gd_ref, gp_ref, ba_ref, *, seq):
    g = _group_of_row(pl.program_id(0) * INPROJ_TILE, seq)
    mod = modv_ref[pl.ds(g, 1), :]
    h = _norm_mod(x_ref[...], nw_ref[...], mod, 0)
    y = jnp.dot(h.astype(BF16), w_ref[...], preferred_element_type=F32)
    o = 0
    for ref in (qkv_ref, z_ref, pin_ref, gd_ref, gp_ref, ba_ref):
        n = ref.shape[1]
        ref[...] = y[:, o:o + n].astype(ref.dtype)
        o += n


def _inproj(xall, modv, norm_w, w_in_r, seq):
    rows = xall.shape[0]
    widths = (3 * DN_WIDTH, DN_WIDTH, POOL_WIDTH, D_MODEL, D_MODEL, BA_PAD)
    dtypes = (F32, BF16, BF16, BF16, BF16, F32)
    return pl.pallas_call(
        functools.partial(_inproj_kernel, seq=seq),
        grid=(rows // INPROJ_TILE,),
        in_specs=[
            pl.BlockSpec((INPROJ_TILE, D_MODEL), lambda i: (i, 0)),
            pl.BlockSpec((8, 6 * D_MODEL), lambda i: (0, 0)),
            pl.BlockSpec((1, D_MODEL), lambda i: (0, 0)),
            pl.BlockSpec((D_MODEL, IN_COLS_R), lambda i: (0, 0)),
        ],
        out_specs=[pl.BlockSpec((INPROJ_TILE, n), lambda i: (i, 0)) for n in widths],
        out_shape=[jax.ShapeDtypeStruct((rows, n), dt) for n, dt in zip(widths, dtypes)],
        compiler_params=_cparams("arbitrary"),
        name="inproj",
    )(xall, modv, norm_w.reshape(1, D_MODEL), w_in_r)


def _dnconv_kernel(cur_ref, prev_ref, next_ref, cw_ref, ba_ref, alog_ref, dtb_ref,
                   q_ref, k_ref, v_ref, bg_ref, *, seq, ctx):
    row0 = pl.program_id(0) * ROW_TILE
    nlat = BATCH * seq
    is_start = (row0 == 0) | (row0 == seq) | (row0 == nlat) | (row0 == nlat + ctx)
    row1 = row0 + ROW_TILE
    is_end = (row1 == seq) | (row1 == nlat) | (row1 == nlat + ctx) | (row1 == nlat + BATCH * ctx)
    prev = jnp.where(is_start, 0.0, prev_ref[...])
    nxt = jnp.where(is_end, 0.0, next_ref[...])
    ext = jnp.concatenate([prev, cur_ref[...], nxt], axis=0)
    left = SHORT_CONV // 2
    cw = cw_ref[...]
    y = None
    for j in range(SHORT_CONV):
        o = HALO - left + j
        term = ext[o:o + ROW_TILE, :] * cw[j:j + 1, :]
        y = term if y is None else y + term
    y = _silu(y)
    for hh in range(DN_HEADS):
        lo, hi = hh * DN_HEAD_DIM, (hh + 1) * DN_HEAD_DIM
        qh = y[:, lo:hi]
        kh = y[:, DN_WIDTH + lo:DN_WIDTH + hi]
        q_ref[:, lo:hi] = qh * lax.rsqrt(jnp.sum(qh * qh, axis=-1, keepdims=True) + EPS) * (DN_HEAD_DIM ** -0.5)
        k_ref[:, lo:hi] = kh * lax.rsqrt(jnp.sum(kh * kh, axis=-1, keepdims=True) + EPS)
    v_ref[...] = y[:, 2 * DN_WIDTH:]
    ba = ba_ref[...]
    beta = jax.nn.sigmoid(ba)
    xs = ba + dtb_ref[...]
    softplus = jnp.maximum(xs, 0.0) + jnp.log(1.0 + jnp.exp(-jnp.abs(xs)))
    gdec = -jnp.exp(alog_ref[...]) * softplus
    col = lax.broadcasted_iota(jnp.int32, ba.shape, 1)
    bg_ref[...] = jnp.where(col < 2 * DN_HEADS, beta, jnp.where(col < 4 * DN_HEADS, gdec, 0.0))


def _dnconv(qkv, ba, conv_w, a_log, dt_bias, seq, ctx):
    rows = qkv.shape[0]
    nh = ROW_TILE // HALO
    last = rows // HALO - 1
    pad = jnp.zeros((2 * DN_HEADS,), F32)
    tail = jnp.zeros((BA_PAD - 4 * DN_HEADS,), F32)
    alog = jnp.concatenate([pad, a_log.reshape(-1), tail]).reshape(1, BA_PAD)
    dtb = jnp.concatenate([pad, dt_bias.reshape(-1), tail]).reshape(1, BA_PAD)
    w3 = 3 * DN_WIDTH
    return pl.pallas_call(
        functools.partial(_dnconv_kernel, seq=seq, ctx=ctx),
        grid=(rows // ROW_TILE,),
        in_specs=[
            pl.BlockSpec((ROW_TILE, w3), lambda i: (i, 0)),
            pl.BlockSpec((HALO, w3), lambda i: (jnp.maximum(i * nh - 1, 0), 0)),
            pl.BlockSpec((HALO, w3), lambda i: (jnp.minimum((i + 1) * nh, last), 0)),
            pl.BlockSpec((SHORT_CONV, w3), lambda i: (0, 0)),
            pl.BlockSpec((ROW_TILE, BA_PAD), lambda i: (i, 0)),
            pl.BlockSpec((1, BA_PAD), lambda i: (0, 0)),
            pl.BlockSpec((1, BA_PAD), lambda i: (0, 0)),
        ],
        out_specs=[pl.BlockSpec((ROW_TILE, DN_WIDTH), lambda i: (i, 0))] * 3
        + [pl.BlockSpec((ROW_TILE, BA_PAD), lambda i: (i, 0))],
        out_shape=[jax.ShapeDtypeStruct((rows, DN_WIDTH), F32)] * 3
        + [jax.ShapeDtypeStruct((rows, BA_PAD), F32)],
        compiler_params=_cparams("arbitrary"),
        name="dnconv",
    )(qkv, qkv, qkv, conv_w, ba, alog, dtb)


def _dnprep_kernel(q_ref, k_ref, v_ref, bg_ref, bgt_ref, wq_ref, u_ref, lk_ref, eg_ref):
    c = DN_CHUNK
    ri = lax.broadcasted_iota(jnp.int32, (c, c), 0)
    ci = lax.broadcasted_iota(jnp.int32, (c, c), 1)
    dirs = ((ri >= ci, ri > ci, ci >= ri, c - 1), (ri <= ci, ri < ci, ci <= ri, 0))
    chains = []
    for j in range(DN_PREP_CHUNKS):
        rows = slice(j * c, (j + 1) * c)
        bg = bg_ref[rows, :]
        bgh, bgl = _split_bf16(bg)
        bgth, bgtl = _split_bf16(bgt_ref[j])
        bg2 = jnp.concatenate([bgh, bgl], axis=0)
        bgt2 = jnp.concatenate([bgth, bgtl], axis=1)
        egs = []
        for d, (incl, strict, incl_t, last) in enumerate(dirs):
            m = incl.astype(BF16)
            mt = incl_t.astype(BF16)
            gc_all = jnp.dot(jnp.concatenate([m, m], axis=1), bg2, preferred_element_type=F32)
            gr_all = jnp.dot(bgt2, jnp.concatenate([mt, mt], axis=0), preferred_element_type=F32)
            for hh in range(DN_HEADS):
                lo, hi = hh * DN_HEAD_DIM, (hh + 1) * DN_HEAD_DIM
                ch = d * DN_HEADS + hh
                gcol = 2 * DN_HEADS + ch
                q = q_ref[rows, lo:hi]
                k = k_ref[rows, lo:hi]
                v = v_ref[rows, lo:hi]
                beta = bg[:, ch:ch + 1]
                gc = gc_all[:, gcol:gcol + 1]
                gr = gr_all[gcol:gcol + 1, :]
                glast = gc_all[last:last + 1, gcol:gcol + 1]
                decay = jnp.where(incl, jnp.exp(jnp.minimum(gc - gr, 0.0)), 0.0)
                kk = lax.dot_general(k, k, NT_DIMS, preferred_element_type=F32)
                qk = lax.dot_general(q, k, NT_DIMS, preferred_element_type=F32)
                egc = jnp.exp(gc)
                cols = slice(ch * DN_HEAD_DIM, (ch + 1) * DN_HEAD_DIM)
                wq_ref[0, j, c:2 * c, cols] = (q * egc).astype(BF16)
                lk_ref[0, j, ch, 0:c, :] = jnp.where(incl, qk * decay, 0.0).astype(BF16)
                lk_ref[0, j, ch, c:, :] = (k * jnp.exp(glast - gc)).T.astype(BF16)
                egs.append(jnp.broadcast_to(jnp.exp(glast), (1, LANES)))
                chains.append((j, cols, jnp.where(strict, -(beta * kk * decay), 0.0),
                               jnp.concatenate([k * (beta * egc), v * beta], axis=1)))
        eg_ref[0, j] = jnp.concatenate(egs, axis=0)
    for g0 in range(0, len(chains), DN_LOCKSTEP):
        group = chains[g0:g0 + DN_LOCKSTEP]
        ps = [ch[2] for ch in group]
        eye = (ri == ci).astype(F32)
        xs = [eye + p for p in ps]
        for _ in range(int(np.log2(c)) - 1):
            ps = [_dot_split(p, p) for p in ps]
            xs = [x + _dot_split(x, p) for x, p in zip(xs, ps)]
        ys = [_dot_split(x, ch[3]) for x, ch in zip(xs, group)]
        for (j, cols, _, _), y in zip(group, ys):
            wq_ref[0, j, 0:c, cols] = y[:, :DN_HEAD_DIM].astype(BF16)
            u_ref[0, j, :, cols] = y[:, DN_HEAD_DIM:]


def _dnprep(q, k, v, bg, seq, ctx):
    rows = q.shape[0]
    c = DN_CHUNK
    n = DN_PREP_CHUNKS
    nct, nlt = ctx // c, seq // c
    assert nct % n == 0 and nlt % n == 0
    nch = nct + nlt
    bgt = bg[:, :4 * DN_HEADS].reshape(rows // c, c, 4 * DN_HEADS).transpose(0, 2, 1)

    def seq_pos(i):
        i = i * n
        lat = i < BATCH * nlt
        j = i - BATCH * nlt
        return jnp.where(lat, i // nlt, j // nct), jnp.where(lat, nct + i % nlt, j % nct) // n

    def out_spec(*tail):
        return pl.BlockSpec((1, n) + tail, lambda i: seq_pos(i) + (0,) * len(tail))

    wide = pl.BlockSpec((n * c, DN_WIDTH), lambda i: (i, 0))
    width = DN_CHAINS * DN_HEAD_DIM
    return pl.pallas_call(
        _dnprep_kernel,
        grid=(rows // (n * c),),
        in_specs=[wide, wide, wide,
                  pl.BlockSpec((n * c, BA_PAD), lambda i: (i, 0)),
                  pl.BlockSpec((n, 4 * DN_HEADS, c), lambda i: (i, 0, 0))],
        out_specs=[out_spec(2 * c, width), out_spec(c, width), out_spec(DN_CHAINS, c + DN_HEAD_DIM, c),
                   out_spec(DN_CHAINS, LANES)],
        out_shape=[jax.ShapeDtypeStruct((BATCH, nch, 2 * c, width), BF16),
                   jax.ShapeDtypeStruct((BATCH, nch, c, width), F32),
                   jax.ShapeDtypeStruct((BATCH, nch, DN_CHAINS, c + DN_HEAD_DIM, c), BF16),
                   jax.ShapeDtypeStruct((BATCH, nch, DN_CHAINS, LANES), F32)],
        compiler_params=_cparams("arbitrary"),
        name="dnprep",
    )(q, k, v, bg, bgt)


def _dnscan_kernel(wqf_ref, uf_ref, lkf_ref, egf_ref, wqb_ref, ub_ref, lkb_ref, egb_ref, of_ref, ob_ref, s_ref):
    @pl.when(pl.program_id(0) == 0)
    def _():
        s_ref[...] = jnp.zeros_like(s_ref)

    c = DN_CHUNK
    dirs = ((wqf_ref, uf_ref, lkf_ref, egf_ref, of_ref), (wqb_ref, ub_ref, lkb_ref, egb_ref, ob_ref))
    chains = [(d, b, hh) for d in range(2) for b in range(BATCH) for hh in range(DN_HEADS)]
    sidx = lambda d, b, hh: (b * 2 + d) * DN_HEADS + hh
    cols = lambda hh: slice(hh * DN_HEAD_DIM, (hh + 1) * DN_HEAD_DIM)
    for j in range(DN_SCAN_CHUNKS):
        pos = (j, DN_SCAN_CHUNKS - 1 - j)
        ss = [s_ref[sidx(*ch)] for ch in chains]
        r1 = [jnp.dot(dirs[d][0][b, pos[d], :, cols(hh)], s.astype(BF16), preferred_element_type=F32)
              for (d, b, hh), s in zip(chains, ss)]
        vn = [dirs[d][1][b, pos[d], :, cols(hh)] - r[:c] for (d, b, hh), r in zip(chains, r1)]
        r2 = [jnp.dot(dirs[d][2][b, pos[d], hh], v.astype(BF16), preferred_element_type=F32)
              for (d, b, hh), v in zip(chains, vn)]
        for (d, b, hh), s, a1, a2 in zip(chains, ss, r1, r2):
            dirs[d][4][b, pos[d], :, cols(hh)] = a1[c:] + a2[:c]
            row = d * DN_HEADS + hh
            s_ref[sidx(d, b, hh)] = s * dirs[d][3][b, pos[d], row:row + 1, :] + a2[c:]


def _dnscan(wq, u, lk, eg, seq, ctx):
    c = DN_CHUNK
    n = DN_SCAN_CHUNKS
    nct, nlt = ctx // c, seq // c
    assert nct % n == 0 and nlt % n == 0
    nch = nct + nlt
    nblk, nctb = nch // n, nct // n

    def bwd(s):
        return jnp.where(s < nctb, nctb - 1 - s, nctb + nblk - 1 - s)

    def specs(pos, d):
        return [pl.BlockSpec((BATCH, n, 2 * c, DN_WIDTH), lambda s: (0, pos(s), 0, d)),
                pl.BlockSpec((BATCH, n, c, DN_WIDTH), lambda s: (0, pos(s), 0, d)),
                pl.BlockSpec((BATCH, n, DN_HEADS, c + DN_HEAD_DIM, c), lambda s: (0, pos(s), d, 0, 0)),
                pl.BlockSpec((BATCH, n, DN_CHAINS, LANES), lambda s: (0, pos(s), 0, 0))]

    fwd = lambda s: s
    return pl.pallas_call(
        _dnscan_kernel,
        grid=(nblk,),
        in_specs=specs(fwd, 0) + specs(bwd, 1),
        out_specs=[pl.BlockSpec((BATCH, n, c, DN_WIDTH), lambda s: (0, s, 0, 0)),
                   pl.BlockSpec((BATCH, n, c, DN_WIDTH), lambda s: (0, bwd(s), 0, 0))],
        out_shape=[jax.ShapeDtypeStruct((BATCH, nch, c, DN_WIDTH), F32)] * 2,
        scratch_shapes=[pltpu.VMEM((BATCH * DN_CHAINS, DN_HEAD_DIM, DN_HEAD_DIM), F32)],
        compiler_params=_cparams("arbitrary"),
        name="dnscan",
    )(wq, u, lk, eg, wq, u, lk, eg)


def _pool_matrices(tile, seg):
    t = np.arange(tile)
    p = t % seg
    mats = []
    for win in POOL_WINDOWS:
        lo = np.clip(p - win // 2, 0, seg)
        hi = np.clip(p + win - win // 2, 0, seg)
        same = (t[:, None] // seg) == (t[None, :] // seg)
        inside = same & (p[None, :] >= lo[:, None]) & (p[None, :] < hi[:, None])
        mats.append(inside / (hi - lo)[:, None].astype(np.float64) - np.eye(tile))
    return np.stack(mats).astype(np.float32)


def _merge_kernel(x_ref, of_ref, ob_ref, z_ref, pin_ref, gd_ref, gp_ref, modv_ref, on_ref, pm_ref, pw_ref, ps_ref,
                  wud_ref, wup_ref, wo_ref, o_ref, *, seq):
    g = _group_of_row(pl.program_id(0) * ROW_TILE, seq)
    mod = modv_ref[pl.ds(g, 1), :]
    gate = mod[:, 2 * D_MODEL:3 * D_MODEL]
    o = (of_ref[0] + ob_ref[0]).reshape(ROW_TILE, DN_WIDTH)
    z = z_ref[...].astype(F32)
    onw = on_ref[...]
    ys = []
    for hh in range(DN_HEADS):
        lo, hi = hh * DN_HEAD_DIM, (hh + 1) * DN_HEAD_DIM
        oh = o[:, lo:hi]
        ys.append(oh * lax.rsqrt(jnp.mean(oh * oh, axis=-1, keepdims=True) + EPS) * onw * _silu(z[:, lo:hi]))
    y_dn = jnp.concatenate(ys, axis=1)
    pin = pin_ref[...].astype(F32)
    yp = []
    for gi in range(POOL_GROUPS):
        lo, hi = gi * POOL_GW, (gi + 1) * POOL_GW
        pooled = jnp.dot(pm_ref[0, gi], pin[:, lo:hi], precision=HIGHEST, preferred_element_type=F32)
        yp.append(jnp.dot(pooled, pw_ref[gi], preferred_element_type=F32))
    y_pool = jnp.concatenate(yp, axis=1) * ps_ref[...]
    m = (jax.nn.sigmoid(gd_ref[...].astype(F32)) * jnp.dot(y_dn.astype(BF16), wud_ref[...], preferred_element_type=F32)
         + jax.nn.sigmoid(gp_ref[...].astype(F32)) * jnp.dot(y_pool.astype(BF16), wup_ref[...], preferred_element_type=F32))
    out = jnp.dot(m.astype(BF16), wo_ref[...], preferred_element_type=F32)
    o_ref[...] = x_ref[...] + gate * out


def _merge(xall, o_f, o_b, z, pin, gd, gp, modv, onorm, pool_w, pool_scale, wud, wup, wo, seq, ctx, rows_out):
    assert ctx == ROW_TILE and seq % ROW_TILE == 0
    lat_tiles = BATCH * seq // ROW_TILE
    tiles_per_seq = seq // ROW_TILE
    cpt = ROW_TILE // DN_CHUNK
    pm = jnp.asarray(np.stack([_pool_matrices(ROW_TILE, GRID_W), _pool_matrices(ROW_TILE, ctx)]))
    row = lambda n: pl.BlockSpec((ROW_TILE, n), lambda i: (i, 0))
    full = lambda *s: pl.BlockSpec(s, lambda i: (0,) * len(s))

    def o_pos(i):
        lat = i < lat_tiles
        return (jnp.where(lat, i // tiles_per_seq, i - lat_tiles), jnp.where(lat, 1 + i % tiles_per_seq, 0), 0, 0)

    o_spec = pl.BlockSpec((1, cpt, DN_CHUNK, DN_WIDTH), o_pos)
    return pl.pallas_call(
        functools.partial(_merge_kernel, seq=seq),
        grid=(rows_out // ROW_TILE,),
        in_specs=[
            row(D_MODEL), o_spec, o_spec, row(DN_WIDTH), row(POOL_WIDTH), row(D_MODEL), row(D_MODEL),
            full(8, 6 * D_MODEL), full(1, DN_HEAD_DIM),
            pl.BlockSpec((1, POOL_GROUPS, ROW_TILE, ROW_TILE), lambda i: (jnp.where(i >= lat_tiles, 1, 0), 0, 0, 0)),
            full(POOL_GROUPS, POOL_GW, POOL_GW), full(1, POOL_WIDTH),
            full(DN_WIDTH, D_MODEL), full(POOL_WIDTH, D_MODEL), full(D_MODEL, D_MODEL),
        ],
        out_specs=row(D_MODEL),
        out_shape=jax.ShapeDtypeStruct((rows_out, D_MODEL), F32),
        compiler_params=_cparams("arbitrary"),
        name="merge",
    )(xall, o_f, o_b, z, pin, gd, gp, modv, onorm.reshape(1, DN_HEAD_DIM), pm, pool_w,
      pool_scale.reshape(1, POOL_WIDTH), wud, wup, wo)


def _col_max(x):
    return jnp.max(x, axis=0, keepdims=True)


def _col_min(x):
    return jnp.min(x, axis=0, keepdims=True)


def _oddeven_merge(lo, hi, r):
    step = r * 2
    if step < hi - lo:
        yield from _oddeven_merge(lo, hi, step)
        yield from _oddeven_merge(lo + r, hi, step)
        yield from [(i, i + r) for i in range(lo + r, hi - r, step)]
    else:
        yield (lo, lo + r)


def _oddeven_merge_sort(lo, hi):
    if hi - lo >= 1:
        mid = lo + (hi - lo) // 2
        yield from _oddeven_merge_sort(lo, mid)
        yield from _oddeven_merge_sort(mid + 1, hi)
        yield from _oddeven_merge(lo, hi, 1)


def _exchange(v, i, j):
    v[i], v[j] = jnp.maximum(v[i], v[j]), jnp.minimum(v[i], v[j])


def _sorted_top_many(block_lists):
    k = PEER_TOPK
    vs = [list(blocks) for blocks in block_lists]
    for i, j in _oddeven_merge_sort(0, k - 1):
        for v in vs:
            _exchange(v, i, j)
    shift = SUBLANES // 2
    while shift >= 1:
        ws = [[pltpu.roll(x, shift, 0) for x in v] for v in vs]
        vs = [[jnp.maximum(v[j], w[k - 1 - j]) for j in range(k)] for v, w in zip(vs, ws)]
        stride = k // 2
        while stride >= 1:
            for i in range(k):
                if i & stride == 0:
                    for v in vs:
                        _exchange(v, i, i + stride)
            stride //= 2
        shift //= 2
    return vs


def _sublane_total(x):
    shift = SUBLANES // 2
    while shift >= 1:
        x = x + pltpu.roll(x, shift, 0)
        shift //= 2
    return x


def _unambiguous(blocks, top):
    ok = top[0] > top[1]
    for a in range(1, PEER_TOPK - 1):
        ok = ok & (top[a] > top[a + 1])
    ge = None
    for blk in blocks:
        one = jnp.where(blk >= top[PEER_TOPK - 1], 1.0, 0.0)
        ge = one if ge is None else ge + one
    return ok & (_sublane_total(ge) == float(PEER_TOPK))


def _top_select(scores, iota):
    n = float(scores[0].shape[0])
    cur = list(scores)
    idxs = [[] for _ in cur]
    for _ in range(PEER_TOPK):
        for j in range(len(cur)):
            m = _col_max(cur[j])
            cand = jnp.where(cur[j] == m, iota, n)
            idx = _col_min(cand)
            cur[j] = jnp.where(cand == idx, NEG_INF, cur[j])
            idxs[j].append(idx)
    return [jnp.concatenate(i, axis=0) for i in idxs]


def _peer_route_kernel(x_ref, modv_ref, nw_ref, wq_ref, keys_ref, hn_ref, n1_ref, e1_ref, r2_ref, e2_ref,
                       q_scr, *, seq):
    g = _group_of_row(pl.program_id(0) * ROW_TILE, seq)
    mod = modv_ref[pl.ds(g, 1), :]
    hn = _norm_mod(x_ref[...], nw_ref[...], mod, 3).astype(BF16)
    hn_ref[...] = hn
    q = jnp.dot(hn, wq_ref[...], preferred_element_type=F32)
    nsub = ROW_TILE // LANES
    for j in range(2 * PEER_HEADS):
        for t in range(nsub):
            q_scr[j, t] = q[t * LANES:(t + 1) * LANES, j * PEER_HALF:(j + 1) * PEER_HALF]
    k = PEER_TOPK
    nblk = N_KEYS // SUBLANES
    iota_k = lax.broadcasted_iota(jnp.int32, (N_KEYS, LANES), 0).astype(F32)
    iota_a = lax.broadcasted_iota(jnp.int32, (k, LANES), 0).astype(F32)
    iota_s = lax.broadcasted_iota(jnp.int32, (SUBLANES, LANES), 0).astype(F32)
    split = lambda s: [s[SUBLANES * j:SUBLANES * (j + 1), :] for j in range(nblk)]
    rep = lambda row: jnp.broadcast_to(row, (SUBLANES, LANES))

    def head(hh, carry):
        subs = range(nsub)
        s1 = [lax.dot_general(keys_ref[2 * hh], q_scr[2 * hh, t], NT_DIMS, preferred_element_type=F32) for t in subs]
        s2 = [lax.dot_general(keys_ref[2 * hh + 1], q_scr[2 * hh + 1, t], NT_DIMS, preferred_element_type=F32)
              for t in subs]
        b1, b2 = [split(x) for x in s1], [split(x) for x in s2]
        tops = _sorted_top_many(b1 + b2)
        top1, top2 = tops[:nsub], tops[nsub:]
        c = [jnp.concatenate([x[0:1, :] for x in top], axis=0) for top in top1]
        d = [jnp.concatenate([x[0:1, :] for x in top], axis=0) for top in top2]

        def grid_cells(c, d):
            cells = [c[0:1, :] + d[0:SUBLANES, :], c[0:1, :] + d[SUBLANES:k, :]]
            for a in range(1, SUBLANES):
                cells.append(jnp.where(iota_s < float(k // (a + 1)), c[a:a + 1, :] + d[0:SUBLANES, :], NEG_INF))
            cells.append(c[SUBLANES:k, :] + d[0:1, :])
            return cells + [jnp.full((SUBLANES, LANES), NEG_INF, F32)] * (k - len(cells))

        cuts = [x[k - 1][0:1, :] for x in _sorted_top_many([grid_cells(c[t], d[t]) for t in subs])]
        cnt_cut = [jnp.zeros((k, LANES), F32) for _ in subs]
        for b in range(k):
            cnt_cut = [acc + jnp.where(c[t] + d[t][b:b + 1, :] >= cuts[t], 1.0, 0.0) for t, acc in enumerate(cnt_cut)]

        def by_merge():
            cnts = [jnp.zeros((k, LANES), F32) for _ in subs]
            for _ in range(k):
                for t in subs:
                    dn = jnp.full((k, LANES), NEG_INF, F32)
                    for b in range(k):
                        dn = jnp.where(cnts[t] == float(b), d[t][b:b + 1, :], dn)
                    f = c[t] + dn
                    cand = jnp.where(f == _col_max(f), iota_a, float(k))
                    cnts[t] = cnts[t] + (cand == _col_min(cand)).astype(F32)
            return tuple(cnts)

        exact_cut = jnp.all(jnp.concatenate([jnp.sum(x, axis=0, keepdims=True) for x in cnt_cut], axis=0) == float(k))
        cnt = lax.cond(exact_cut, lambda: tuple(cnt_cut), by_merge)
        zsum = []
        for t in subs:
            e1c = jnp.exp(c[t] - c[t][0:1, :])
            e2d = jnp.exp(d[t] - d[t][0:1, :])
            part = jnp.zeros((k, LANES), F32)
            for b in range(k):
                part = part + jnp.where(cnt[t] > float(b), e2d[b:b + 1, :], 0.0)
            zsum.append(jnp.sum(e1c * part, axis=0, keepdims=True))

        def by_value():
            n1b = [[jnp.zeros((SUBLANES, LANES), F32)] * nblk for _ in subs]
            r2b = [[jnp.full((SUBLANES, LANES), float(k), F32)] * nblk for _ in subs]
            for a in range(k):
                for t in subs:
                    cnt_a = rep(cnt[t][a:a + 1, :])
                    n1b[t] = [jnp.where(blk == top1[t][a], cnt_a, acc) for blk, acc in zip(b1[t], n1b[t])]
                    r2b[t] = [jnp.where(blk == top2[t][a], float(a), acc) for blk, acc in zip(b2[t], r2b[t])]
            return tuple(jnp.concatenate(x, axis=0) for x in n1b + r2b)

        def by_index():
            idx = _top_select(s1 + s2, iota_k)
            out1, out2 = [], []
            for t in subs:
                n1 = jnp.zeros((N_KEYS, LANES), F32)
                r2 = jnp.full((N_KEYS, LANES), float(k), F32)
                for a in range(k):
                    n1 = jnp.where(iota_k == idx[t][a:a + 1, :], cnt[t][a:a + 1, :], n1)
                    r2 = jnp.where(iota_k == idx[nsub + t][a:a + 1, :], float(a), r2)
                out1.append(n1)
                out2.append(r2)
            return tuple(out1 + out2)

        clean = None
        for t in subs:
            ok = _unambiguous(b1[t], top1[t]) & _unambiguous(b2[t], top2[t])
            clean = ok if clean is None else clean & ok
        ranks = lax.cond(jnp.all(clean), by_value, by_index)
        for t in subs:
            lanes = slice(t * LANES, (t + 1) * LANES)
            n1_ref[hh, 0, :, lanes] = _bf16_pair_word(ranks[t])
            e1_ref[hh, 0, :, lanes] = _bf16_pair_word(0.5 * jnp.exp(s1[t] - c[t][0:1, :]) / zsum[t])
            r2_ref[hh, 0, :, lanes] = ranks[nsub + t].astype(BF16)
            e2_ref[hh, 0, :, lanes] = jnp.exp(s2[t] - d[t][0:1, :]).astype(BF16)
        return carry

    lax.fori_loop(0, PEER_HEADS, head, 0)


def _peer_route(xall, modv, norm_w, wq, keys, seq, rows_out):
    row = lambda n: pl.BlockSpec((ROW_TILE, n), lambda i: (i, 0))
    full = lambda *s: pl.BlockSpec(s, lambda i: (0,) * len(s))
    tab = pl.BlockSpec((PEER_HEADS, 1, N_KEYS, ROW_TILE), lambda i: (0, i, 0, 0))
    tab_shape = lambda dt: jax.ShapeDtypeStruct((PEER_HEADS, rows_out // ROW_TILE, N_KEYS, ROW_TILE), dt)
    return pl.pallas_call(
        functools.partial(_peer_route_kernel, seq=seq),
        grid=(rows_out // ROW_TILE,),
        in_specs=[row(D_MODEL), full(8, 6 * D_MODEL), full(1, D_MODEL), full(D_MODEL, PEER_HEADS * PEER_QDIM),
                  full(2 * PEER_HEADS, N_KEYS, PEER_HALF)],
        out_specs=[row(D_MODEL), tab, tab, tab, tab],
        out_shape=[jax.ShapeDtypeStruct((rows_out, D_MODEL), BF16), tab_shape(jnp.uint32), tab_shape(jnp.uint32),
                   tab_shape(BF16), tab_shape(BF16)],
        scratch_shapes=[pltpu.VMEM((2 * PEER_HEADS, ROW_TILE // LANES, LANES, PEER_HALF), F32)],
        compiler_params=_cparams("arbitrary"),
        name="peer_route",
    )(xall, modv, norm_w.reshape(1, D_MODEL), wq, keys.reshape(2 * PEER_HEADS, N_KEYS, PEER_HALF))


PEER_PIPE_LAG = 2
PEER_MXU_BLOCK = 256


def _peer_tile(n, lag, n_tiles, n_exp):
    m = jnp.clip(n - lag, 0, n_tiles - 1)
    return m // n_exp, m % n_exp


def _peer_dense_kernel(x_ref, modv_ref, hn_ref, u_ref, vt_ref, n1_ref, e1_ref, r2_ref, e2_ref, fn_ref, o_ref,
                       acc_ref, a0_ref, a1_ref, act0_ref, act1_ref, hn_scr, *, seq, final, n_tiles, n_exp):
    n = pl.program_id(0)
    t_c, e_c = _peer_tile(n, PEER_PIPE_LAG, n_tiles, n_exp)

    @pl.when(n == 0)
    def _():
        for ref in (acc_ref, a0_ref, a1_ref, act0_ref, act1_ref):
            ref[...] = jnp.zeros_like(ref)

    @pl.when(_peer_tile(n, 0, n_tiles, n_exp)[1] == 0)
    def _():
        hn_scr[...] = hn_ref[...]

    def step(a_next, a_cur, act_next, act_cur):
        zero = jnp.zeros((), BF16)
        blk = PEER_MXU_BLOCK

        def stage_a(rb, cb):
            rows, cols = slice(rb * blk, (rb + 1) * blk), slice(cb * blk, (cb + 1) * blk)
            a_next[rows, cols] = lax.dot_general(u_ref[rows, :], hn_scr[cols, :], NT_DIMS,
                                                 preferred_element_type=F32)

        def stage_b(t, i):
            lanes = slice(t * ROW_TILE, (t + 1) * ROW_TILE)
            wsum = None
            for hh in range(PEER_HEADS):
                keep = r2_ref[hh, t] < _pair_word_rows(n1_ref[hh, t, i:i + 1, :])
                term = jnp.where(keep, e2_ref[hh, t], zero) * _pair_word_rows(e1_ref[hh, t, i:i + 1, :])
                wsum = term if wsum is None else wsum + term
            rows = slice(i * N_KEYS, (i + 1) * N_KEYS)
            a = a_cur[rows, lanes]
            gelu = a * (1.0 + lax.erf(a * (2.0 ** -0.5)))
            act_next[rows, lanes] = gelu.astype(BF16) * wsum

        def stage_c(rb, cb):
            rows, cols = slice(rb * blk, (rb + 1) * blk), slice(cb * blk, (cb + 1) * blk)
            prev = jnp.where(e_c == 0, 0.0, acc_ref[rows, cols])
            acc_ref[rows, cols] = prev + jnp.dot(vt_ref[rows, :], act_cur[:, cols], preferred_element_type=F32)

        a_pieces = [(rb, cb) for cb in range(PEER_TOK_TILE // blk) for rb in range(PEER_EXP_TILE // blk)]
        c_pieces = [(rb, cb) for cb in range(PEER_TOK_TILE // blk) for rb in range(D_MODEL // blk)]
        b_pieces = [(t, i) for t in range(PEER_TOK_TILE // ROW_TILE) for i in range(PEER_EXP_TILE // N_KEYS)]
        a_per_c = len(a_pieces) // len(c_pieces)
        b_per_c = len(b_pieces) // len(c_pieces)
        assert a_per_c * len(c_pieces) == len(a_pieces) and b_per_c * len(c_pieces) == len(b_pieces)
        b_iter = iter(b_pieces)
        for k, c_piece in enumerate(c_pieces):
            mxu = [(stage_a, p) for p in a_pieces[k * a_per_c:(k + 1) * a_per_c]]
            mxu.insert(1, (stage_c, c_piece))
            quota = [b_per_c // len(mxu) + (1 if m < b_per_c % len(mxu) else 0) for m in range(len(mxu))]
            for (fn, piece), nb in zip(mxu, quota):
                fn(*piece)
                for _ in range(nb):
                    stage_b(*next(b_iter))

    step(a0_ref, a1_ref, act0_ref, act1_ref)
    a1_ref[...] = a0_ref[...]

    @pl.when(n >= 0)
    def _():
        act1_ref[...] = act0_ref[...]

    @pl.when((e_c == n_exp - 1) & (n >= PEER_PIPE_LAG))
    def _():
        g = _group_of_row(t_c * PEER_TOK_TILE, seq)
        mod = modv_ref[pl.ds(g, 1), :]
        y = x_ref[...] + mod[:, 5 * D_MODEL:6 * D_MODEL] * acc_ref[...].T
        if final:
            y = y * lax.rsqrt(jnp.mean(y * y, axis=-1, keepdims=True) + EPS) * fn_ref[...]
        o_ref[...] = y


def _peer_dense(xall, modv, hn, u_bf, vt_bf, n1, e1, r2, e2, final_norm, seq, rows_out, final):
    tt, et = PEER_TOK_TILE, PEER_EXP_TILE
    n_first = et // N_KEYS
    n_exp = N_EXPERTS // et
    n_tiles = (rows_out // tt) * n_exp
    tile = lambda lag: (lambda n: _peer_tile(n, lag, n_tiles, n_exp))
    ta, tb, tc = tile(0), tile(1), tile(PEER_PIPE_LAG)
    full = lambda *s: pl.BlockSpec(s, lambda n: (0,) * len(s))
    per_first = pl.BlockSpec((PEER_HEADS, tt // ROW_TILE, n_first, ROW_TILE), lambda n: (0, tb(n)[0], tb(n)[1], 0))
    per_second = pl.BlockSpec((PEER_HEADS, tt // ROW_TILE, N_KEYS, ROW_TILE), lambda n: (0, tb(n)[0], 0, 0))
    return pl.pallas_call(
        functools.partial(_peer_dense_kernel, seq=seq, final=final, n_tiles=n_tiles, n_exp=n_exp),
        grid=(n_tiles + PEER_PIPE_LAG,),
        in_specs=[pl.BlockSpec((tt, D_MODEL), lambda n: (tc(n)[0], 0)),
                  full(8, 6 * D_MODEL),
                  pl.BlockSpec((tt, D_MODEL), lambda n: (ta(n)[0], 0)),
                  pl.BlockSpec((et, D_MODEL), lambda n: (ta(n)[1], 0)),
                  pl.BlockSpec((D_MODEL, et), lambda n: (0, tc(n)[1])),
                  per_first, per_first, per_second, per_second, full(1, D_MODEL)],
        out_specs=pl.BlockSpec((tt, D_MODEL), lambda n: (tc(n)[0], 0)),
        out_shape=jax.ShapeDtypeStruct((rows_out, D_MODEL), F32),
        scratch_shapes=[pltpu.VMEM((D_MODEL, tt), F32), pltpu.VMEM((et, tt), F32), pltpu.VMEM((et, tt), F32),
                        pltpu.VMEM((et, tt), BF16), pltpu.VMEM((et, tt), BF16), pltpu.VMEM((tt, D_MODEL), BF16)],
        compiler_params=_cparams("arbitrary"),
        name="peer_dense",
    )(xall, modv, hn, u_bf, vt_bf, n1, e1, r2, e2, final_norm.reshape(1, D_MODEL))


def _reorder_in_weight(w):
    s = np.cumsum((0, 3 * DN_WIDTH, DN_WIDTH, POOL_WIDTH, 2 * DN_HEADS, 2 * DN_HEADS, D_MODEL, D_MODEL))
    qkv, z, pin, b, a, gd, gp = (w[:, s[i]:s[i + 1]] for i in range(7))
    pad = jnp.zeros((w.shape[0], BA_PAD - 4 * DN_HEADS), w.dtype)
    return jnp.concatenate([qkv, z, pin, gd, gp, b, a, pad], axis=1).astype(BF16)


def _forward(x, c, ctx, c_ctx, w_mod, b_mod, norm_mix, w_in, conv_w, a_log, dt_bias, dn_out_norm, pool_w, pool_scale,
             w_up_dn, w_up_pool, w_out, norm_ffn, peer_wq, peer_keys, peer_u, peer_v, final_norm):
    seq, nctx = x.shape[1], ctx.shape[1]
    nlat = BATCH * seq
    xall = jnp.concatenate([x.reshape(nlat, D_MODEL), ctx.reshape(BATCH * nctx, D_MODEL)], axis=0)
    rows = xall.shape[0]
    cvec = jnp.concatenate([c, c_ctx[None, :], jnp.zeros((8 - BATCH - 1, D_MODEL), F32)], axis=0)
    modv_all = _modulation(cvec, w_mod, b_mod)
    for i in range(DEPTH):
        last = i == DEPTH - 1
        rows_out = nlat if last else rows
        modv = modv_all[i]
        qkv, z, pin, gd, gp, ba = _inproj(xall, modv, norm_mix[i], _reorder_in_weight(w_in[i]), seq)
        q, k, v, bg = _dnconv(qkv, ba, conv_w[i], a_log[i], dt_bias[i], seq, nctx)
        o_f, o_b = _dnscan(*_dnprep(q, k, v, bg, seq, nctx), seq, nctx)
        xall = _merge(xall, o_f, o_b, z, pin, gd, gp, modv, dn_out_norm[i], pool_w[i], pool_scale[i],
                      w_up_dn[i].astype(BF16), w_up_pool[i].astype(BF16), w_out[i].astype(BF16), seq, nctx, rows_out)
        hn, n1, e1, r2, e2 = _peer_route(xall, modv, norm_ffn[i], peer_wq[i].astype(BF16), peer_keys[i], seq, rows_out)
        xall = _peer_dense(xall, modv, hn, peer_u[i].astype(BF16), peer_v[i].T.astype(BF16), n1, e1, r2, e2,
                           final_norm, seq, rows_out, last)
    return xall.reshape(BATCH, seq, D_MODEL)


def kernel(x, c, ctx, c_ctx, w_mod, b_mod, norm_mix, w_in, conv_w, a_log, dt_bias, dn_out_norm, pool_w, pool_scale, w_up_dn, w_up_pool, w_out, norm_ffn, peer_wq, peer_keys, peer_u, peer_v, final_norm):
    return _forward(x, c, ctx, c_ctx, w_mod, b_mod, norm_mix, w_in, conv_w, a_log, dt_bias, dn_out_norm, pool_w,
                    pool_scale, w_up_dn, w_up_pool, w_out, norm_ffn, peer_wq, peer_keys, peer_u, peer_v, final_norm)
```

```python
import functools

import numpy as np
import jax
import jax.numpy as jnp
from jax import lax
from jax.experimental import pallas as pl
from jax.experimental.pallas import tpu as pltpu

D_MODEL = 1024
BATCH = 2
DEPTH = 2
GRID_W = 64
EPS = 1e-6

DN_HEADS = 4
DN_HEAD_DIM = 128
DN_WIDTH = DN_HEADS * DN_HEAD_DIM
SHORT_CONV = 4
DN_CHUNK = 64
DN_CHAINS = 2 * DN_HEADS
DN_PREP_CHUNKS = 2
DN_SCAN_CHUNKS = 4
DN_LOCKSTEP = 16

POOL_WINDOWS = (2, 4, 8, 16)
POOL_GROUPS = 4
POOL_WIDTH = D_MODEL // 2
POOL_GW = POOL_WIDTH // POOL_GROUPS

PEER_HEADS = 8
N_KEYS = 128
N_EXPERTS = N_KEYS * N_KEYS
PEER_TOPK = 16
PEER_QDIM = 256
PEER_HALF = PEER_QDIM // 2

BA_PAD = 128
IN_COLS_R = 3 * DN_WIDTH + DN_WIDTH + POOL_WIDTH + 2 * D_MODEL + BA_PAD

LANES = 128
SUBLANES = 8
ROW_TILE = 256
INPROJ_TILE = 512
PEER_TOK_TILE = 512
PEER_EXP_TILE = 2048
PEER_SLAB = 256
HALO = 8
VMEM_LIMIT = 56 * 1024 * 1024

F32 = jnp.float32
BF16 = jnp.bfloat16
HIGHEST = lax.Precision.HIGHEST
NEG_INF = float("-inf")
NT_DIMS = (((1,), (1,)), ((), ()))


def _cparams(*sem):
    return pltpu.CompilerParams(dimension_semantics=sem, vmem_limit_bytes=VMEM_LIMIT)


def _group_of_row(row0, seq):
    return jnp.where(row0 < seq, 0, jnp.where(row0 < 2 * seq, 1, 2))


def _silu(x):
    return x * jax.nn.sigmoid(x)


def _split_bf16(a):
    hi = a.astype(BF16)
    lo = (a - hi.astype(F32)).astype(BF16)
    return hi, lo


def _bf16_pair_word(x):
    hi = lax.bitcast_convert_type(x.astype(BF16).astype(F32), jnp.uint32)
    return hi | (hi >> 16)


def _pair_word_rows(row):
    tile = pltpu.bitcast(jnp.broadcast_to(row, (SUBLANES, row.shape[1])), BF16)
    return jnp.concatenate([tile] * (N_KEYS // tile.shape[0]), axis=0)


def _dot_split(a, b):
    ah, al = _split_bf16(a)
    bh, bl = _split_bf16(b)
    a4 = jnp.concatenate([ah, al, ah, al], axis=1)
    b4 = jnp.concatenate([bh, bh, bl, bl], axis=0)
    return jnp.dot(a4, b4, preferred_element_type=F32)


def _mod_kernel(c_ref, w_ref, b_ref, o_ref):
    o_ref[0] = jnp.dot(_silu(c_ref[...]), w_ref[0], preferred_element_type=F32) + b_ref[0]


def _modulation(cvec, w_mod, b_mod):
    tn = 1536
    return pl.pallas_call(
        _mod_kernel,
        grid=(DEPTH, 6 * D_MODEL // tn),
        in_specs=[
            pl.BlockSpec((8, D_MODEL), lambda l, j: (0, 0)),
            pl.BlockSpec((1, D_MODEL, tn), lambda l, j: (l, 0, j)),
            pl.BlockSpec((1, 1, tn), lambda l, j: (l, 0, j)),
        ],
        out_specs=pl.BlockSpec((1, 8, tn), lambda l, j: (l, 0, j)),
        out_shape=jax.ShapeDtypeStruct((DEPTH, 8, 6 * D_MODEL), F32),
        compiler_params=_cparams("arbitrary", "arbitrary"),
        name="modulation",
    )(cvec, w_mod, b_mod.reshape(DEPTH, 1, 6 * D_MODEL))


def _norm_mod(x, nw, mod, k):
    ms = jnp.mean(x * x, axis=-1, keepdims=True)
    xn = x * lax.rsqrt(ms + EPS) * nw
    sh = mod[:, k * D_MODEL:(k + 1) * D_MODEL]
    sc = mod[:, (k + 1) * D_MODEL:(k + 2) * D_MODEL]
    return xn * (1 + sc) + sh


def _inproj_kernel(x_ref, modv_ref, nw_ref, w_ref, qkv_ref, z_ref, pin_ref, gd_ref, gp_ref, ba_ref, *, seq):
    g = _group_of_row(pl.program_id(0) * INPROJ_TILE, seq)
    mod = modv_ref[pl.ds(g, 1), :]
    h = _norm_mod(x_ref[...], nw_ref[...], mod, 0)
    y = jnp.dot(h.astype(BF16), w_ref[...], preferred_element_type=F32)
    o = 0
    for ref in (qkv_ref, z_ref, pin_ref, gd_ref, gp_ref, ba_ref):
        n = ref.shape[1]
        ref[...] = y[:, o:o + n]
        o += n


def _inproj(xall, modv, norm_w, w_in_r, seq):
    rows = xall.shape[0]
    widths = (3 * DN_WIDTH, DN_WIDTH, POOL_WIDTH, D_MODEL, D_MODEL, BA_PAD)
    return pl.pallas_call(
        functools.partial(_inproj_kernel, seq=seq),
        grid=(rows // INPROJ_TILE,),
        in_specs=[
            pl.BlockSpec((INPROJ_TILE, D_MODEL), lambda i: (i, 0)),
            pl.BlockSpec((8, 6 * D_MODEL), lambda i: (0, 0)),
            pl.BlockSpec((1, D_MODEL), lambda i: (0, 0)),
            pl.BlockSpec((D_MODEL, IN_COLS_R), lambda i: (0, 0)),
        ],
        out_specs=[pl.BlockSpec((INPROJ_TILE, n), lambda i: (i, 0)) for n in widths],
        out_shape=[jax.ShapeDtypeStruct((rows, n), F32) for n in widths],
        compiler_params=_cparams("arbitrary"),
        name="inproj",
    )(xall, modv, norm_w.reshape(1, D_MODEL), w_in_r)


def _dn_inputs(cur, prev, nxt, cw, ba, alog, dtb):
    tile = cur.shape[0]
    ext = jnp.concatenate([prev, cur, nxt], axis=0)
    left = SHORT_CONV // 2
    y = None
    for j in range(SHORT_CONV):
        o = HALO - left + j
        term = ext[o:o + tile, :] * cw[j:j + 1, :]
        y = term if y is None else y + term
    y = _silu(y)
    qs, ks = [], []
    for hh in range(DN_HEADS):
        lo, hi = hh * DN_HEAD_DIM, (hh + 1) * DN_HEAD_DIM
        qh = y[:, lo:hi]
        kh = y[:, DN_WIDTH + lo:DN_WIDTH + hi]
        qs.append(qh * lax.rsqrt(jnp.sum(qh * qh, axis=-1, keepdims=True) + EPS) * (DN_HEAD_DIM ** -0.5))
        ks.append(kh * lax.rsqrt(jnp.sum(kh * kh, axis=-1, keepdims=True) + EPS))
    beta = jax.nn.sigmoid(ba)
    xs = ba + dtb
    softplus = jnp.maximum(xs, 0.0) + jnp.log(1.0 + jnp.exp(-jnp.abs(xs)))
    gdec = -jnp.exp(alog) * softplus
    col = lax.broadcasted_iota(jnp.int32, ba.shape, 1)
    bg = jnp.where(col < 2 * DN_HEADS, beta, jnp.where(col < 4 * DN_HEADS, gdec, 0.0))
    return qs, ks, y[:, 2 * DN_WIDTH:], bg


def _dnprep_kernel(cur_ref, prev_ref, next_ref, cw_ref, ba_ref, alog_ref, dtb_ref, wq_ref, u_ref, lk_ref, eg_ref,
                   *, seq, ctx):
    c = DN_CHUNK
    tile = DN_PREP_CHUNKS * c
    row0 = pl.program_id(0) * tile
    nlat = BATCH * seq
    is_start = (row0 == 0) | (row0 == seq) | (row0 == nlat) | (row0 == nlat + ctx)
    row1 = row0 + tile
    is_end = (row1 == seq) | (row1 == nlat) | (row1 == nlat + ctx) | (row1 == nlat + BATCH * ctx)
    qs, ks, v_all, bg_all = _dn_inputs(cur_ref[...], jnp.where(is_start, 0.0, prev_ref[...]),
                                       jnp.where(is_end, 0.0, next_ref[...]), cw_ref[...], ba_ref[...],
                                       alog_ref[...], dtb_ref[...])
    ri = lax.broadcasted_iota(jnp.int32, (c, c), 0)
    ci = lax.broadcasted_iota(jnp.int32, (c, c), 1)
    dirs = ((ri >= ci, ri > ci, ci >= ri, c - 1), (ri <= ci, ri < ci, ci <= ri, 0))
    chains = []
    for j in range(DN_PREP_CHUNKS):
        rows = slice(j * c, (j + 1) * c)
        bg = bg_all[rows, :]
        bgh, bgl = _split_bf16(bg)
        bgth, bgtl = _split_bf16(bg.T[0:4 * DN_HEADS, :])
        bg2 = jnp.concatenate([bgh, bgl], axis=0)
        bgt2 = jnp.concatenate([bgth, bgtl], axis=1)
        egs = []
        for d, (incl, strict, incl_t, last) in enumerate(dirs):
            m = incl.astype(BF16)
            mt = incl_t.astype(BF16)
            gc_all = jnp.dot(jnp.concatenate([m, m], axis=1), bg2, preferred_element_type=F32)
            gr_all = jnp.dot(bgt2, jnp.concatenate([mt, mt], axis=0), preferred_element_type=F32)
            for hh in range(DN_HEADS):
                lo, hi = hh * DN_HEAD_DIM, (hh + 1) * DN_HEAD_DIM
                ch = d * DN_HEADS + hh
                gcol = 2 * DN_HEADS + ch
                q = qs[hh][rows, :]
                k = ks[hh][rows, :]
                v = v_all[rows, lo:hi]
                beta = bg[:, ch:ch + 1]
                gc = gc_all[:, gcol:gcol + 1]
                gr = gr_all[gcol:gcol + 1, :]
                glast = gc_all[last:last + 1, gcol:gcol + 1]
                decay = jnp.where(incl, jnp.exp(jnp.minimum(gc - gr, 0.0)), 0.0)
                kk = lax.dot_general(k, k, NT_DIMS, preferred_element_type=F32)
                qk = lax.dot_general(q, k, NT_DIMS, preferred_element_type=F32)
                egc = jnp.exp(gc)
                cols = slice(ch * DN_HEAD_DIM, (ch + 1) * DN_HEAD_DIM)
                wq_ref[0, j, c:2 * c, cols] = (q * egc).astype(BF16)
                lk_ref[0, j, ch, 0:c, :] = jnp.where(incl, qk * decay, 0.0).astype(BF16)
                lk_ref[0, j, ch, c:, :] = (k * jnp.exp(glast - gc)).T.astype(BF16)
                egs.append(jnp.broadcast_to(jnp.exp(glast), (1, LANES)))
                chains.append((j, cols, jnp.where(strict, -(beta * kk * decay), 0.0),
                               jnp.concatenate([k * (beta * egc), v * beta], axis=1)))
        eg_ref[0, j] = jnp.concatenate(egs, axis=0)
    for g0 in range(0, len(chains), DN_LOCKSTEP):
        group = chains[g0:g0 + DN_LOCKSTEP]
        ps = [ch[2] for ch in group]
        eye = (ri == ci).astype(F32)
        xs = [eye + p for p in ps]
        for _ in range(int(np.log2(c)) - 1):
            ps = [_dot_split(p, p) for p in ps]
            xs = [x + _dot_split(x, p) for x, p in zip(xs, ps)]
        ys = [_dot_split(x, ch[3]) for x, ch in zip(xs, group)]
        for (j, cols, _, _), y in zip(group, ys):
            wq_ref[0, j, 0:c, cols] = y[:, :DN_HEAD_DIM].astype(BF16)
            u_ref[0, j, :, cols] = y[:, DN_HEAD_DIM:]


def _dnprep(qkv, ba, conv_w, a_log, dt_bias, seq, ctx):
    rows = qkv.shape[0]
    c = DN_CHUNK
    n = DN_PREP_CHUNKS
    tile = n * c
    nct, nlt = ctx // c, seq // c
    assert nct % n == 0 and nlt % n == 0
    nch = nct + nlt
    nh = tile // HALO
    last = rows // HALO - 1
    pad = jnp.zeros((2 * DN_HEADS,), F32)
    tail = jnp.zeros((BA_PAD - 4 * DN_HEADS,), F32)
    alog = jnp.concatenate([pad, a_log.reshape(-1), tail]).reshape(1, BA_PAD)
    dtb = jnp.concatenate([pad, dt_bias.reshape(-1), tail]).reshape(1, BA_PAD)
    w3 = 3 * DN_WIDTH

    def seq_pos(i):
        i = i * n
        lat = i < BATCH * nlt
        j = i - BATCH * nlt
        return jnp.where(lat, i // nlt, j // nct), jnp.where(lat, nct + i % nlt, j % nct) // n

    def out_spec(*tail_dims):
        return pl.BlockSpec((1, n) + tail_dims, lambda i: seq_pos(i) + (0,) * len(tail_dims))

    width = DN_CHAINS * DN_HEAD_DIM
    return pl.pallas_call(
        functools.partial(_dnprep_kernel, seq=seq, ctx=ctx),
        grid=(rows // tile,),
        in_specs=[pl.BlockSpec((tile, w3), lambda i: (i, 0)),
                  pl.BlockSpec((HALO, w3), lambda i: (jnp.maximum(i * nh - 1, 0), 0)),
                  pl.BlockSpec((HALO, w3), lambda i: (jnp.minimum((i + 1) * nh, last), 0)),
                  pl.BlockSpec((SHORT_CONV, w3), lambda i: (0, 0)),
                  pl.BlockSpec((tile, BA_PAD), lambda i: (i, 0)),
                  pl.BlockSpec((1, BA_PAD), lambda i: (0, 0)),
                  pl.BlockSpec((1, BA_PAD), lambda i: (0, 0))],
        out_specs=[out_spec(2 * c, width), out_spec(c, width), out_spec(DN_CHAINS, c + DN_HEAD_DIM, c),
                   out_spec(DN_CHAINS, LANES)],
        out_shape=[jax.ShapeDtypeStruct((BATCH, nch, 2 * c, width), BF16),
                   jax.ShapeDtypeStruct((BATCH, nch, c, width), F32),
                   jax.ShapeDtypeStruct((BATCH, nch, DN_CHAINS, c + DN_HEAD_DIM, c), BF16),
                   jax.ShapeDtypeStruct((BATCH, nch, DN_CHAINS, LANES), F32)],
        compiler_params=_cparams("arbitrary"),
        name="dnprep",
    )(qkv, qkv, qkv, conv_w, ba, alog, dtb)


def _dnscan_kernel(wqf_ref, uf_ref, lkf_ref, egf_ref, wqb_ref, ub_ref, lkb_ref, egb_ref, of_ref, ob_ref, s_ref):
    @pl.when(pl.program_id(0) == 0)
    def _():
        s_ref[...] = jnp.zeros_like(s_ref)

    c = DN_CHUNK
    dirs = ((wqf_ref, uf_ref, lkf_ref, egf_ref, of_ref), (wqb_ref, ub_ref, lkb_ref, egb_ref, ob_ref))
    chains = [(d, b, hh) for d in range(2) for b in range(BATCH) for hh in range(DN_HEADS)]
    sidx = lambda d, b, hh: (b * 2 + d) * DN_HEADS + hh
    cols = lambda hh: slice(hh * DN_HEAD_DIM, (hh + 1) * DN_HEAD_DIM)
    for j in range(DN_SCAN_CHUNKS):
        pos = (j, DN_SCAN_CHUNKS - 1 - j)
        ss = [s_ref[sidx(*ch)] for ch in chains]
        r1 = [jnp.dot(dirs[d][0][b, pos[d], :, cols(hh)], s.astype(BF16), preferred_element_type=F32)
              for (d, b, hh), s in zip(chains, ss)]
        vn = [dirs[d][1][b, pos[d], :, cols(hh)] - r[:c] for (d, b, hh), r in zip(chains, r1)]
        r2 = [jnp.dot(dirs[d][2][b, pos[d], hh], v.astype(BF16), preferred_element_type=F32)
              for (d, b, hh), v in zip(chains, vn)]
        for (d, b, hh), s, a1, a2 in zip(chains, ss, r1, r2):
            dirs[d][4][b, pos[d], :, cols(hh)] = a1[c:] + a2[:c]
            row = d * DN_HEADS + hh
            s_ref[sidx(d, b, hh)] = s * dirs[d][3][b, pos[d], row:row + 1, :] + a2[c:]


def _dnscan(wq, u, lk, eg, seq, ctx):
    c = DN_CHUNK
    n = DN_SCAN_CHUNKS
    nct, nlt = ctx // c, seq // c
    assert nct % n == 0 and nlt % n == 0
    nch = nct + nlt
    nblk, nctb = nch // n, nct // n

    def bwd(s):
        return jnp.where(s < nctb, nctb - 1 - s, nctb + nblk - 1 - s)

    def specs(pos, d):
        return [pl.BlockSpec((BATCH, n, 2 * c, DN_WIDTH), lambda s: (0, pos(s), 0, d)),
                pl.BlockSpec((BATCH, n, c, DN_WIDTH), lambda s: (0, pos(s), 0, d)),
                pl.BlockSpec((BATCH, n, DN_HEADS, c + DN_HEAD_DIM, c), lambda s: (0, pos(s), d, 0, 0)),
                pl.BlockSpec((BATCH, n, DN_CHAINS, LANES), lambda s: (0, pos(s), 0, 0))]

    fwd = lambda s: s
    return pl.pallas_call(
        _dnscan_kernel,
        grid=(nblk,),
        in_specs=specs(fwd, 0) + specs(bwd, 1),
        out_specs=[pl.BlockSpec((BATCH, n, c, DN_WIDTH), lambda s: (0, s, 0, 0)),
                   pl.BlockSpec((BATCH, n, c, DN_WIDTH), lambda s: (0, bwd(s), 0, 0))],
        out_shape=[jax.ShapeDtypeStruct((BATCH, nch, c, DN_WIDTH), F32)] * 2,
        scratch_shapes=[pltpu.VMEM((BATCH * DN_CHAINS, DN_HEAD_DIM, DN_HEAD_DIM), F32)],
        compiler_params=_cparams("arbitrary"),
        name="dnscan",
    )(wq, u, lk, eg, wq, u, lk, eg)


def _pool_matrices(tile, seg):
    t = np.arange(tile)
    p = t % seg
    mats = []
    for win in POOL_WINDOWS:
        lo = np.clip(p - win // 2, 0, seg)
        hi = np.clip(p + win - win // 2, 0, seg)
        same = (t[:, None] // seg) == (t[None, :] // seg)
        inside = same & (p[None, :] >= lo[:, None]) & (p[None, :] < hi[:, None])
        mats.append(inside / (hi - lo)[:, None].astype(np.float64) - np.eye(tile))
    return np.stack(mats).astype(np.float32)


def _merge_kernel(x_ref, of_ref, ob_ref, z_ref, pin_ref, gd_ref, gp_ref, modv_ref, on_ref, pm_ref, pw_ref, ps_ref,
                  wud_ref, wup_ref, wo_ref, o_ref, *, seq):
    g = _group_of_row(pl.program_id(0) * ROW_TILE, seq)
    mod = modv_ref[pl.ds(g, 1), :]
    gate = mod[:, 2 * D_MODEL:3 * D_MODEL]
    o = (of_ref[0] + ob_ref[0]).reshape(ROW_TILE, DN_WIDTH)
    z = z_ref[...]
    onw = on_ref[...]
    ys = []
    for hh in range(DN_HEADS):
        lo, hi = hh * DN_HEAD_DIM, (hh + 1) * DN_HEAD_DIM
        oh = o[:, lo:hi]
        ys.append(oh * lax.rsqrt(jnp.mean(oh * oh, axis=-1, keepdims=True) + EPS) * onw * _silu(z[:, lo:hi]))
    y_dn = jnp.concatenate(ys, axis=1)
    pin = pin_ref[...]
    yp = []
    for gi in range(POOL_GROUPS):
        lo, hi = gi * POOL_GW, (gi + 1) * POOL_GW
        pooled = jnp.dot(pm_ref[0, gi], pin[:, lo:hi], precision=HIGHEST, preferred_element_type=F32)
        yp.append(jnp.dot(pooled, pw_ref[gi], preferred_element_type=F32))
    y_pool = jnp.concatenate(yp, axis=1) * ps_ref[...]
    m = (jax.nn.sigmoid(gd_ref[...]) * jnp.dot(y_dn.astype(BF16), wud_ref[...], preferred_element_type=F32)
         + jax.nn.sigmoid(gp_ref[...]) * jnp.dot(y_pool.astype(BF16), wup_ref[...], preferred_element_type=F32))
    out = jnp.dot(m.astype(BF16), wo_ref[...], preferred_element_type=F32)
    o_ref[...] = x_ref[...] + gate * out


def _merge(xall, o_f, o_b, z, pin, gd, gp, modv, onorm, pool_w, pool_scale, wud, wup, wo, seq, ctx, rows_out):
    assert ctx == ROW_TILE and seq % ROW_TILE == 0
    lat_tiles = BATCH * seq // ROW_TILE
    tiles_per_seq = seq // ROW_TILE
    cpt = ROW_TILE // DN_CHUNK
    pm = jnp.asarray(np.stack([_pool_matrices(ROW_TILE, GRID_W), _pool_matrices(ROW_TILE, ctx)]))
    row = lambda n: pl.BlockSpec((ROW_TILE, n), lambda i: (i, 0))
    full = lambda *s: pl.BlockSpec(s, lambda i: (0,) * len(s))

    def o_pos(i):
        lat = i < lat_tiles
        return (jnp.where(lat, i // tiles_per_seq, i - lat_tiles), jnp.where(lat, 1 + i % tiles_per_seq, 0), 0, 0)

    o_spec = pl.BlockSpec((1, cpt, DN_CHUNK, DN_WIDTH), o_pos)
    return pl.pallas_call(
        functools.partial(_merge_kernel, seq=seq),
        grid=(rows_out // ROW_TILE,),
        in_specs=[
            row(D_MODEL), o_spec, o_spec, row(DN_WIDTH), row(POOL_WIDTH), row(D_MODEL), row(D_MODEL),
            full(8, 6 * D_MODEL), full(1, DN_HEAD_DIM),
            pl.BlockSpec((1, POOL_GROUPS, ROW_TILE, ROW_TILE), lambda i: (jnp.where(i >= lat_tiles, 1, 0), 0, 0, 0)),
            full(POOL_GROUPS, POOL_GW, POOL_GW), full(1, POOL_WIDTH),
            full(DN_WIDTH, D_MODEL), full(POOL_WIDTH, D_MODEL), full(D_MODEL, D_MODEL),
        ],
        out_specs=row(D_MODEL),
        out_shape=jax.ShapeDtypeStruct((rows_out, D_MODEL), F32),
        compiler_params=_cparams("arbitrary"),
        name="merge",
    )(xall, o_f, o_b, z, pin, gd, gp, modv, onorm.reshape(1, DN_HEAD_DIM), pm, pool_w,
      pool_scale.reshape(1, POOL_WIDTH), wud, wup, wo)


def _col_max(x):
    return jnp.max(x, axis=0, keepdims=True)


def _col_min(x):
    return jnp.min(x, axis=0, keepdims=True)


def _oddeven_merge(lo, hi, r):
    step = r * 2
    if step < hi - lo:
        yield from _oddeven_merge(lo, hi, step)
        yield from _oddeven_merge(lo + r, hi, step)
        yield from [(i, i + r) for i in range(lo + r, hi - r, step)]
    else:
        yield (lo, lo + r)


def _oddeven_merge_sort(lo, hi):
    if hi - lo >= 1:
        mid = lo + (hi - lo) // 2
        yield from _oddeven_merge_sort(lo, mid)
        yield from _oddeven_merge_sort(mid + 1, hi)
        yield from _oddeven_merge(lo, hi, 1)


def _exchange(v, i, j):
    v[i], v[j] = jnp.maximum(v[i], v[j]), jnp.minimum(v[i], v[j])


def _sorted_top_many(block_lists):
    k = PEER_TOPK
    vs = [list(blocks) for blocks in block_lists]
    for i, j in _oddeven_merge_sort(0, k - 1):
        for v in vs:
            _exchange(v, i, j)
    shift = SUBLANES // 2
    while shift >= 1:
        ws = [[pltpu.roll(x, shift, 0) for x in v] for v in vs]
        vs = [[jnp.maximum(v[j], w[k - 1 - j]) for j in range(k)] for v, w in zip(vs, ws)]
        stride = k // 2
        while stride >= 1:
            for i in range(k):
                if i & stride == 0:
                    for v in vs:
                        _exchange(v, i, i + stride)
            stride //= 2
        shift //= 2
    return vs


def _sublane_total(x):
    shift = SUBLANES // 2
    while shift >= 1:
        x = x + pltpu.roll(x, shift, 0)
        shift //= 2
    return x


def _unambiguous(blocks, top):
    ok = top[0] > top[1]
    for a in range(1, PEER_TOPK - 1):
        ok = ok & (top[a] > top[a + 1])
    ge = None
    for blk in blocks:
        one = jnp.where(blk >= top[PEER_TOPK - 1], 1.0, 0.0)
        ge = one if ge is None else ge + one
    return ok & (_sublane_total(ge) == float(PEER_TOPK))


def _top_select(scores, iota):
    n = float(scores[0].shape[0])
    cur = list(scores)
    idxs = [[] for _ in cur]
    for _ in range(PEER_TOPK):
        for j in range(len(cur)):
            m = _col_max(cur[j])
            cand = jnp.where(cur[j] == m, iota, n)
            idx = _col_min(cand)
            cur[j] = jnp.where(cand == idx, NEG_INF, cur[j])
            idxs[j].append(idx)
    return [jnp.concatenate(i, axis=0) for i in idxs]


def _peer_route_kernel(x_ref, modv_ref, nw_ref, wq_ref, keys_ref, hn_ref, n1_ref, e1_ref, r2_ref, e2_ref,
                       q_scr, *, seq):
    g = _group_of_row(pl.program_id(0) * ROW_TILE, seq)
    mod = modv_ref[pl.ds(g, 1), :]
    hn = _norm_mod(x_ref[...], nw_ref[...], mod, 3).astype(BF16)
    hn_ref[...] = hn
    q = jnp.dot(hn, wq_ref[...], preferred_element_type=F32)
    nsub = ROW_TILE // LANES
    for j in range(2 * PEER_HEADS):
        for t in range(nsub):
            q_scr[j, t] = q[t * LANES:(t + 1) * LANES, j * PEER_HALF:(j + 1) * PEER_HALF]
    k = PEER_TOPK
    nblk = N_KEYS // SUBLANES
    iota_k = lax.broadcasted_iota(jnp.int32, (N_KEYS, LANES), 0).astype(F32)
    iota_a = lax.broadcasted_iota(jnp.int32, (k, LANES), 0).astype(F32)
    iota_s = lax.broadcasted_iota(jnp.int32, (SUBLANES, LANES), 0).astype(F32)
    split = lambda s: [s[SUBLANES * j:SUBLANES * (j + 1), :] for j in range(nblk)]
    rep = lambda row: jnp.broadcast_to(row, (SUBLANES, LANES))

    def head(hh, carry):
        subs = range(nsub)
        s1 = [lax.dot_general(keys_ref[2 * hh], q_scr[2 * hh, t], NT_DIMS, preferred_element_type=F32) for t in subs]
        s2 = [lax.dot_general(keys_ref[2 * hh + 1], q_scr[2 * hh + 1, t], NT_DIMS, preferred_element_type=F32)
              for t in subs]
        b1, b2 = [split(x) for x in s1], [split(x) for x in s2]
        tops = _sorted_top_many(b1 + b2)
        top1, top2 = tops[:nsub], tops[nsub:]
        c = [jnp.concatenate([x[0:1, :] for x in top], axis=0) for top in top1]
        d = [jnp.concatenate([x[0:1, :] for x in top], axis=0) for top in top2]

        def grid_cells(c, d):
            cells = [c[0:1, :] + d[0:SUBLANES, :], c[0:1, :] + d[SUBLANES:k, :]]
            for a in range(1, SUBLANES):
                cells.append(jnp.where(iota_s < float(k // (a + 1)), c[a:a + 1, :] + d[0:SUBLANES, :], NEG_INF))
            cells.append(c[SUBLANES:k, :] + d[0:1, :])
            return cells + [jnp.full((SUBLANES, LANES), NEG_INF, F32)] * (k - len(cells))

        cuts = [x[k - 1][0:1, :] for x in _sorted_top_many([grid_cells(c[t], d[t]) for t in subs])]
        cnt_cut = [jnp.zeros((k, LANES), F32) for _ in subs]
        for b in range(k):
            cnt_cut = [acc + jnp.where(c[t] + d[t][b:b + 1, :] >= cuts[t], 1.0, 0.0) for t, acc in enumerate(cnt_cut)]

        def by_merge():
            cnts = [jnp.zeros((k, LANES), F32) for _ in subs]
            for _ in range(k):
                for t in subs:
                    dn = jnp.full((k, LANES), NEG_INF, F32)
                    for b in range(k):
                        dn = jnp.where(cnts[t] == float(b), d[t][b:b + 1, :], dn)
                    f = c[t] + dn
                    cand = jnp.where(f == _col_max(f), iota_a, float(k))
                    cnts[t] = cnts[t] + (cand == _col_min(cand)).astype(F32)
            return tuple(cnts)

        exact_cut = jnp.all(jnp.concatenate([jnp.sum(x, axis=0, keepdims=True) for x in cnt_cut], axis=0) == float(k))
        cnt = lax.cond(exact_cut, lambda: tuple(cnt_cut), by_merge)
        zsum = []
        for t in subs:
            e1c = jnp.exp(c[t] - c[t][0:1, :])
            e2d = jnp.exp(d[t] - d[t][0:1, :])
            part = jnp.zeros((k, LANES), F32)
            for b in range(k):
                part = part + jnp.where(cnt[t] > float(b), e2d[b:b + 1, :], 0.0)
            zsum.append(jnp.sum(e1c * part, axis=0, keepdims=True))

        def by_value():
            n1b = [[jnp.zeros((SUBLANES, LANES), F32)] * nblk for _ in subs]
            r2b = [[jnp.full((SUBLANES, LANES), float(k), F32)] * nblk for _ in subs]
            for a in range(k):
                for t in subs:
                    cnt_a = rep(cnt[t][a:a + 1, :])
                    n1b[t] = [jnp.where(blk == top1[t][a], cnt_a, acc) for blk, acc in zip(b1[t], n1b[t])]
                    r2b[t] = [jnp.where(blk == top2[t][a], float(a), acc) for blk, acc in zip(b2[t], r2b[t])]
            return tuple(jnp.concatenate(x, axis=0) for x in n1b + r2b)

        def by_index():
            idx = _top_select(s1 + s2, iota_k)
            out1, out2 = [], []
            for t in subs:
                n1 = jnp.zeros((N_KEYS, LANES), F32)
                r2 = jnp.full((N_KEYS, LANES), float(k), F32)
                for a in range(k):
                    n1 = jnp.where(iota_k == idx[t][a:a + 1, :], cnt[t][a:a + 1, :], n1)
                    r2 = jnp.where(iota_k == idx[nsub + t][a:a + 1, :], float(a), r2)
                out1.append(n1)
                out2.append(r2)
            return tuple(out1 + out2)

        clean = None
        for t in subs:
            ok = _unambiguous(b1[t], top1[t]) & _unambiguous(b2[t], top2[t])
            clean = ok if clean is None else clean & ok
        ranks = lax.cond(jnp.all(clean), by_value, by_index)
        for t in subs:
            lanes = slice(t * LANES, (t + 1) * LANES)
            n1_ref[hh, 0, :, lanes] = _bf16_pair_word(ranks[t])
            e1_ref[hh, 0, :, lanes] = _bf16_pair_word(0.5 * jnp.exp(s1[t] - c[t][0:1, :]) / zsum[t])
            r2_ref[hh, 0, :, lanes] = ranks[nsub + t].astype(BF16)
            e2_ref[hh, 0, :, lanes] = jnp.exp(s2[t] - d[t][0:1, :]).astype(BF16)
        return carry

    lax.fori_loop(0, PEER_HEADS, head, 0)


def _peer_route(xall, modv, norm_w, wq, keys, seq, rows_out):
    row = lambda n: pl.BlockSpec((ROW_TILE, n), lambda i: (i, 0))
    full = lambda *s: pl.BlockSpec(s, lambda i: (0,) * len(s))
    tab = pl.BlockSpec((PEER_HEADS, 1, N_KEYS, ROW_TILE), lambda i: (0, i, 0, 0))
    tab_shape = lambda dt: jax.ShapeDtypeStruct((PEER_HEADS, rows_out // ROW_TILE, N_KEYS, ROW_TILE), dt)
    return pl.pallas_call(
        functools.partial(_peer_route_kernel, seq=seq),
        grid=(rows_out // ROW_TILE,),
        in_specs=[row(D_MODEL), full(8, 6 * D_MODEL), full(1, D_MODEL), full(D_MODEL, PEER_HEADS * PEER_QDIM),
                  full(2 * PEER_HEADS, N_KEYS, PEER_HALF)],
        out_specs=[row(D_MODEL), tab, tab, tab, tab],
        out_shape=[jax.ShapeDtypeStruct((rows_out, D_MODEL), BF16), tab_shape(jnp.uint32), tab_shape(jnp.uint32),
                   tab_shape(BF16), tab_shape(BF16)],
        scratch_shapes=[pltpu.VMEM((2 * PEER_HEADS, ROW_TILE // LANES, LANES, PEER_HALF), F32)],
        compiler_params=_cparams("arbitrary"),
        name="peer_route",
    )(xall, modv, norm_w.reshape(1, D_MODEL), wq, keys.reshape(2 * PEER_HEADS, N_KEYS, PEER_HALF))


PEER_PIPE_LAG = 2
PEER_MXU_BLOCK = 256


def _peer_tile(n, lag, n_tiles, n_exp):
    m = jnp.clip(n - lag, 0, n_tiles - 1)
    return m // n_exp, m % n_exp


def _peer_dense_kernel(x_ref, modv_ref, hn_ref, u_ref, vt_ref, n1_ref, e1_ref, r2_ref, e2_ref, fn_ref, o_ref,
                       acc_ref, a0_ref, a1_ref, act0_ref, act1_ref, hn_scr, *, seq, final, n_tiles, n_exp):
    n = pl.program_id(0)
    t_c, e_c = _peer_tile(n, PEER_PIPE_LAG, n_tiles, n_exp)

    @pl.when(n == 0)
    def _():
        for ref in (acc_ref, a0_ref, a1_ref, act0_ref, act1_ref):
            ref[...] = jnp.zeros_like(ref)

    @pl.when(_peer_tile(n, 0, n_tiles, n_exp)[1] == 0)
    def _():
        hn_scr[...] = hn_ref[...]

    def step(a_next, a_cur, act_next, act_cur):
        zero = jnp.zeros((), BF16)
        blk = PEER_MXU_BLOCK

        def stage_a(rb, cb):
            rows, cols = slice(rb * blk, (rb + 1) * blk), slice(cb * blk, (cb + 1) * blk)
            a_next[rows, cols] = lax.dot_general(u_ref[rows, :].astype(BF16), hn_scr[cols, :], NT_DIMS,
                                                 preferred_element_type=F32)

        def stage_b(t, i):
            lanes = slice(t * ROW_TILE, (t + 1) * ROW_TILE)
            wsum = None
            for hh in range(PEER_HEADS):
                keep = r2_ref[hh, t] < _pair_word_rows(n1_ref[hh, t, i:i + 1, :])
                term = jnp.where(keep, e2_ref[hh, t], zero) * _pair_word_rows(e1_ref[hh, t, i:i + 1, :])
                wsum = term if wsum is None else wsum + term
            rows = slice(i * N_KEYS, (i + 1) * N_KEYS)
            a = a_cur[rows, lanes]
            gelu = a * (1.0 + lax.erf(a * (2.0 ** -0.5)))
            act_next[rows, lanes] = gelu.astype(BF16) * wsum

        def stage_c(rb, cb):
            rows, cols = slice(rb * blk, (rb + 1) * blk), slice(cb * blk, (cb + 1) * blk)
            prev = jnp.where(e_c == 0, 0.0, acc_ref[rows, cols])
            acc_ref[rows, cols] = prev + jnp.dot(vt_ref[rows, :], act_cur[:, cols], preferred_element_type=F32)

        a_pieces = [(rb, cb) for cb in range(PEER_TOK_TILE // blk) for rb in range(PEER_EXP_TILE // blk)]
        c_pieces = [(rb, cb) for cb in range(PEER_TOK_TILE // blk) for rb in range(D_MODEL // blk)]
        b_pieces = [(t, i) for t in range(PEER_TOK_TILE // ROW_TILE) for i in range(PEER_EXP_TILE // N_KEYS)]
        a_per_c = len(a_pieces) // len(c_pieces)
        b_per_c = len(b_pieces) // len(c_pieces)
        assert a_per_c * len(c_pieces) == len(a_pieces) and b_per_c * len(c_pieces) == len(b_pieces)
        b_iter = iter(b_pieces)
        for k, c_piece in enumerate(c_pieces):
            mxu = [(stage_a, p) for p in a_pieces[k * a_per_c:(k + 1) * a_per_c]]
            mxu.insert(1, (stage_c, c_piece))
            quota = [b_per_c // len(mxu) + (1 if m < b_per_c % len(mxu) else 0) for m in range(len(mxu))]
            for (fn, piece), nb in zip(mxu, quota):
                fn(*piece)
                for _ in range(nb):
                    stage_b(*next(b_iter))

    step(a0_ref, a1_ref, act0_ref, act1_ref)
    a1_ref[...] = a0_ref[...]

    @pl.when(n >= 0)
    def _():
        act1_ref[...] = act0_ref[...]

    @pl.when((e_c == n_exp - 1) & (n >= PEER_PIPE_LAG))
    def _():
        g = _group_of_row(t_c * PEER_TOK_TILE, seq)
        mod = modv_ref[pl.ds(g, 1), :]
        y = x_ref[...] + mod[:, 5 * D_MODEL:6 * D_MODEL] * acc_ref[...].T
        if final:
            y = y * lax.rsqrt(jnp.mean(y * y, axis=-1, keepdims=True) + EPS) * fn_ref[...]
        o_ref[...] = y


def _peer_dense(xall, modv, hn, u_tab, vt_bf, n1, e1, r2, e2, final_norm, seq, rows_out, final):
    tt, et = PEER_TOK_TILE, PEER_EXP_TILE
    n_first = et // N_KEYS
    n_exp = N_EXPERTS // et
    n_tiles = (rows_out // tt) * n_exp
    tile = lambda lag: (lambda n: _peer_tile(n, lag, n_tiles, n_exp))
    ta, tb, tc = tile(0), tile(1), tile(PEER_PIPE_LAG)
    full = lambda *s: pl.BlockSpec(s, lambda n: (0,) * len(s))
    per_first = pl.BlockSpec((PEER_HEADS, tt // ROW_TILE, n_first, ROW_TILE), lambda n: (0, tb(n)[0], tb(n)[1], 0))
    per_second = pl.BlockSpec((PEER_HEADS, tt // ROW_TILE, N_KEYS, ROW_TILE), lambda n: (0, tb(n)[0], 0, 0))
    return pl.pallas_call(
        functools.partial(_peer_dense_kernel, seq=seq, final=final, n_tiles=n_tiles, n_exp=n_exp),
        grid=(n_tiles + PEER_PIPE_LAG,),
        in_specs=[pl.BlockSpec((tt, D_MODEL), lambda n: (tc(n)[0], 0)),
                  full(8, 6 * D_MODEL),
                  pl.BlockSpec((tt, D_MODEL), lambda n: (ta(n)[0], 0)),
                  pl.BlockSpec((et, D_MODEL), lambda n: (ta(n)[1], 0)),
                  pl.BlockSpec((D_MODEL, et), lambda n: (0, tc(n)[1])),
                  per_first, per_first, per_second, per_second, full(1, D_MODEL)],
        out_specs=pl.BlockSpec((tt, D_MODEL), lambda n: (tc(n)[0], 0)),
        out_shape=jax.ShapeDtypeStruct((rows_out, D_MODEL), F32),
        scratch_shapes=[pltpu.VMEM((D_MODEL, tt), F32), pltpu.VMEM((et, tt), F32), pltpu.VMEM((et, tt), F32),
                        pltpu.VMEM((et, tt), BF16), pltpu.VMEM((et, tt), BF16), pltpu.VMEM((tt, D_MODEL), BF16)],
        compiler_params=_cparams("arbitrary"),
        name="peer_dense",
    )(xall, modv, hn, u_tab, vt_bf, n1, e1, r2, e2, final_norm.reshape(1, D_MODEL))


def _reorder_in_weight(w):
    s = np.cumsum((0, 3 * DN_WIDTH, DN_WIDTH, POOL_WIDTH, 2 * DN_HEADS, 2 * DN_HEADS, D_MODEL, D_MODEL))
    qkv, z, pin, b, a, gd, gp = (w[:, s[i]:s[i + 1]] for i in range(7))
    pad = jnp.zeros((w.shape[0], BA_PAD - 4 * DN_HEADS), w.dtype)
    return jnp.concatenate([qkv, z, pin, gd, gp, b, a, pad], axis=1).astype(BF16)


def _forward(x, c, ctx, c_ctx, w_mod, b_mod, norm_mix, w_in, conv_w, a_log, dt_bias, dn_out_norm, pool_w, pool_scale,
             w_up_dn, w_up_pool, w_out, norm_ffn, peer_wq, peer_keys, peer_u, peer_v, final_norm):
    seq, nctx = x.shape[1], ctx.shape[1]
    nlat = BATCH * seq
    xall = jnp.concatenate([x.reshape(nlat, D_MODEL), ctx.reshape(BATCH * nctx, D_MODEL)], axis=0)
    rows = xall.shape[0]
    cvec = jnp.concatenate([c, c_ctx[None, :], jnp.zeros((8 - BATCH - 1, D_MODEL), F32)], axis=0)
    modv_all = _modulation(cvec, w_mod, b_mod)
    for i in range(DEPTH):
        last = i == DEPTH - 1
        rows_out = nlat if last else rows
        modv = modv_all[i]
        qkv, z, pin, gd, gp, ba = _inproj(xall, modv, norm_mix[i], _reorder_in_weight(w_in[i]), seq)
        o_f, o_b = _dnscan(*_dnprep(qkv, ba, conv_w[i], a_log[i], dt_bias[i], seq, nctx), seq, nctx)
        xall = _merge(xall, o_f, o_b, z, pin, gd, gp, modv, dn_out_norm[i], pool_w[i], pool_scale[i],
                      w_up_dn[i].astype(BF16), w_up_pool[i].astype(BF16), w_out[i].astype(BF16), seq, nctx, rows_out)
        hn, n1, e1, r2, e2 = _peer_route(xall, modv, norm_ffn[i], peer_wq[i].astype(BF16), peer_keys[i], seq, rows_out)
        xall = _peer_dense(xall, modv, hn, peer_u[i], peer_v[i].T.astype(BF16), n1, e1, r2, e2,
                           final_norm, seq, rows_out, last)
    return xall.reshape(BATCH, seq, D_MODEL)


def kernel(x, c, ctx, c_ctx, w_mod, b_mod, norm_mix, w_in, conv_w, a_log, dt_bias, dn_out_norm, pool_w, pool_scale, w_up_dn, w_up_pool, w_out, norm_ffn, peer_wq, peer_keys, peer_u, peer_v, final_norm):
    return _forward(x, c, ctx, c_ctx, w_mod, b_mod, norm_mix, w_in, conv_w, a_log, dt_bias, dn_out_norm, pool_w,
                    pool_scale, w_up_dn, w_up_pool, w_out, norm_ffn, peer_wq, peer_keys, peer_u, peer_v, final_norm)
```

```python
import functools

import numpy as np
import jax
import jax.numpy as jnp
from jax import lax
from jax.experimental import pallas as pl
from jax.experimental.pallas import tpu as pltpu

D_MODEL = 1024
BATCH = 2
DEPTH = 2
GRID_W = 64
EPS = 1e-6

DN_HEADS = 4
DN_HEAD_DIM = 128
DN_WIDTH = DN_HEADS * DN_HEAD_DIM
SHORT_CONV = 4
DN_CHUNK = 64
DN_CHAINS = 2 * DN_HEADS
DN_PREP_CHUNKS = 2
DN_SCAN_CHUNKS = 4
DN_LOCKSTEP = 16

POOL_WINDOWS = (2, 4, 8, 16)
POOL_GROUPS = 4
POOL_WIDTH = D_MODEL // 2
POOL_GW = POOL_WIDTH // POOL_GROUPS

PEER_HEADS = 8
N_KEYS = 128
N_EXPERTS = N_KEYS * N_KEYS
PEER_TOPK = 16
PEER_QDIM = 256
PEER_HALF = PEER_QDIM // 2

BA_PAD = 128
IN_COLS_R = 3 * DN_WIDTH + DN_WIDTH + POOL_WIDTH + 2 * D_MODEL + BA_PAD

LANES = 128
SUBLANES = 8
ROW_TILE = 256
INPROJ_TILE = 512
PEER_TOK_TILE = 512
PEER_EXP_TILE = 2048
HALO = SUBLANES
MOD_ROWS = SUBLANES
MOD_COL_TILE = 1536
VMEM_LIMIT = 56 * 1024 * 1024

F32 = jnp.float32
BF16 = jnp.bfloat16
HIGHEST = lax.Precision.HIGHEST
NEG_INF = float("-inf")
NT_DIMS = (((1,), (1,)), ((), ()))


def _cparams(*sem):
    return pltpu.CompilerParams(dimension_semantics=sem, vmem_limit_bytes=VMEM_LIMIT)


def _group_of_row(row0, seq):
    return jnp.where(row0 < seq, 0, jnp.where(row0 < 2 * seq, 1, 2))


def _silu(x):
    return x * jax.nn.sigmoid(x)


def _split_bf16(a):
    hi = a.astype(BF16)
    lo = (a - hi.astype(F32)).astype(BF16)
    return hi, lo


def _bf16_pair_word(x):
    hi = lax.bitcast_convert_type(x.astype(BF16).astype(F32), jnp.uint32)
    return hi | (hi >> 16)


def _pair_word_rows(row):
    tile = pltpu.bitcast(jnp.broadcast_to(row, (SUBLANES, row.shape[1])), BF16)
    return jnp.concatenate([tile] * (N_KEYS // tile.shape[0]), axis=0)


def _dot_split(a, b):
    ah, al = _split_bf16(a)
    bh, bl = _split_bf16(b)
    a4 = jnp.concatenate([ah, al, ah, al], axis=1)
    b4 = jnp.concatenate([bh, bh, bl, bl], axis=0)
    return jnp.dot(a4, b4, preferred_element_type=F32)


def _mod_kernel(c_ref, w_ref, b_ref, o_ref):
    o_ref[0] = jnp.dot(_silu(c_ref[...]), w_ref[0], preferred_element_type=F32) + b_ref[0]


def _modulation(cvec, w_mod, b_mod):
    tn = MOD_COL_TILE
    return pl.pallas_call(
        _mod_kernel,
        grid=(DEPTH, 6 * D_MODEL // tn),
        in_specs=[
            pl.BlockSpec((MOD_ROWS, D_MODEL), lambda l, j: (0, 0)),
            pl.BlockSpec((1, D_MODEL, tn), lambda l, j: (l, 0, j)),
            pl.BlockSpec((1, 1, tn), lambda l, j: (l, 0, j)),
        ],
        out_specs=pl.BlockSpec((1, MOD_ROWS, tn), lambda l, j: (l, 0, j)),
        out_shape=jax.ShapeDtypeStruct((DEPTH, MOD_ROWS, 6 * D_MODEL), F32),
        compiler_params=_cparams("arbitrary", "arbitrary"),
        name="modulation",
    )(cvec, w_mod, b_mod.reshape(DEPTH, 1, 6 * D_MODEL))


def _norm_mod(x, nw, mod, k):
    ms = jnp.mean(x * x, axis=-1, keepdims=True)
    xn = x * lax.rsqrt(ms + EPS) * nw
    sh = mod[:, k * D_MODEL:(k + 1) * D_MODEL]
    sc = mod[:, (k + 1) * D_MODEL:(k + 2) * D_MODEL]
    return xn * (1 + sc) + sh


def _inproj_kernel(x_ref, modv_ref, nw_ref, w_ref, qkv_ref, z_ref, pin_ref, gd_ref, gp_ref, ba_ref, *, seq):
    g = _group_of_row(pl.program_id(0) * INPROJ_TILE, seq)
    mod = modv_ref[pl.ds(g, 1), :]
    h = _norm_mod(x_ref[...], nw_ref[...], mod, 0)
    y = jnp.dot(h.astype(BF16), w_ref[...], preferred_element_type=F32)
    o = 0
    for ref in (qkv_ref, z_ref, pin_ref, gd_ref, gp_ref, ba_ref):
        n = ref.shape[1]
        ref[...] = y[:, o:o + n]
        o += n


def _inproj(xall, modv, norm_w, w_in_r, seq):
    rows = xall.shape[0]
    widths = (3 * DN_WIDTH, DN_WIDTH, POOL_WIDTH, D_MODEL, D_MODEL, BA_PAD)
    return pl.pallas_call(
        functools.partial(_inproj_kernel, seq=seq),
        grid=(rows // INPROJ_TILE,),
        in_specs=[
            pl.BlockSpec((INPROJ_TILE, D_MODEL), lambda i: (i, 0)),
            pl.BlockSpec((MOD_ROWS, 6 * D_MODEL), lambda i: (0, 0)),
            pl.BlockSpec((1, D_MODEL), lambda i: (0, 0)),
            pl.BlockSpec((D_MODEL, IN_COLS_R), lambda i: (0, 0)),
        ],
        out_specs=[pl.BlockSpec((INPROJ_TILE, n), lambda i: (i, 0)) for n in widths],
        out_shape=[jax.ShapeDtypeStruct((rows, n), F32) for n in widths],
        compiler_params=_cparams("arbitrary"),
        name="inproj",
    )(xall, modv, norm_w.reshape(1, D_MODEL), w_in_r)


def _dn_inputs(cur, prev, nxt, cw, ba, alog, dtb):
    tile = cur.shape[0]
    ext = jnp.concatenate([prev, cur, nxt], axis=0)
    left = SHORT_CONV // 2
    y = None
    for j in range(SHORT_CONV):
        o = HALO - left + j
        term = ext[o:o + tile, :] * cw[j:j + 1, :]
        y = term if y is None else y + term
    y = _silu(y)
    qs, ks = [], []
    for hh in range(DN_HEADS):
        lo, hi = hh * DN_HEAD_DIM, (hh + 1) * DN_HEAD_DIM
        qh = y[:, lo:hi]
        kh = y[:, DN_WIDTH + lo:DN_WIDTH + hi]
        qs.append(qh * lax.rsqrt(jnp.sum(qh * qh, axis=-1, keepdims=True) + EPS) * (DN_HEAD_DIM ** -0.5))
        ks.append(kh * lax.rsqrt(jnp.sum(kh * kh, axis=-1, keepdims=True) + EPS))
    beta = jax.nn.sigmoid(ba)
    xs = ba + dtb
    softplus = jnp.maximum(xs, 0.0) + jnp.log(1.0 + jnp.exp(-jnp.abs(xs)))
    gdec = -jnp.exp(alog) * softplus
    col = lax.broadcasted_iota(jnp.int32, ba.shape, 1)
    bg = jnp.where(col < 2 * DN_HEADS, beta, jnp.where(col < 4 * DN_HEADS, gdec, 0.0))
    return qs, ks, y[:, 2 * DN_WIDTH:], bg


def _dnprep_kernel(cur_ref, prev_ref, next_ref, cw_ref, ba_ref, alog_ref, dtb_ref, wq_ref, u_ref, lk_ref, eg_ref,
                   *, seq, ctx):
    c = DN_CHUNK
    tile = DN_PREP_CHUNKS * c
    row0 = pl.program_id(0) * tile
    nlat = BATCH * seq
    is_start = (row0 == 0) | (row0 == seq) | (row0 == nlat) | (row0 == nlat + ctx)
    row1 = row0 + tile
    is_end = (row1 == seq) | (row1 == nlat) | (row1 == nlat + ctx) | (row1 == nlat + BATCH * ctx)
    qs, ks, v_all, bg_all = _dn_inputs(cur_ref[...], jnp.where(is_start, 0.0, prev_ref[...]),
                                       jnp.where(is_end, 0.0, next_ref[...]), cw_ref[...], ba_ref[...],
                                       alog_ref[...], dtb_ref[...])
    ri = lax.broadcasted_iota(jnp.int32, (c, c), 0)
    ci = lax.broadcasted_iota(jnp.int32, (c, c), 1)
    dirs = ((ri >= ci, ri > ci, ci >= ri, c - 1), (ri <= ci, ri < ci, ci <= ri, 0))
    chains = []
    for j in range(DN_PREP_CHUNKS):
        rows = slice(j * c, (j + 1) * c)
        bg = bg_all[rows, :]
        bgh, bgl = _split_bf16(bg)
        bgth, bgtl = _split_bf16(bg.T[0:4 * DN_HEADS, :])
        bg2 = jnp.concatenate([bgh, bgl], axis=0)
        bgt2 = jnp.concatenate([bgth, bgtl], axis=1)
        egs = []
        for d, (incl, strict, incl_t, last) in enumerate(dirs):
            m = incl.astype(BF16)
            mt = incl_t.astype(BF16)
            gc_all = jnp.dot(jnp.concatenate([m, m], axis=1), bg2, preferred_element_type=F32)
            gr_all = jnp.dot(bgt2, jnp.concatenate([mt, mt], axis=0), preferred_element_type=F32)
            for hh in range(DN_HEADS):
                lo, hi = hh * DN_HEAD_DIM, (hh + 1) * DN_HEAD_DIM
                ch = d * DN_HEADS + hh
                gcol = 2 * DN_HEADS + ch
                q = qs[hh][rows, :]
                k = ks[hh][rows, :]
                v = v_all[rows, lo:hi]
                beta = bg[:, ch:ch + 1]
                gc = gc_all[:, gcol:gcol + 1]
                gr = gr_all[gcol:gcol + 1, :]
                glast = gc_all[last:last + 1, gcol:gcol + 1]
                decay = jnp.where(incl, jnp.exp(jnp.minimum(gc - gr, 0.0)), 0.0)
                kk = lax.dot_general(k, k, NT_DIMS, preferred_element_type=F32)
                qk = lax.dot_general(q, k, NT_DIMS, preferred_element_type=F32)
                egc = jnp.exp(gc)
                cols = slice(ch * DN_HEAD_DIM, (ch + 1) * DN_HEAD_DIM)
                wq_ref[0, j, c:2 * c, cols] = (q * egc).astype(BF16)
                lk_ref[0, j, ch, 0:c, :] = jnp.where(incl, qk * decay, 0.0).astype(BF16)
                lk_ref[0, j, ch, c:, :] = (k * jnp.exp(glast - gc)).T.astype(BF16)
                egs.append(jnp.broadcast_to(jnp.exp(glast), (1, LANES)))
                chains.append((j, cols, jnp.where(strict, -(beta * kk * decay), 0.0),
                               jnp.concatenate([k * (beta * egc), v * beta], axis=1)))
        eg_ref[0, j] = jnp.concatenate(egs, axis=0)
    for g0 in range(0, len(chains), DN_LOCKSTEP):
        group = chains[g0:g0 + DN_LOCKSTEP]
        ps = [ch[2] for ch in group]
        eye = (ri == ci).astype(F32)
        xs = [eye + p for p in ps]
        for _ in range(int(np.log2(c)) - 1):
            ps = [_dot_split(p, p) for p in ps]
            xs = [x + _dot_split(x, p) for x, p in zip(xs, ps)]
        ys = [_dot_split(x, ch[3]) for x, ch in zip(xs, group)]
        for (j, cols, _, _), y in zip(group, ys):
            wq_ref[0, j, 0:c, cols] = y[:, :DN_HEAD_DIM].astype(BF16)
            u_ref[0, j, :, cols] = y[:, DN_HEAD_DIM:]


def _dnprep(qkv, ba, conv_w, a_log, dt_bias, seq, ctx):
    rows = qkv.shape[0]
    c = DN_CHUNK
    n = DN_PREP_CHUNKS
    tile = n * c
    nct, nlt = ctx // c, seq // c
    assert nct % n == 0 and nlt % n == 0
    nch = nct + nlt
    nh = tile // HALO
    last = rows // HALO - 1
    pad = jnp.zeros((2 * DN_HEADS,), F32)
    tail = jnp.zeros((BA_PAD - 4 * DN_HEADS,), F32)
    alog = jnp.concatenate([pad, a_log.reshape(-1), tail]).reshape(1, BA_PAD)
    dtb = jnp.concatenate([pad, dt_bias.reshape(-1), tail]).reshape(1, BA_PAD)
    w3 = 3 * DN_WIDTH

    def seq_pos(i):
        i = i * n
        lat = i < BATCH * nlt
        j = i - BATCH * nlt
        return jnp.where(lat, i // nlt, j // nct), jnp.where(lat, nct + i % nlt, j % nct) // n

    def out_spec(*tail_dims):
        return pl.BlockSpec((1, n) + tail_dims, lambda i: seq_pos(i) + (0,) * len(tail_dims))

    width = DN_CHAINS * DN_HEAD_DIM
    return pl.pallas_call(
        functools.partial(_dnprep_kernel, seq=seq, ctx=ctx),
        grid=(rows // tile,),
        in_specs=[pl.BlockSpec((tile, w3), lambda i: (i, 0)),
                  pl.BlockSpec((HALO, w3), lambda i: (jnp.maximum(i * nh - 1, 0), 0)),
                  pl.BlockSpec((HALO, w3), lambda i: (jnp.minimum((i + 1) * nh, last), 0)),
                  pl.BlockSpec((SHORT_CONV, w3), lambda i: (0, 0)),
                  pl.BlockSpec((tile, BA_PAD), lambda i: (i, 0)),
                  pl.BlockSpec((1, BA_PAD), lambda i: (0, 0)),
                  pl.BlockSpec((1, BA_PAD), lambda i: (0, 0))],
        out_specs=[out_spec(2 * c, width), out_spec(c, width), out_spec(DN_CHAINS, c + DN_HEAD_DIM, c),
                   out_spec(DN_CHAINS, LANES)],
        out_shape=[jax.ShapeDtypeStruct((BATCH, nch, 2 * c, width), BF16),
                   jax.ShapeDtypeStruct((BATCH, nch, c, width), F32),
                   jax.ShapeDtypeStruct((BATCH, nch, DN_CHAINS, c + DN_HEAD_DIM, c), BF16),
                   jax.ShapeDtypeStruct((BATCH, nch, DN_CHAINS, LANES), F32)],
        compiler_params=_cparams("arbitrary"),
        name="dnprep",
    )(qkv, qkv, qkv, conv_w, ba, alog, dtb)


def _dnscan_kernel(wqf_ref, uf_ref, lkf_ref, egf_ref, wqb_ref, ub_ref, lkb_ref, egb_ref, of_ref, ob_ref, s_ref):
    @pl.when(pl.program_id(0) == 0)
    def _():
        s_ref[...] = jnp.zeros_like(s_ref)

    c = DN_CHUNK
    dirs = ((wqf_ref, uf_ref, lkf_ref, egf_ref, of_ref), (wqb_ref, ub_ref, lkb_ref, egb_ref, ob_ref))
    chains = [(d, b, hh) for d in range(2) for b in range(BATCH) for hh in range(DN_HEADS)]
    sidx = lambda d, b, hh: (b * 2 + d) * DN_HEADS + hh
    cols = lambda hh: slice(hh * DN_HEAD_DIM, (hh + 1) * DN_HEAD_DIM)
    for j in range(DN_SCAN_CHUNKS):
        pos = (j, DN_SCAN_CHUNKS - 1 - j)
        ss = [s_ref[sidx(*ch)] for ch in chains]
        r1 = [jnp.dot(dirs[d][0][b, pos[d], :, cols(hh)], s.astype(BF16), preferred_element_type=F32)
              for (d, b, hh), s in zip(chains, ss)]
        vn = [dirs[d][1][b, pos[d], :, cols(hh)] - r[:c] for (d, b, hh), r in zip(chains, r1)]
        r2 = [jnp.dot(dirs[d][2][b, pos[d], hh], v.astype(BF16), preferred_element_type=F32)
              for (d, b, hh), v in zip(chains, vn)]
        for (d, b, hh), s, a1, a2 in zip(chains, ss, r1, r2):
            dirs[d][4][b, pos[d], :, cols(hh)] = a1[c:] + a2[:c]
            row = d * DN_HEADS + hh
            s_ref[sidx(d, b, hh)] = s * dirs[d][3][b, pos[d], row:row + 1, :] + a2[c:]


def _dnscan(wq, u, lk, eg, seq, ctx):
    c = DN_CHUNK
    n = DN_SCAN_CHUNKS
    nct, nlt = ctx // c, seq // c
    assert nct % n == 0 and nlt % n == 0
    nch = nct + nlt
    nblk, nctb = nch // n, nct // n

    def bwd(s):
        return jnp.where(s < nctb, nctb - 1 - s, nctb + nblk - 1 - s)

    def specs(pos, d):
        return [pl.BlockSpec((BATCH, n, 2 * c, DN_WIDTH), lambda s: (0, pos(s), 0, d)),
                pl.BlockSpec((BATCH, n, c, DN_WIDTH), lambda s: (0, pos(s), 0, d)),
                pl.BlockSpec((BATCH, n, DN_HEADS, c + DN_HEAD_DIM, c), lambda s: (0, pos(s), d, 0, 0)),
                pl.BlockSpec((BATCH, n, DN_CHAINS, LANES), lambda s: (0, pos(s), 0, 0))]

    fwd = lambda s: s
    return pl.pallas_call(
        _dnscan_kernel,
        grid=(nblk,),
        in_specs=specs(fwd, 0) + specs(bwd, 1),
        out_specs=[pl.BlockSpec((BATCH, n, c, DN_WIDTH), lambda s: (0, s, 0, 0)),
                   pl.BlockSpec((BATCH, n, c, DN_WIDTH), lambda s: (0, bwd(s), 0, 0))],
        out_shape=[jax.ShapeDtypeStruct((BATCH, nch, c, DN_WIDTH), F32)] * 2,
        scratch_shapes=[pltpu.VMEM((BATCH * DN_CHAINS, DN_HEAD_DIM, DN_HEAD_DIM), F32)],
        compiler_params=_cparams("arbitrary"),
        name="dnscan",
    )(wq, u, lk, eg, wq, u, lk, eg)


def _pool_matrices(tile, seg):
    t = np.arange(tile)
    p = t % seg
    mats = []
    for win in POOL_WINDOWS:
        lo = np.clip(p - win // 2, 0, seg)
        hi = np.clip(p + win - win // 2, 0, seg)
        same = (t[:, None] // seg) == (t[None, :] // seg)
        inside = same & (p[None, :] >= lo[:, None]) & (p[None, :] < hi[:, None])
        mats.append(inside / (hi - lo)[:, None].astype(np.float64) - np.eye(tile))
    return np.stack(mats).astype(np.float32)


def _merge_kernel(x_ref, of_ref, ob_ref, z_ref, pin_ref, gd_ref, gp_ref, modv_ref, on_ref, pm_ref, pw_ref, ps_ref,
                  wud_ref, wup_ref, wo_ref, o_ref, *, seq):
    g = _group_of_row(pl.program_id(0) * ROW_TILE, seq)
    mod = modv_ref[pl.ds(g, 1), :]
    gate = mod[:, 2 * D_MODEL:3 * D_MODEL]
    o = (of_ref[0] + ob_ref[0]).reshape(ROW_TILE, DN_WIDTH)
    z = z_ref[...]
    onw = on_ref[...]
    ys = []
    for hh in range(DN_HEADS):
        lo, hi = hh * DN_HEAD_DIM, (hh + 1) * DN_HEAD_DIM
        oh = o[:, lo:hi]
        ys.append(oh * lax.rsqrt(jnp.mean(oh * oh, axis=-1, keepdims=True) + EPS) * onw * _silu(z[:, lo:hi]))
    y_dn = jnp.concatenate(ys, axis=1)
    pin = pin_ref[...]
    yp = []
    for gi in range(POOL_GROUPS):
        lo, hi = gi * POOL_GW, (gi + 1) * POOL_GW
        pooled = jnp.dot(pm_ref[0, gi], pin[:, lo:hi], precision=HIGHEST, preferred_element_type=F32)
        yp.append(jnp.dot(pooled, pw_ref[gi], preferred_element_type=F32))
    y_pool = jnp.concatenate(yp, axis=1) * ps_ref[...]
    m = (jax.nn.sigmoid(gd_ref[...]) * jnp.dot(y_dn.astype(BF16), wud_ref[...], preferred_element_type=F32)
         + jax.nn.sigmoid(gp_ref[...]) * jnp.dot(y_pool.astype(BF16), wup_ref[...], preferred_element_type=F32))
    out = jnp.dot(m.astype(BF16), wo_ref[...], preferred_element_type=F32)
    o_ref[...] = x_ref[...] + gate * out


def _merge(xall, o_f, o_b, z, pin, gd, gp, modv, onorm, pool_w, pool_scale, wud, wup, wo, seq, ctx, rows_out):
    assert ctx == ROW_TILE and seq % ROW_TILE == 0
    lat_tiles = BATCH * seq // ROW_TILE
    tiles_per_seq = seq // ROW_TILE
    cpt = ROW_TILE // DN_CHUNK
    pm = jnp.asarray(np.stack([_pool_matrices(ROW_TILE, GRID_W), _pool_matrices(ROW_TILE, ctx)]))
    row = lambda n: pl.BlockSpec((ROW_TILE, n), lambda i: (i, 0))
    full = lambda *s: pl.BlockSpec(s, lambda i: (0,) * len(s))

    def o_pos(i):
        lat = i < lat_tiles
        return (jnp.where(lat, i // tiles_per_seq, i - lat_tiles), jnp.where(lat, 1 + i % tiles_per_seq, 0), 0, 0)

    o_spec = pl.BlockSpec((1, cpt, DN_CHUNK, DN_WIDTH), o_pos)
    return pl.pallas_call(
        functools.partial(_merge_kernel, seq=seq),
        grid=(rows_out // ROW_TILE,),
        in_specs=[
            row(D_MODEL), o_spec, o_spec, row(DN_WIDTH), row(POOL_WIDTH), row(D_MODEL), row(D_MODEL),
            full(MOD_ROWS, 6 * D_MODEL), full(1, DN_HEAD_DIM),
            pl.BlockSpec((1, POOL_GROUPS, ROW_TILE, ROW_TILE), lambda i: (jnp.where(i >= lat_tiles, 1, 0), 0, 0, 0)),
            full(POOL_GROUPS, POOL_GW, POOL_GW), full(1, POOL_WIDTH),
            full(DN_WIDTH, D_MODEL), full(POOL_WIDTH, D_MODEL), full(D_MODEL, D_MODEL),
        ],
        out_specs=row(D_MODEL),
        out_shape=jax.ShapeDtypeStruct((rows_out, D_MODEL), F32),
        compiler_params=_cparams("arbitrary"),
        name="merge",
    )(xall, o_f, o_b, z, pin, gd, gp, modv, onorm.reshape(1, DN_HEAD_DIM), pm, pool_w,
      pool_scale.reshape(1, POOL_WIDTH), wud, wup, wo)


def _col_max(x):
    return jnp.max(x, axis=0, keepdims=True)


def _col_min(x):
    return jnp.min(x, axis=0, keepdims=True)


def _oddeven_merge(lo, hi, r):
    step = r * 2
    if step < hi - lo:
        yield from _oddeven_merge(lo, hi, step)
        yield from _oddeven_merge(lo + r, hi, step)
        yield from [(i, i + r) for i in range(lo + r, hi - r, step)]
    else:
        yield (lo, lo + r)


def _oddeven_merge_sort(lo, hi):
    if hi - lo >= 1:
        mid = lo + (hi - lo) // 2
        yield from _oddeven_merge_sort(lo, mid)
        yield from _oddeven_merge_sort(mid + 1, hi)
        yield from _oddeven_merge(lo, hi, 1)


def _exchange(v, i, j):
    v[i], v[j] = jnp.maximum(v[i], v[j]), jnp.minimum(v[i], v[j])


def _sorted_top_many(block_lists):
    k = PEER_TOPK
    vs = [list(blocks) for blocks in block_lists]
    for i, j in _oddeven_merge_sort(0, k - 1):
        for v in vs:
            _exchange(v, i, j)
    shift = SUBLANES // 2
    while shift >= 1:
        ws = [[pltpu.roll(x, shift, 0) for x in v] for v in vs]
        vs = [[jnp.maximum(v[j], w[k - 1 - j]) for j in range(k)] for v, w in zip(vs, ws)]
        stride = k // 2
        while stride >= 1:
            for i in range(k):
                if i & stride == 0:
                    for v in vs:
                        _exchange(v, i, i + stride)
            stride //= 2
        shift //= 2
    return vs


def _sublane_total(x):
    shift = SUBLANES // 2
    while shift >= 1:
        x = x + pltpu.roll(x, shift, 0)
        shift //= 2
    return x


def _unambiguous(blocks, top):
    ok = top[0] > top[1]
    for a in range(1, PEER_TOPK - 1):
        ok = ok & (top[a] > top[a + 1])
    ge = None
    for blk in blocks:
        one = jnp.where(blk >= top[PEER_TOPK - 1], 1.0, 0.0)
        ge = one if ge is None else ge + one
    return ok & (_sublane_total(ge) == float(PEER_TOPK))


def _top_select(scores, iota):
    n = float(scores[0].shape[0])
    cur = list(scores)
    idxs = [[] for _ in cur]
    for _ in range(PEER_TOPK):
        for j in range(len(cur)):
            m = _col_max(cur[j])
            cand = jnp.where(cur[j] == m, iota, n)
            idx = _col_min(cand)
            cur[j] = jnp.where(cand == idx, NEG_INF, cur[j])
            idxs[j].append(idx)
    return [jnp.concatenate(i, axis=0) for i in idxs]


def _peer_route_kernel(x_ref, modv_ref, nw_ref, wq_ref, keys_ref, hn_ref, n1_ref, e1_ref, r2_ref, e2_ref,
                       q_scr, *, seq):
    g = _group_of_row(pl.program_id(0) * ROW_TILE, seq)
    mod = modv_ref[pl.ds(g, 1), :]
    hn = _norm_mod(x_ref[...], nw_ref[...], mod, 3).astype(BF16)
    hn_ref[...] = hn
    q = jnp.dot(hn, wq_ref[...], preferred_element_type=F32)
    nsub = ROW_TILE // LANES
    for j in range(2 * PEER_HEADS):
        for t in range(nsub):
            q_scr[j, t] = q[t * LANES:(t + 1) * LANES, j * PEER_HALF:(j + 1) * PEER_HALF]
    k = PEER_TOPK
    nblk = N_KEYS // SUBLANES
    iota_k = lax.broadcasted_iota(jnp.int32, (N_KEYS, LANES), 0).astype(F32)
    iota_a = lax.broadcasted_iota(jnp.int32, (k, LANES), 0).astype(F32)
    iota_s = lax.broadcasted_iota(jnp.int32, (SUBLANES, LANES), 0).astype(F32)
    split = lambda s: [s[SUBLANES * j:SUBLANES * (j + 1), :] for j in range(nblk)]
    rep = lambda row: jnp.broadcast_to(row, (SUBLANES, LANES))

    def head(hh, carry):
        subs = range(nsub)
        s1 = [lax.dot_general(keys_ref[2 * hh], q_scr[2 * hh, t], NT_DIMS, preferred_element_type=F32) for t in subs]
        s2 = [lax.dot_general(keys_ref[2 * hh + 1], q_scr[2 * hh + 1, t], NT_DIMS, preferred_element_type=F32)
              for t in subs]
        b1, b2 = [split(x) for x in s1], [split(x) for x in s2]
        tops = _sorted_top_many(b1 + b2)
        top1, top2 = tops[:nsub], tops[nsub:]
        c = [jnp.concatenate([x[0:1, :] for x in top], axis=0) for top in top1]
        d = [jnp.concatenate([x[0:1, :] for x in top], axis=0) for top in top2]

        def grid_cells(c, d):
            cells = [c[0:1, :] + d[0:SUBLANES, :], c[0:1, :] + d[SUBLANES:k, :]]
            for a in range(1, SUBLANES):
                cells.append(jnp.where(iota_s < float(k // (a + 1)), c[a:a + 1, :] + d[0:SUBLANES, :], NEG_INF))
            cells.append(c[SUBLANES:k, :] + d[0:1, :])
            return cells + [jnp.full((SUBLANES, LANES), NEG_INF, F32)] * (k - len(cells))

        cuts = [x[k - 1][0:1, :] for x in _sorted_top_many([grid_cells(c[t], d[t]) for t in subs])]
        cnt_cut = [jnp.zeros((k, LANES), F32) for _ in subs]
        for b in range(k):
            cnt_cut = [acc + jnp.where(c[t] + d[t][b:b + 1, :] >= cuts[t], 1.0, 0.0) for t, acc in enumerate(cnt_cut)]

        def by_merge():
            cnts = [jnp.zeros((k, LANES), F32) for _ in subs]
            for _ in range(k):
                for t in subs:
                    dn = jnp.full((k, LANES), NEG_INF, F32)
                    for b in range(k):
                        dn = jnp.where(cnts[t] == float(b), d[t][b:b + 1, :], dn)
                    f = c[t] + dn
                    cand = jnp.where(f == _col_max(f), iota_a, float(k))
                    cnts[t] = cnts[t] + (cand == _col_min(cand)).astype(F32)
            return tuple(cnts)

        exact_cut = jnp.all(jnp.concatenate([jnp.sum(x, axis=0, keepdims=True) for x in cnt_cut], axis=0) == float(k))
        cnt = lax.cond(exact_cut, lambda: tuple(cnt_cut), by_merge)
        zsum = []
        for t in subs:
            e1c = jnp.exp(c[t] - c[t][0:1, :])
            e2d = jnp.exp(d[t] - d[t][0:1, :])
            part = jnp.zeros((k, LANES), F32)
            for b in range(k):
                part = part + jnp.where(cnt[t] > float(b), e2d[b:b + 1, :], 0.0)
            zsum.append(jnp.sum(e1c * part, axis=0, keepdims=True))

        def by_value():
            n1b = [[jnp.zeros((SUBLANES, LANES), F32)] * nblk for _ in subs]
            r2b = [[jnp.full((SUBLANES, LANES), float(k), F32)] * nblk for _ in subs]
            for a in range(k):
                for t in subs:
                    cnt_a = rep(cnt[t][a:a + 1, :])
                    n1b[t] = [jnp.where(blk == top1[t][a], cnt_a, acc) for blk, acc in zip(b1[t], n1b[t])]
                    r2b[t] = [jnp.where(blk == top2[t][a], float(a), acc) for blk, acc in zip(b2[t], r2b[t])]
            return tuple(jnp.concatenate(x, axis=0) for x in n1b + r2b)

        def by_index():
            idx = _top_select(s1 + s2, iota_k)
            out1, out2 = [], []
            for t in subs:
                n1 = jnp.zeros((N_KEYS, LANES), F32)
                r2 = jnp.full((N_KEYS, LANES), float(k), F32)
                for a in range(k):
                    n1 = jnp.where(iota_k == idx[t][a:a + 1, :], cnt[t][a:a + 1, :], n1)
                    r2 = jnp.where(iota_k == idx[nsub + t][a:a + 1, :], float(a), r2)
                out1.append(n1)
                out2.append(r2)
            return tuple(out1 + out2)

        clean = None
        for t in subs:
            ok = _unambiguous(b1[t], top1[t]) & _unambiguous(b2[t], top2[t])
            clean = ok if clean is None else clean & ok
        ranks = lax.cond(jnp.all(clean), by_value, by_index)
        for t in subs:
            lanes = slice(t * LANES, (t + 1) * LANES)
            n1_ref[hh, 0, :, lanes] = _bf16_pair_word(ranks[t])
            e1_ref[hh, 0, :, lanes] = _bf16_pair_word(0.5 * jnp.exp(s1[t] - c[t][0:1, :]) / zsum[t])
            r2_ref[hh, 0, :, lanes] = ranks[nsub + t].astype(BF16)
            e2_ref[hh, 0, :, lanes] = jnp.exp(s2[t] - d[t][0:1, :]).astype(BF16)
        return carry

    lax.fori_loop(0, PEER_HEADS, head, 0)


def _peer_route(xall, modv, norm_w, wq, keys, seq, rows_out):
    row = lambda n: pl.BlockSpec((ROW_TILE, n), lambda i: (i, 0))
    full = lambda *s: pl.BlockSpec(s, lambda i: (0,) * len(s))
    tab = pl.BlockSpec((PEER_HEADS, 1, N_KEYS, ROW_TILE), lambda i: (0, i, 0, 0))
    tab_shape = lambda dt: jax.ShapeDtypeStruct((PEER_HEADS, rows_out // ROW_TILE, N_KEYS, ROW_TILE), dt)
    return pl.pallas_call(
        functools.partial(_peer_route_kernel, seq=seq),
        grid=(rows_out // ROW_TILE,),
        in_specs=[row(D_MODEL), full(MOD_ROWS, 6 * D_MODEL), full(1, D_MODEL), full(D_MODEL, PEER_HEADS * PEER_QDIM),
                  full(2 * PEER_HEADS, N_KEYS, PEER_HALF)],
        out_specs=[row(D_MODEL), tab, tab, tab, tab],
        out_shape=[jax.ShapeDtypeStruct((rows_out, D_MODEL), BF16), tab_shape(jnp.uint32), tab_shape(jnp.uint32),
                   tab_shape(BF16), tab_shape(BF16)],
        scratch_shapes=[pltpu.VMEM((2 * PEER_HEADS, ROW_TILE // LANES, LANES, PEER_HALF), F32)],
        compiler_params=_cparams("arbitrary"),
        name="peer_route",
    )(xall, modv, norm_w.reshape(1, D_MODEL), wq, keys.reshape(2 * PEER_HEADS, N_KEYS, PEER_HALF))


PEER_PIPE_LAG = 2
PEER_MXU_BLOCK = 256


def _peer_tile(n, lag, n_tiles, n_exp):
    m = jnp.clip(n - lag, 0, n_tiles - 1)
    return m // n_exp, m % n_exp


def _peer_dense_kernel(x_ref, modv_ref, hn_ref, u_ref, vt_ref, n1_ref, e1_ref, r2_ref, e2_ref, fn_ref, o_ref,
                       acc_ref, a0_ref, a1_ref, act0_ref, act1_ref, hn_scr, *, seq, final, n_tiles, n_exp):
    n = pl.program_id(0)
    t_c, e_c = _peer_tile(n, PEER_PIPE_LAG, n_tiles, n_exp)

    @pl.when(n == 0)
    def _():
        for ref in (acc_ref, a0_ref, a1_ref, act0_ref, act1_ref):
            ref[...] = jnp.zeros_like(ref)

    @pl.when(_peer_tile(n, 0, n_tiles, n_exp)[1] == 0)
    def _():
        hn_scr[...] = hn_ref[...]

    def step(a_next, a_cur, act_next, act_cur):
        zero = jnp.zeros((), BF16)
        blk = PEER_MXU_BLOCK

        def stage_a(rb, cb):
            rows, cols = slice(rb * blk, (rb + 1) * blk), slice(cb * blk, (cb + 1) * blk)
            a_next[rows, cols] = lax.dot_general(u_ref[rows, :], hn_scr[cols, :], NT_DIMS,
                                                 preferred_element_type=F32)

        def stage_b(t, i):
            lanes = slice(t * ROW_TILE, (t + 1) * ROW_TILE)
            wsum = None
            for hh in range(PEER_HEADS):
                keep = r2_ref[hh, t] < _pair_word_rows(n1_ref[hh, t, i:i + 1, :])
                term = jnp.where(keep, e2_ref[hh, t], zero) * _pair_word_rows(e1_ref[hh, t, i:i + 1, :])
                wsum = term if wsum is None else wsum + term
            rows = slice(i * N_KEYS, (i + 1) * N_KEYS)
            a = a_cur[rows, lanes]
            gelu = a * (1.0 + lax.erf(a * (2.0 ** -0.5)))
            act_next[rows, lanes] = gelu.astype(BF16) * wsum

        def stage_c(rb, cb):
            rows, cols = slice(rb * blk, (rb + 1) * blk), slice(cb * blk, (cb + 1) * blk)
            prev = jnp.where(e_c == 0, 0.0, acc_ref[rows, cols])
            acc_ref[rows, cols] = prev + jnp.dot(vt_ref[rows, :], act_cur[:, cols], preferred_element_type=F32)

        a_pieces = [(rb, cb) for cb in range(PEER_TOK_TILE // blk) for rb in range(PEER_EXP_TILE // blk)]
        c_pieces = [(rb, cb) for cb in range(PEER_TOK_TILE // blk) for rb in range(D_MODEL // blk)]
        b_pieces = [(t, i) for t in range(PEER_TOK_TILE // ROW_TILE) for i in range(PEER_EXP_TILE // N_KEYS)]
        a_per_c = len(a_pieces) // len(c_pieces)
        b_per_c = len(b_pieces) // len(c_pieces)
        assert a_per_c * len(c_pieces) == len(a_pieces) and b_per_c * len(c_pieces) == len(b_pieces)
        b_iter = iter(b_pieces)
        for k, c_piece in enumerate(c_pieces):
            mxu = [(stage_a, p) for p in a_pieces[k * a_per_c:(k + 1) * a_per_c]]
            mxu.insert(1, (stage_c, c_piece))
            quota = [b_per_c // len(mxu) + (1 if m < b_per_c % len(mxu) else 0) for m in range(len(mxu))]
            for (fn, piece), nb in zip(mxu, quota):
                fn(*piece)
                for _ in range(nb):
                    stage_b(*next(b_iter))

    step(a0_ref, a1_ref, act0_ref, act1_ref)
    a1_ref[...] = a0_ref[...]

    @pl.when(n >= 0)
    def _():
        act1_ref[...] = act0_ref[...]

    @pl.when((e_c == n_exp - 1) & (n >= PEER_PIPE_LAG))
    def _():
        g = _group_of_row(t_c * PEER_TOK_TILE, seq)
        mod = modv_ref[pl.ds(g, 1), :]
        y = x_ref[...] + mod[:, 5 * D_MODEL:6 * D_MODEL] * acc_ref[...].T
        if final:
            y = y * lax.rsqrt(jnp.mean(y * y, axis=-1, keepdims=True) + EPS) * fn_ref[...]
        o_ref[...] = y


def _peer_dense(xall, modv, hn, u_bf, vt_bf, n1, e1, r2, e2, final_norm, seq, rows_out, final):
    tt, et = PEER_TOK_TILE, PEER_EXP_TILE
    n_first = et // N_KEYS
    n_exp = N_EXPERTS // et
    n_tiles = (rows_out // tt) * n_exp
    tile = lambda lag: (lambda n: _peer_tile(n, lag, n_tiles, n_exp))
    ta, tb, tc = tile(0), tile(1), tile(PEER_PIPE_LAG)
    full = lambda *s: pl.BlockSpec(s, lambda n: (0,) * len(s))
    per_first = pl.BlockSpec((PEER_HEADS, tt // ROW_TILE, n_first, ROW_TILE), lambda n: (0, tb(n)[0], tb(n)[1], 0))
    per_second = pl.BlockSpec((PEER_HEADS, tt // ROW_TILE, N_KEYS, ROW_TILE), lambda n: (0, tb(n)[0], 0, 0))
    return pl.pallas_call(
        functools.partial(_peer_dense_kernel, seq=seq, final=final, n_tiles=n_tiles, n_exp=n_exp),
        grid=(n_tiles + PEER_PIPE_LAG,),
        in_specs=[pl.BlockSpec((tt, D_MODEL), lambda n: (tc(n)[0], 0)),
                  full(MOD_ROWS, 6 * D_MODEL),
                  pl.BlockSpec((tt, D_MODEL), lambda n: (ta(n)[0], 0)),
                  pl.BlockSpec((et, D_MODEL), lambda n: (ta(n)[1], 0)),
                  pl.BlockSpec((D_MODEL, et), lambda n: (0, tc(n)[1])),
                  per_first, per_first, per_second, per_second, full(1, D_MODEL)],
        out_specs=pl.BlockSpec((tt, D_MODEL), lambda n: (tc(n)[0], 0)),
        out_shape=jax.ShapeDtypeStruct((rows_out, D_MODEL), F32),
        scratch_shapes=[pltpu.VMEM((D_MODEL, tt), F32), pltpu.VMEM((et, tt), F32), pltpu.VMEM((et, tt), F32),
                        pltpu.VMEM((et, tt), BF16), pltpu.VMEM((et, tt), BF16), pltpu.VMEM((tt, D_MODEL), BF16)],
        compiler_params=_cparams("arbitrary"),
        name="peer_dense",
    )(xall, modv, hn, u_bf, vt_bf, n1, e1, r2, e2, final_norm.reshape(1, D_MODEL))


def _reorder_in_weight(w):
    s = np.cumsum((0, 3 * DN_WIDTH, DN_WIDTH, POOL_WIDTH, 2 * DN_HEADS, 2 * DN_HEADS, D_MODEL, D_MODEL))
    qkv, z, pin, b, a, gd, gp = (w[:, s[i]:s[i + 1]] for i in range(7))
    pad = jnp.zeros((w.shape[0], BA_PAD - 4 * DN_HEADS), w.dtype)
    return jnp.concatenate([qkv, z, pin, gd, gp, b, a, pad], axis=1).astype(BF16)


def _forward(x, c, ctx, c_ctx, w_mod, b_mod, norm_mix, w_in, conv_w, a_log, dt_bias, dn_out_norm, pool_w, pool_scale,
             w_up_dn, w_up_pool, w_out, norm_ffn, peer_wq, peer_keys, peer_u, peer_v, final_norm):
    seq, nctx = x.shape[1], ctx.shape[1]
    nlat = BATCH * seq
    xall = jnp.concatenate([x.reshape(nlat, D_MODEL), ctx.reshape(BATCH * nctx, D_MODEL)], axis=0)
    rows = xall.shape[0]
    cvec = jnp.concatenate([c, c_ctx[None, :], jnp.zeros((MOD_ROWS - BATCH - 1, D_MODEL), F32)], axis=0)
    modv_all = _modulation(cvec, w_mod, b_mod)
    for i in range(DEPTH):
        last = i == DEPTH - 1
        rows_out = nlat if last else rows
        modv = modv_all[i]
        qkv, z, pin, gd, gp, ba = _inproj(xall, modv, norm_mix[i], _reorder_in_weight(w_in[i]), seq)
        o_f, o_b = _dnscan(*_dnprep(qkv, ba, conv_w[i], a_log[i], dt_bias[i], seq, nctx), seq, nctx)
        xall = _merge(xall, o_f, o_b, z, pin, gd, gp, modv, dn_out_norm[i], pool_w[i], pool_scale[i],
                      w_up_dn[i].astype(BF16), w_up_pool[i].astype(BF16), w_out[i].astype(BF16), seq, nctx, rows_out)
        hn, n1, e1, r2, e2 = _peer_route(xall, modv, norm_ffn[i], peer_wq[i].astype(BF16), peer_keys[i], seq, rows_out)
        xall = _peer_dense(xall, modv, hn, peer_u[i].astype(BF16), peer_v[i].T.astype(BF16), n1, e1, r2, e2,
                           final_norm, seq, rows_out, last)
    return xall.reshape(BATCH, seq, D_MODEL)


def kernel(x, c, ctx, c_ctx, w_mod, b_mod, norm_mix, w_in, conv_w, a_log, dt_bias, dn_out_norm, pool_w, pool_scale, w_up_dn, w_up_pool, w_out, norm_ffn, peer_wq, peer_keys, peer_u, peer_v, final_norm):
    return _forward(x, c, ctx, c_ctx, w_mod, b_mod, norm_mix, w_in, conv_w, a_log, dt_bias, dn_out_norm, pool_w,
                    pool_scale, w_up_dn, w_up_pool, w_out, norm_ffn, peer_wq, peer_keys, peer_u, peer_v, final_norm)
```

```python
import functools

import numpy as np
import jax
import jax.numpy as jnp
from jax import lax
from jax.experimental import pallas as pl
from jax.experimental.pallas import tpu as pltpu

D_MODEL = 1024
BATCH = 2
DEPTH = 2
GRID_W = 64
EPS = 1e-6

DN_HEADS = 4
DN_HEAD_DIM = 128
DN_WIDTH = DN_HEADS * DN_HEAD_DIM
SHORT_CONV = 4
DN_CHUNK = 64
DN_CHAINS = 2 * DN_HEADS
DN_PREP_CHUNKS = 2
DN_SCAN_CHUNKS = 4
DN_LOCKSTEP = 16

POOL_WINDOWS = (2, 4, 8, 16)
POOL_GROUPS = 4
POOL_WIDTH = D_MODEL // 2
POOL_GW = POOL_WIDTH // POOL_GROUPS

PEER_HEADS = 8
N_KEYS = 128
N_EXPERTS = N_KEYS * N_KEYS
PEER_TOPK = 16
PEER_QDIM = 256
PEER_HALF = PEER_QDIM // 2

BA_PAD = 128
IN_COLS_R = 3 * DN_WIDTH + DN_WIDTH + POOL_WIDTH + 2 * D_MODEL + BA_PAD

LANES = 128
SUBLANES = 8
ROW_TILE = 256
INPROJ_TILE = 512
PEER_TOK_TILE = 512
PEER_EXP_TILE = 2048
HALO = SUBLANES
MOD_ROWS = SUBLANES
MOD_COL_TILE = 1536
VMEM_LIMIT = 56 * 1024 * 1024

F32 = jnp.float32
BF16 = jnp.bfloat16
HIGHEST = lax.Precision.HIGHEST
NEG_INF = float("-inf")
NT_DIMS = (((1,), (1,)), ((), ()))


def _cparams(*sem):
    return pltpu.CompilerParams(dimension_semantics=sem, vmem_limit_bytes=VMEM_LIMIT)


def _group_of_row(row0, seq):
    return jnp.where(row0 < seq, 0, jnp.where(row0 < 2 * seq, 1, 2))


def _silu(x):
    return x * jax.nn.sigmoid(x)


def _split_bf16(a):
    hi = a.astype(BF16)
    lo = (a - hi.astype(F32)).astype(BF16)
    return hi, lo


def _bf16_pair_word(x):
    hi = lax.bitcast_convert_type(x.astype(BF16).astype(F32), jnp.uint32)
    return hi | (hi >> 16)


def _pair_word_rows(row):
    tile = pltpu.bitcast(jnp.broadcast_to(row, (SUBLANES, row.shape[1])), BF16)
    return jnp.concatenate([tile] * (N_KEYS // tile.shape[0]), axis=0)


def _dot_split(a, b):
    ah, al = _split_bf16(a)
    bh, bl = _split_bf16(b)
    a4 = jnp.concatenate([ah, al, ah, al], axis=1)
    b4 = jnp.concatenate([bh, bh, bl, bl], axis=0)
    return jnp.dot(a4, b4, preferred_element_type=F32)


def _mod_kernel(c_ref, w_ref, b_ref, o_ref):
    o_ref[0] = jnp.dot(_silu(c_ref[...]), w_ref[0], preferred_element_type=F32) + b_ref[0]


def _modulation(cvec, w_mod, b_mod):
    tn = MOD_COL_TILE
    return pl.pallas_call(
        _mod_kernel,
        grid=(DEPTH, 6 * D_MODEL // tn),
        in_specs=[
            pl.BlockSpec((MOD_ROWS, D_MODEL), lambda l, j: (0, 0)),
            pl.BlockSpec((1, D_MODEL, tn), lambda l, j: (l, 0, j)),
            pl.BlockSpec((1, 1, tn), lambda l, j: (l, 0, j)),
        ],
        out_specs=pl.BlockSpec((1, MOD_ROWS, tn), lambda l, j: (l, 0, j)),
        out_shape=jax.ShapeDtypeStruct((DEPTH, MOD_ROWS, 6 * D_MODEL), F32),
        compiler_params=_cparams("arbitrary", "arbitrary"),
        name="modulation",
    )(cvec, w_mod, b_mod.reshape(DEPTH, 1, 6 * D_MODEL))


def _norm_mod(x, nw, mod, k):
    ms = jnp.mean(x * x, axis=-1, keepdims=True)
    xn = x * lax.rsqrt(ms + EPS) * nw
    sh = mod[:, k * D_MODEL:(k + 1) * D_MODEL]
    sc = mod[:, (k + 1) * D_MODEL:(k + 2) * D_MODEL]
    return xn * (1 + sc) + sh


def _inproj_kernel(x_ref, modv_ref, nw_ref, w_ref, qkv_ref, z_ref, pin_ref, gd_ref, gp_ref, ba_ref, *, seq):
    g = _group_of_row(pl.program_id(0) * INPROJ_TILE, seq)
    mod = modv_ref[pl.ds(g, 1), :]
    h = _norm_mod(x_ref[...], nw_ref[...], mod, 0)
    y = jnp.dot(h.astype(BF16), w_ref[...], preferred_element_type=F32)
    o = 0
    for ref in (qkv_ref, z_ref, pin_ref, gd_ref, gp_ref, ba_ref):
        n = ref.shape[1]
        ref[...] = y[:, o:o + n]
        o += n


def _inproj(xall, modv, norm_w, w_in_r, seq):
    rows = xall.shape[0]
    widths = (3 * DN_WIDTH, DN_WIDTH, POOL_WIDTH, D_MODEL, D_MODEL, BA_PAD)
    return pl.pallas_call(
        functools.partial(_inproj_kernel, seq=seq),
        grid=(rows // INPROJ_TILE,),
        in_specs=[
            pl.BlockSpec((INPROJ_TILE, D_MODEL), lambda i: (i, 0)),
            pl.BlockSpec((MOD_ROWS, 6 * D_MODEL), lambda i: (0, 0)),
            pl.BlockSpec((1, D_MODEL), lambda i: (0, 0)),
            pl.BlockSpec((D_MODEL, IN_COLS_R), lambda i: (0, 0)),
        ],
        out_specs=[pl.BlockSpec((INPROJ_TILE, n), lambda i: (i, 0)) for n in widths],
        out_shape=[jax.ShapeDtypeStruct((rows, n), F32) for n in widths],
        compiler_params=_cparams("arbitrary"),
        name="inproj",
    )(xall, modv, norm_w.reshape(1, D_MODEL), w_in_r)


def _dn_inputs(cur, prev, nxt, cw, ba, alog, dtb):
    tile = cur.shape[0]
    ext = jnp.concatenate([prev, cur, nxt], axis=0)
    left = SHORT_CONV // 2
    y = None
    for j in range(SHORT_CONV):
        o = HALO - left + j
        term = ext[o:o + tile, :] * cw[j:j + 1, :]
        y = term if y is None else y + term
    y = _silu(y)
    qs, ks = [], []
    for hh in range(DN_HEADS):
        lo, hi = hh * DN_HEAD_DIM, (hh + 1) * DN_HEAD_DIM
        qh = y[:, lo:hi]
        kh = y[:, DN_WIDTH + lo:DN_WIDTH + hi]
        qs.append(qh * lax.rsqrt(jnp.sum(qh * qh, axis=-1, keepdims=True) + EPS) * (DN_HEAD_DIM ** -0.5))
        ks.append(kh * lax.rsqrt(jnp.sum(kh * kh, axis=-1, keepdims=True) + EPS))
    beta = jax.nn.sigmoid(ba)
    xs = ba + dtb
    softplus = jnp.maximum(xs, 0.0) + jnp.log(1.0 + jnp.exp(-jnp.abs(xs)))
    gdec = -jnp.exp(alog) * softplus
    col = lax.broadcasted_iota(jnp.int32, ba.shape, 1)
    bg = jnp.where(col < 2 * DN_HEADS, beta, jnp.where(col < 4 * DN_HEADS, gdec, 0.0))
    return qs, ks, y[:, 2 * DN_WIDTH:], bg


def _dnprep_kernel(cur_ref, prev_ref, next_ref, cw_ref, ba_ref, alog_ref, dtb_ref, wq_ref, u_ref, lk_ref, eg_ref,
                   q_scr, k_scr, v_scr, bg_scr, *, seq, ctx):
    c = DN_CHUNK
    tile = DN_PREP_CHUNKS * c
    row0 = pl.program_id(0) * tile
    nlat = BATCH * seq
    is_start = (row0 == 0) | (row0 == seq) | (row0 == nlat) | (row0 == nlat + ctx)
    row1 = row0 + tile
    is_end = (row1 == seq) | (row1 == nlat) | (row1 == nlat + ctx) | (row1 == nlat + BATCH * ctx)
    qs, ks, v_all, bg_all = _dn_inputs(cur_ref[...], jnp.where(is_start, 0.0, prev_ref[...]),
                                       jnp.where(is_end, 0.0, next_ref[...]), cw_ref[...], ba_ref[...],
                                       alog_ref[...], dtb_ref[...])
    for hh in range(DN_HEADS):
        q_scr[:, hh * DN_HEAD_DIM:(hh + 1) * DN_HEAD_DIM] = qs[hh]
        k_scr[:, hh * DN_HEAD_DIM:(hh + 1) * DN_HEAD_DIM] = ks[hh]
    v_scr[...] = v_all
    bg_scr[...] = bg_all
    ri = lax.broadcasted_iota(jnp.int32, (c, c), 0)
    ci = lax.broadcasted_iota(jnp.int32, (c, c), 1)
    dirs = ((ri >= ci, ri > ci, ci >= ri, c - 1), (ri <= ci, ri < ci, ci <= ri, 0))
    chains = []
    for j in range(DN_PREP_CHUNKS):
        rows = slice(j * c, (j + 1) * c)
        bg = bg_scr[rows, :]
        bgh, bgl = _split_bf16(bg)
        bgth, bgtl = _split_bf16(bg.T[0:4 * DN_HEADS, :])
        bg2 = jnp.concatenate([bgh, bgl], axis=0)
        bgt2 = jnp.concatenate([bgth, bgtl], axis=1)
        egs = []
        for d, (incl, strict, incl_t, last) in enumerate(dirs):
            m = incl.astype(BF16)
            mt = incl_t.astype(BF16)
            gc_all = jnp.dot(jnp.concatenate([m, m], axis=1), bg2, preferred_element_type=F32)
            gr_all = jnp.dot(bgt2, jnp.concatenate([mt, mt], axis=0), preferred_element_type=F32)
            for hh in range(DN_HEADS):
                lo, hi = hh * DN_HEAD_DIM, (hh + 1) * DN_HEAD_DIM
                ch = d * DN_HEADS + hh
                gcol = 2 * DN_HEADS + ch
                q = q_scr[rows, lo:hi]
                k = k_scr[rows, lo:hi]
                v = v_scr[rows, lo:hi]
                beta = bg[:, ch:ch + 1]
                gc = gc_all[:, gcol:gcol + 1]
                gr = gr_all[gcol:gcol + 1, :]
                glast = gc_all[last:last + 1, gcol:gcol + 1]
                decay = jnp.where(incl, jnp.exp(jnp.minimum(gc - gr, 0.0)), 0.0)
                kk = lax.dot_general(k, k, NT_DIMS, preferred_element_type=F32)
                qk = lax.dot_general(q, k, NT_DIMS, preferred_element_type=F32)
                egc = jnp.exp(gc)
                cols = slice(ch * DN_HEAD_DIM, (ch + 1) * DN_HEAD_DIM)
                wq_ref[0, j, c:2 * c, cols] = (q * egc).astype(BF16)
                lk_ref[0, j, ch, 0:c, :] = jnp.where(incl, qk * decay, 0.0).astype(BF16)
                lk_ref[0, j, ch, c:, :] = (k * jnp.exp(glast - gc)).T.astype(BF16)
                egs.append(jnp.broadcast_to(jnp.exp(glast), (1, LANES)))
                chains.append((j, cols, jnp.where(strict, -(beta * kk * decay), 0.0),
                               jnp.concatenate([k * (beta * egc), v * beta], axis=1)))
        eg_ref[0, j] = jnp.concatenate(egs, axis=0)
    for g0 in range(0, len(chains), DN_LOCKSTEP):
        group = chains[g0:g0 + DN_LOCKSTEP]
        ps = [ch[2] for ch in group]
        eye = (ri == ci).astype(F32)
        xs = [eye + p for p in ps]
        for _ in range(int(np.log2(c)) - 1):
            ps = [_dot_split(p, p) for p in ps]
            xs = [x + _dot_split(x, p) for x, p in zip(xs, ps)]
        ys = [_dot_split(x, ch[3]) for x, ch in zip(xs, group)]
        for (j, cols, _, _), y in zip(group, ys):
            wq_ref[0, j, 0:c, cols] = y[:, :DN_HEAD_DIM].astype(BF16)
            u_ref[0, j, :, cols] = y[:, DN_HEAD_DIM:]


def _dnprep(qkv, ba, conv_w, a_log, dt_bias, seq, ctx):
    rows = qkv.shape[0]
    c = DN_CHUNK
    n = DN_PREP_CHUNKS
    tile = n * c
    nct, nlt = ctx // c, seq // c
    assert nct % n == 0 and nlt % n == 0
    nch = nct + nlt
    nh = tile // HALO
    last = rows // HALO - 1
    pad = jnp.zeros((2 * DN_HEADS,), F32)
    tail = jnp.zeros((BA_PAD - 4 * DN_HEADS,), F32)
    alog = jnp.concatenate([pad, a_log.reshape(-1), tail]).reshape(1, BA_PAD)
    dtb = jnp.concatenate([pad, dt_bias.reshape(-1), tail]).reshape(1, BA_PAD)
    w3 = 3 * DN_WIDTH

    def seq_pos(i):
        i = i * n
        lat = i < BATCH * nlt
        j = i - BATCH * nlt
        return jnp.where(lat, i // nlt, j // nct), jnp.where(lat, nct + i % nlt, j % nct) // n

    def out_spec(*tail_dims):
        return pl.BlockSpec((1, n) + tail_dims, lambda i: seq_pos(i) + (0,) * len(tail_dims))

    width = DN_CHAINS * DN_HEAD_DIM
    return pl.pallas_call(
        functools.partial(_dnprep_kernel, seq=seq, ctx=ctx),
        grid=(rows // tile,),
        in_specs=[pl.BlockSpec((tile, w3), lambda i: (i, 0)),
                  pl.BlockSpec((HALO, w3), lambda i: (jnp.maximum(i * nh - 1, 0), 0)),
                  pl.BlockSpec((HALO, w3), lambda i: (jnp.minimum((i + 1) * nh, last), 0)),
                  pl.BlockSpec((SHORT_CONV, w3), lambda i: (0, 0)),
                  pl.BlockSpec((tile, BA_PAD), lambda i: (i, 0)),
                  pl.BlockSpec((1, BA_PAD), lambda i: (0, 0)),
                  pl.BlockSpec((1, BA_PAD), lambda i: (0, 0))],
        out_specs=[out_spec(2 * c, width), out_spec(c, width), out_spec(DN_CHAINS, c + DN_HEAD_DIM, c),
                   out_spec(DN_CHAINS, LANES)],
        out_shape=[jax.ShapeDtypeStruct((BATCH, nch, 2 * c, width), BF16),
                   jax.ShapeDtypeStruct((BATCH, nch, c, width), F32),
                   jax.ShapeDtypeStruct((BATCH, nch, DN_CHAINS, c + DN_HEAD_DIM, c), BF16),
                   jax.ShapeDtypeStruct((BATCH, nch, DN_CHAINS, LANES), F32)],
        scratch_shapes=[pltpu.VMEM((tile, DN_WIDTH), F32)] * 3 + [pltpu.VMEM((tile, BA_PAD), F32)],
        compiler_params=_cparams("arbitrary"),
        name="dnprep",
    )(qkv, qkv, qkv, conv_w, ba, alog, dtb)


def _dnscan_kernel(wqf_ref, uf_ref, lkf_ref, egf_ref, wqb_ref, ub_ref, lkb_ref, egb_ref, of_ref, ob_ref, s_ref):
    @pl.when(pl.program_id(0) == 0)
    def _():
        s_ref[...] = jnp.zeros_like(s_ref)

    c = DN_CHUNK
    dirs = ((wqf_ref, uf_ref, lkf_ref, egf_ref, of_ref), (wqb_ref, ub_ref, lkb_ref, egb_ref, ob_ref))
    chains = [(d, b, hh) for d in range(2) for b in range(BATCH) for hh in range(DN_HEADS)]
    sidx = lambda d, b, hh: (b * 2 + d) * DN_HEADS + hh
    cols = lambda hh: slice(hh * DN_HEAD_DIM, (hh + 1) * DN_HEAD_DIM)
    for j in range(DN_SCAN_CHUNKS):
        pos = (j, DN_SCAN_CHUNKS - 1 - j)
        ss = [s_ref[sidx(*ch)] for ch in chains]
        r1 = [jnp.dot(dirs[d][0][b, pos[d], :, cols(hh)], s.astype(BF16), preferred_element_type=F32)
              for (d, b, hh), s in zip(chains, ss)]
        vn = [dirs[d][1][b, pos[d], :, cols(hh)] - r[:c] for (d, b, hh), r in zip(chains, r1)]
        r2 = [jnp.dot(dirs[d][2][b, pos[d], hh], v.astype(BF16), preferred_element_type=F32)
              for (d, b, hh), v in zip(chains, vn)]
        for (d, b, hh), s, a1, a2 in zip(chains, ss, r1, r2):
            dirs[d][4][b, pos[d], :, cols(hh)] = a1[c:] + a2[:c]
            row = d * DN_HEADS + hh
            s_ref[sidx(d, b, hh)] = s * dirs[d][3][b, pos[d], row:row + 1, :] + a2[c:]


def _dnscan(wq, u, lk, eg, seq, ctx):
    c = DN_CHUNK
    n = DN_SCAN_CHUNKS
    nct, nlt = ctx // c, seq // c
    assert nct % n == 0 and nlt % n == 0
    nch = nct + nlt
    nblk, nctb = nch // n, nct // n

    def bwd(s):
        return jnp.where(s < nctb, nctb - 1 - s, nctb + nblk - 1 - s)

    def specs(pos, d):
        return [pl.BlockSpec((BATCH, n, 2 * c, DN_WIDTH), lambda s: (0, pos(s), 0, d)),
                pl.BlockSpec((BATCH, n, c, DN_WIDTH), lambda s: (0, pos(s), 0, d)),
                pl.BlockSpec((BATCH, n, DN_HEADS, c + DN_HEAD_DIM, c), lambda s: (0, pos(s), d, 0, 0)),
                pl.BlockSpec((BATCH, n, DN_CHAINS, LANES), lambda s: (0, pos(s), 0, 0))]

    fwd = lambda s: s
    return pl.pallas_call(
        _dnscan_kernel,
        grid=(nblk,),
        in_specs=specs(fwd, 0) + specs(bwd, 1),
        out_specs=[pl.BlockSpec((BATCH, n, c, DN_WIDTH), lambda s: (0, s, 0, 0)),
                   pl.BlockSpec((BATCH, n, c, DN_WIDTH), lambda s: (0, bwd(s), 0, 0))],
        out_shape=[jax.ShapeDtypeStruct((BATCH, nch, c, DN_WIDTH), F32)] * 2,
        scratch_shapes=[pltpu.VMEM((BATCH * DN_CHAINS, DN_HEAD_DIM, DN_HEAD_DIM), F32)],
        compiler_params=_cparams("arbitrary"),
        name="dnscan",
    )(wq, u, lk, eg, wq, u, lk, eg)


def _pool_matrices(tile, seg):
    t = np.arange(tile)
    p = t % seg
    mats = []
    for win in POOL_WINDOWS:
        lo = np.clip(p - win // 2, 0, seg)
        hi = np.clip(p + win - win // 2, 0, seg)
        same = (t[:, None] // seg) == (t[None, :] // seg)
        inside = same & (p[None, :] >= lo[:, None]) & (p[None, :] < hi[:, None])
        mats.append(inside / (hi - lo)[:, None].astype(np.float64) - np.eye(tile))
    return np.stack(mats).astype(np.float32)


def _merge_kernel(x_ref, of_ref, ob_ref, z_ref, pin_ref, gd_ref, gp_ref, modv_ref, on_ref, pm_ref, pw_ref, ps_ref,
                  wud_ref, wup_ref, wo_ref, o_ref, *, seq):
    g = _group_of_row(pl.program_id(0) * ROW_TILE, seq)
    mod = modv_ref[pl.ds(g, 1), :]
    gate = mod[:, 2 * D_MODEL:3 * D_MODEL]
    o = (of_ref[0] + ob_ref[0]).reshape(ROW_TILE, DN_WIDTH)
    z = z_ref[...]
    onw = on_ref[...]
    ys = []
    for hh in range(DN_HEADS):
        lo, hi = hh * DN_HEAD_DIM, (hh + 1) * DN_HEAD_DIM
        oh = o[:, lo:hi]
        ys.append(oh * lax.rsqrt(jnp.mean(oh * oh, axis=-1, keepdims=True) + EPS) * onw * _silu(z[:, lo:hi]))
    y_dn = jnp.concatenate(ys, axis=1)
    pin = pin_ref[...]
    yp = []
    for gi in range(POOL_GROUPS):
        lo, hi = gi * POOL_GW, (gi + 1) * POOL_GW
        pooled = jnp.dot(pm_ref[0, gi], pin[:, lo:hi], precision=HIGHEST, preferred_element_type=F32)
        yp.append(jnp.dot(pooled, pw_ref[gi], preferred_element_type=F32))
    y_pool = jnp.concatenate(yp, axis=1) * ps_ref[...]
    m = (jax.nn.sigmoid(gd_ref[...]) * jnp.dot(y_dn.astype(BF16), wud_ref[...], preferred_element_type=F32)
         + jax.nn.sigmoid(gp_ref[...]) * jnp.dot(y_pool.astype(BF16), wup_ref[...], preferred_element_type=F32))
    out = jnp.dot(m.astype(BF16), wo_ref[...], preferred_element_type=F32)
    o_ref[...] = x_ref[...] + gate * out


def _merge(xall, o_f, o_b, z, pin, gd, gp, modv, onorm, pool_w, pool_scale, wud, wup, wo, seq, ctx, rows_out):
    assert ctx == ROW_TILE and seq % ROW_TILE == 0
    lat_tiles = BATCH * seq // ROW_TILE
    tiles_per_seq = seq // ROW_TILE
    cpt = ROW_TILE // DN_CHUNK
    pm = jnp.asarray(np.stack([_pool_matrices(ROW_TILE, GRID_W), _pool_matrices(ROW_TILE, ctx)]))
    row = lambda n: pl.BlockSpec((ROW_TILE, n), lambda i: (i, 0))
    full = lambda *s: pl.BlockSpec(s, lambda i: (0,) * len(s))

    def o_pos(i):
        lat = i < lat_tiles
        return (jnp.where(lat, i // tiles_per_seq, i - lat_tiles), jnp.where(lat, 1 + i % tiles_per_seq, 0), 0, 0)

    o_spec = pl.BlockSpec((1, cpt, DN_CHUNK, DN_WIDTH), o_pos)
    return pl.pallas_call(
        functools.partial(_merge_kernel, seq=seq),
        grid=(rows_out // ROW_TILE,),
        in_specs=[
            row(D_MODEL), o_spec, o_spec, row(DN_WIDTH), row(POOL_WIDTH), row(D_MODEL), row(D_MODEL),
            full(MOD_ROWS, 6 * D_MODEL), full(1, DN_HEAD_DIM),
            pl.BlockSpec((1, POOL_GROUPS, ROW_TILE, ROW_TILE), lambda i: (jnp.where(i >= lat_tiles, 1, 0), 0, 0, 0)),
            full(POOL_GROUPS, POOL_GW, POOL_GW), full(1, POOL_WIDTH),
            full(DN_WIDTH, D_MODEL), full(POOL_WIDTH, D_MODEL), full(D_MODEL, D_MODEL),
        ],
        out_specs=row(D_MODEL),
        out_shape=jax.ShapeDtypeStruct((rows_out, D_MODEL), F32),
        compiler_params=_cparams("arbitrary"),
        name="merge",
    )(xall, o_f, o_b, z, pin, gd, gp, modv, onorm.reshape(1, DN_HEAD_DIM), pm, pool_w,
      pool_scale.reshape(1, POOL_WIDTH), wud, wup, wo)


def _col_max(x):
    return jnp.max(x, axis=0, keepdims=True)


def _col_min(x):
    return jnp.min(x, axis=0, keepdims=True)


def _oddeven_merge(lo, hi, r):
    step = r * 2
    if step < hi - lo:
        yield from _oddeven_merge(lo, hi, step)
        yield from _oddeven_merge(lo + r, hi, step)
        yield from [(i, i + r) for i in range(lo + r, hi - r, step)]
    else:
        yield (lo, lo + r)


def _oddeven_merge_sort(lo, hi):
    if hi - lo >= 1:
        mid = lo + (hi - lo) // 2
        yield from _oddeven_merge_sort(lo, mid)
        yield from _oddeven_merge_sort(mid + 1, hi)
        yield from _oddeven_merge(lo, hi, 1)


def _exchange(v, i, j):
    v[i], v[j] = jnp.maximum(v[i], v[j]), jnp.minimum(v[i], v[j])


def _sorted_top_many(block_lists):
    k = PEER_TOPK
    vs = [list(blocks) for blocks in block_lists]
    for i, j in _oddeven_merge_sort(0, k - 1):
        for v in vs:
            _exchange(v, i, j)
    shift = SUBLANES // 2
    while shift >= 1:
        ws = [[pltpu.roll(x, shift, 0) for x in v] for v in vs]
        vs = [[jnp.maximum(v[j], w[k - 1 - j]) for j in range(k)] for v, w in zip(vs, ws)]
        stride = k // 2
        while stride >= 1:
            for i in range(k):
                if i & stride == 0:
                    for v in vs:
                        _exchange(v, i, i + stride)
            stride //= 2
        shift //= 2
    return vs


def _sublane_total(x):
    shift = SUBLANES // 2
    while shift >= 1:
        x = x + pltpu.roll(x, shift, 0)
        shift //= 2
    return x


def _unambiguous(blocks, top):
    ok = top[0] > top[1]
    for a in range(1, PEER_TOPK - 1):
        ok = ok & (top[a] > top[a + 1])
    ge = None
    for blk in blocks:
        one = jnp.where(blk >= top[PEER_TOPK - 1], 1.0, 0.0)
        ge = one if ge is None else ge + one
    return ok & (_sublane_total(ge) == float(PEER_TOPK))


def _top_select(scores, iota):
    n = float(scores[0].shape[0])
    cur = list(scores)
    idxs = [[] for _ in cur]
    for _ in range(PEER_TOPK):
        for j in range(len(cur)):
            m = _col_max(cur[j])
            cand = jnp.where(cur[j] == m, iota, n)
            idx = _col_min(cand)
            cur[j] = jnp.where(cand == idx, NEG_INF, cur[j])
            idxs[j].append(idx)
    return [jnp.concatenate(i, axis=0) for i in idxs]


def _peer_route_kernel(x_ref, modv_ref, nw_ref, wq_ref, keys_ref, hn_ref, n1_ref, e1_ref, r2_ref, e2_ref,
                       q_scr, *, seq):
    g = _group_of_row(pl.program_id(0) * ROW_TILE, seq)
    mod = modv_ref[pl.ds(g, 1), :]
    hn = _norm_mod(x_ref[...], nw_ref[...], mod, 3).astype(BF16)
    hn_ref[...] = hn
    q = jnp.dot(hn, wq_ref[...], preferred_element_type=F32)
    nsub = ROW_TILE // LANES
    for j in range(2 * PEER_HEADS):
        for t in range(nsub):
            q_scr[j, t] = q[t * LANES:(t + 1) * LANES, j * PEER_HALF:(j + 1) * PEER_HALF]
    k = PEER_TOPK
    nblk = N_KEYS // SUBLANES
    iota_k = lax.broadcasted_iota(jnp.int32, (N_KEYS, LANES), 0).astype(F32)
    iota_a = lax.broadcasted_iota(jnp.int32, (k, LANES), 0).astype(F32)
    iota_s = lax.broadcasted_iota(jnp.int32, (SUBLANES, LANES), 0).astype(F32)
    split = lambda s: [s[SUBLANES * j:SUBLANES * (j + 1), :] for j in range(nblk)]
    rep = lambda row: jnp.broadcast_to(row, (SUBLANES, LANES))

    def head(hh, carry):
        subs = range(nsub)
        s1 = [lax.dot_general(keys_ref[2 * hh], q_scr[2 * hh, t], NT_DIMS, preferred_element_type=F32) for t in subs]
        s2 = [lax.dot_general(keys_ref[2 * hh + 1], q_scr[2 * hh + 1, t], NT_DIMS, preferred_element_type=F32)
              for t in subs]
        b1, b2 = [split(x) for x in s1], [split(x) for x in s2]
        tops = _sorted_top_many(b1 + b2)
        top1, top2 = tops[:nsub], tops[nsub:]
        c = [jnp.concatenate([x[0:1, :] for x in top], axis=0) for top in top1]
        d = [jnp.concatenate([x[0:1, :] for x in top], axis=0) for top in top2]

        def grid_cells(c, d):
            cells = [c[0:1, :] + d[0:SUBLANES, :], c[0:1, :] + d[SUBLANES:k, :]]
            for a in range(1, SUBLANES):
                cells.append(jnp.where(iota_s < float(k // (a + 1)), c[a:a + 1, :] + d[0:SUBLANES, :], NEG_INF))
            cells.append(c[SUBLANES:k, :] + d[0:1, :])
            return cells + [jnp.full((SUBLANES, LANES), NEG_INF, F32)] * (k - len(cells))

        cuts = [x[k - 1][0:1, :] for x in _sorted_top_many([grid_cells(c[t], d[t]) for t in subs])]
        cnt_cut = [jnp.zeros((k, LANES), F32) for _ in subs]
        for b in range(k):
            cnt_cut = [acc + jnp.where(c[t] + d[t][b:b + 1, :] >= cuts[t], 1.0, 0.0) for t, acc in enumerate(cnt_cut)]

        def by_merge():
            cnts = [jnp.zeros((k, LANES), F32) for _ in subs]
            for _ in range(k):
                for t in subs:
                    dn = jnp.full((k, LANES), NEG_INF, F32)
                    for b in range(k):
                        dn = jnp.where(cnts[t] == float(b), d[t][b:b + 1, :], dn)
                    f = c[t] + dn
                    cand = jnp.where(f == _col_max(f), iota_a, float(k))
                    cnts[t] = cnts[t] + (cand == _col_min(cand)).astype(F32)
            return tuple(cnts)

        exact_cut = jnp.all(jnp.concatenate([jnp.sum(x, axis=0, keepdims=True) for x in cnt_cut], axis=0) == float(k))
        cnt = lax.cond(exact_cut, lambda: tuple(cnt_cut), by_merge)
        zsum = []
        for t in subs:
            e1c = jnp.exp(c[t] - c[t][0:1, :])
            e2d = jnp.exp(d[t] - d[t][0:1, :])
            part = jnp.zeros((k, LANES), F32)
            for b in range(k):
                part = part + jnp.where(cnt[t] > float(b), e2d[b:b + 1, :], 0.0)
            zsum.append(jnp.sum(e1c * part, axis=0, keepdims=True))

        def by_value():
            n1b = [[jnp.zeros((SUBLANES, LANES), F32)] * nblk for _ in subs]
            r2b = [[jnp.full((SUBLANES, LANES), float(k), F32)] * nblk for _ in subs]
            for a in range(k):
                for t in subs:
                    cnt_a = rep(cnt[t][a:a + 1, :])
                    n1b[t] = [jnp.where(blk == top1[t][a], cnt_a, acc) for blk, acc in zip(b1[t], n1b[t])]
                    r2b[t] = [jnp.where(blk == top2[t][a], float(a), acc) for blk, acc in zip(b2[t], r2b[t])]
            return tuple(jnp.concatenate(x, axis=0) for x in n1b + r2b)

        def by_index():
            idx = _top_select(s1 + s2, iota_k)
            out1, out2 = [], []
            for t in subs:
                n1 = jnp.zeros((N_KEYS, LANES), F32)
                r2 = jnp.full((N_KEYS, LANES), float(k), F32)
                for a in range(k):
                    n1 = jnp.where(iota_k == idx[t][a:a + 1, :], cnt[t][a:a + 1, :], n1)
                    r2 = jnp.where(iota_k == idx[nsub + t][a:a + 1, :], float(a), r2)
                out1.append(n1)
                out2.append(r2)
            return tuple(out1 + out2)

        clean = None
        for t in subs:
            ok = _unambiguous(b1[t], top1[t]) & _unambiguous(b2[t], top2[t])
            clean = ok if clean is None else clean & ok
        ranks = lax.cond(jnp.all(clean), by_value, by_index)
        for t in subs:
            lanes = slice(t * LANES, (t + 1) * LANES)
            n1_ref[hh, 0, :, lanes] = _bf16_pair_word(ranks[t])
            e1_ref[hh, 0, :, lanes] = _bf16_pair_word(0.5 * jnp.exp(s1[t] - c[t][0:1, :]) / zsum[t])
            r2_ref[hh, 0, :, lanes] = ranks[nsub + t].astype(BF16)
            e2_ref[hh, 0, :, lanes] = jnp.exp(s2[t] - d[t][0:1, :]).astype(BF16)
        return carry

    lax.fori_loop(0, PEER_HEADS, head, 0)


def _peer_route(xall, modv, norm_w, wq, keys, seq, rows_out):
    row = lambda n: pl.BlockSpec((ROW_TILE, n), lambda i: (i, 0))
    full = lambda *s: pl.BlockSpec(s, lambda i: (0,) * len(s))
    tab = pl.BlockSpec((PEER_HEADS, 1, N_KEYS, ROW_TILE), lambda i: (0, i, 0, 0))
    tab_shape = lambda dt: jax.ShapeDtypeStruct((PEER_HEADS, rows_out // ROW_TILE, N_KEYS, ROW_TILE), dt)
    return pl.pallas_call(
        functools.partial(_peer_route_kernel, seq=seq),
        grid=(rows_out // ROW_TILE,),
        in_specs=[row(D_MODEL), full(MOD_ROWS, 6 * D_MODEL), full(1, D_MODEL), full(D_MODEL, PEER_HEADS * PEER_QDIM),
                  full(2 * PEER_HEADS, N_KEYS, PEER_HALF)],
        out_specs=[row(D_MODEL), tab, tab, tab, tab],
        out_shape=[jax.ShapeDtypeStruct((rows_out, D_MODEL), BF16), tab_shape(jnp.uint32), tab_shape(jnp.uint32),
                   tab_shape(BF16), tab_shape(BF16)],
        scratch_shapes=[pltpu.VMEM((2 * PEER_HEADS, ROW_TILE // LANES, LANES, PEER_HALF), F32)],
        compiler_params=_cparams("arbitrary"),
        name="peer_route",
    )(xall, modv, norm_w.reshape(1, D_MODEL), wq, keys.reshape(2 * PEER_HEADS, N_KEYS, PEER_HALF))


PEER_PIPE_LAG = 2
PEER_MXU_BLOCK = 256


def _peer_tile(n, lag, n_tiles, n_exp):
    m = jnp.clip(n - lag, 0, n_tiles - 1)
    return m // n_exp, m % n_exp


def _peer_dense_kernel(x_ref, modv_ref, hn_ref, u_ref, vt_ref, n1_ref, e1_ref, r2_ref, e2_ref, fn_ref, o_ref,
                       acc_ref, a0_ref, a1_ref, act0_ref, act1_ref, hn_scr, *, seq, final, n_tiles, n_exp):
    n = pl.program_id(0)
    t_c, e_c = _peer_tile(n, PEER_PIPE_LAG, n_tiles, n_exp)

    @pl.when(n == 0)
    def _():
        for ref in (acc_ref, a0_ref, a1_ref, act0_ref, act1_ref):
            ref[...] = jnp.zeros_like(ref)

    @pl.when(_peer_tile(n, 0, n_tiles, n_exp)[1] == 0)
    def _():
        hn_scr[...] = hn_ref[...]

    def step(a_next, a_cur, act_next, act_cur):
        zero = jnp.zeros((), BF16)
        blk = PEER_MXU_BLOCK

        def stage_a(rb, cb):
            rows, cols = slice(rb * blk, (rb + 1) * blk), slice(cb * blk, (cb + 1) * blk)
            a_next[rows, cols] = lax.dot_general(u_ref[rows, :], hn_scr[cols, :], NT_DIMS,
                                                 preferred_element_type=F32)

        def stage_b(t, i):
            lanes = slice(t * ROW_TILE, (t + 1) * ROW_TILE)
            wsum = None
            for hh in range(PEER_HEADS):
                keep = r2_ref[hh, t] < _pair_word_rows(n1_ref[hh, t, i:i + 1, :])
                term = jnp.where(keep, e2_ref[hh, t], zero) * _pair_word_rows(e1_ref[hh, t, i:i + 1, :])
                wsum = term if wsum is None else wsum + term
            rows = slice(i * N_KEYS, (i + 1) * N_KEYS)
            a = a_cur[rows, lanes]
            gelu = a * (1.0 + lax.erf(a * (2.0 ** -0.5)))
            act_next[rows, lanes] = gelu.astype(BF16) * wsum

        def stage_c(rb, cb):
            rows, cols = slice(rb * blk, (rb + 1) * blk), slice(cb * blk, (cb + 1) * blk)
            prev = jnp.where(e_c == 0, 0.0, acc_ref[rows, cols])
            acc_ref[rows, cols] = prev + jnp.dot(vt_ref[rows, :], act_cur[:, cols], preferred_element_type=F32)

        a_pieces = [(rb, cb) for cb in range(PEER_TOK_TILE // blk) for rb in range(PEER_EXP_TILE // blk)]
        c_pieces = [(rb, cb) for cb in range(PEER_TOK_TILE // blk) for rb in range(D_MODEL // blk)]
        b_pieces = [(t, i) for t in range(PEER_TOK_TILE // ROW_TILE) for i in range(PEER_EXP_TILE // N_KEYS)]
        a_per_c = len(a_pieces) // len(c_pieces)
        b_per_c = len(b_pieces) // len(c_pieces)
        assert a_per_c * len(c_pieces) == len(a_pieces) and b_per_c * len(c_pieces) == len(b_pieces)
        b_iter = iter(b_pieces)
        for k, c_piece in enumerate(c_pieces):
            mxu = [(stage_a, p) for p in a_pieces[k * a_per_c:(k + 1) * a_per_c]]
            mxu.insert(1, (stage_c, c_piece))
            quota = [b_per_c // len(mxu) + (1 if m < b_per_c % len(mxu) else 0) for m in range(len(mxu))]
            for (fn, piece), nb in zip(mxu, quota):
                fn(*piece)
                for _ in range(nb):
                    stage_b(*next(b_iter))

    step(a0_ref, a1_ref, act0_ref, act1_ref)
    a1_ref[...] = a0_ref[...]

    @pl.when(n >= 0)
    def _():
        act1_ref[...] = act0_ref[...]

    @pl.when((e_c == n_exp - 1) & (n >= PEER_PIPE_LAG))
    def _():
        g = _group_of_row(t_c * PEER_TOK_TILE, seq)
        mod = modv_ref[pl.ds(g, 1), :]
        y = x_ref[...] + mod[:, 5 * D_MODEL:6 * D_MODEL] * acc_ref[...].T
        if final:
            y = y * lax.rsqrt(jnp.mean(y * y, axis=-1, keepdims=True) + EPS) * fn_ref[...]
        o_ref[...] = y


def _peer_dense(xall, modv, hn, u_bf, vt_bf, n1, e1, r2, e2, final_norm, seq, rows_out, final):
    tt, et = PEER_TOK_TILE, PEER_EXP_TILE
    n_first = et // N_KEYS
    n_exp = N_EXPERTS // et
    n_tiles = (rows_out // tt) * n_exp
    tile = lambda lag: (lambda n: _peer_tile(n, lag, n_tiles, n_exp))
    ta, tb, tc = tile(0), tile(1), tile(PEER_PIPE_LAG)
    full = lambda *s: pl.BlockSpec(s, lambda n: (0,) * len(s))
    per_first = pl.BlockSpec((PEER_HEADS, tt // ROW_TILE, n_first, ROW_TILE), lambda n: (0, tb(n)[0], tb(n)[1], 0))
    per_second = pl.BlockSpec((PEER_HEADS, tt // ROW_TILE, N_KEYS, ROW_TILE), lambda n: (0, tb(n)[0], 0, 0))
    return pl.pallas_call(
        functools.partial(_peer_dense_kernel, seq=seq, final=final, n_tiles=n_tiles, n_exp=n_exp),
        grid=(n_tiles + PEER_PIPE_LAG,),
        in_specs=[pl.BlockSpec((tt, D_MODEL), lambda n: (tc(n)[0], 0)),
                  full(MOD_ROWS, 6 * D_MODEL),
                  pl.BlockSpec((tt, D_MODEL), lambda n: (ta(n)[0], 0)),
                  pl.BlockSpec((et, D_MODEL), lambda n: (ta(n)[1], 0)),
                  pl.BlockSpec((D_MODEL, et), lambda n: (0, tc(n)[1])),
                  per_first, per_first, per_second, per_second, full(1, D_MODEL)],
        out_specs=pl.BlockSpec((tt, D_MODEL), lambda n: (tc(n)[0], 0)),
        out_shape=jax.ShapeDtypeStruct((rows_out, D_MODEL), F32),
        scratch_shapes=[pltpu.VMEM((D_MODEL, tt), F32), pltpu.VMEM((et, tt), F32), pltpu.VMEM((et, tt), F32),
                        pltpu.VMEM((et, tt), BF16), pltpu.VMEM((et, tt), BF16), pltpu.VMEM((tt, D_MODEL), BF16)],
        compiler_params=_cparams("arbitrary"),
        name="peer_dense",
    )(xall, modv, hn, u_bf, vt_bf, n1, e1, r2, e2, final_norm.reshape(1, D_MODEL))


def _reorder_in_weight(w):
    s = np.cumsum((0, 3 * DN_WIDTH, DN_WIDTH, POOL_WIDTH, 2 * DN_HEADS, 2 * DN_HEADS, D_MODEL, D_MODEL))
    qkv, z, pin, b, a, gd, gp = (w[:, s[i]:s[i + 1]] for i in range(7))
    pad = jnp.zeros((w.shape[0], BA_PAD - 4 * DN_HEADS), w.dtype)
    return jnp.concatenate([qkv, z, pin, gd, gp, b, a, pad], axis=1).astype(BF16)


def _forward(x, c, ctx, c_ctx, w_mod, b_mod, norm_mix, w_in, conv_w, a_log, dt_bias, dn_out_norm, pool_w, pool_scale,
             w_up_dn, w_up_pool, w_out, norm_ffn, peer_wq, peer_keys, peer_u, peer_v, final_norm):
    seq, nctx = x.shape[1], ctx.shape[1]
    nlat = BATCH * seq
    xall = jnp.concatenate([x.reshape(nlat, D_MODEL), ctx.reshape(BATCH * nctx, D_MODEL)], axis=0)
    rows = xall.shape[0]
    cvec = jnp.concatenate([c, c_ctx[None, :], jnp.zeros((MOD_ROWS - BATCH - 1, D_MODEL), F32)], axis=0)
    modv_all = _modulation(cvec, w_mod, b_mod)
    for i in range(DEPTH):
        last = i == DEPTH - 1
        rows_out = nlat if last else rows
        modv = modv_all[i]
        qkv, z, pin, gd, gp, ba = _inproj(xall, modv, norm_mix[i], _reorder_in_weight(w_in[i]), seq)
        o_f, o_b = _dnscan(*_dnprep(qkv, ba, conv_w[i], a_log[i], dt_bias[i], seq, nctx), seq, nctx)
        xall = _merge(xall, o_f, o_b, z, pin, gd, gp, modv, dn_out_norm[i], pool_w[i], pool_scale[i],
                      w_up_dn[i].astype(BF16), w_up_pool[i].astype(BF16), w_out[i].astype(BF16), seq, nctx, rows_out)
        hn, n1, e1, r2, e2 = _peer_route(xall, modv, norm_ffn[i], peer_wq[i].astype(BF16), peer_keys[i], seq, rows_out)
        xall = _peer_dense(xall, modv, hn, peer_u[i].astype(BF16), peer_v[i].T.astype(BF16), n1, e1, r2, e2,
                           final_norm, seq, rows_out, last)
    return xall.reshape(BATCH, seq, D_MODEL)


def kernel(x, c, ctx, c_ctx, w_mod, b_mod, norm_mix, w_in, conv_w, a_log, dt_bias, dn_out_norm, pool_w, pool_scale, w_up_dn, w_up_pool, w_out, norm_ffn, peer_wq, peer_keys, peer_u, peer_v, final_norm):
    return _forward(x, c, ctx, c_ctx, w_mod, b_mod, norm_mix, w_in, conv_w, a_log, dt_bias, dn_out_norm, pool_w,
                    pool_scale, w_up_dn, w_up_pool, w_out, norm_ffn, peer_wq, peer_keys, peer_u, peer_v, final_norm)
```
